```python
import jax, jax.numpy as jnp
from jax import lax
import numpy as np

D_MODEL = 1024
BATCH = 16
SEQ = 4096
DEPTH = 4

CHUNK = 128
A_HEADS = 4
A_HEAD_DIM = 128
A_WIDTH = A_HEADS * A_HEAD_DIM
POOL_WINDOWS = (2, 4, 8, 16)
B_GROUP_DIM = 128
B_WIDTH = len(POOL_WINDOWS) * B_GROUP_DIM
C_WIDTH = 512
CONV_WIDTH = 3
N_BRANCH = 3
IN_SPLIT_SIZES = (A_WIDTH, A_WIDTH, B_WIDTH, C_WIDTH, C_WIDTH, C_WIDTH, N_BRANCH * D_MODEL)
IN_COLS = sum(IN_SPLIT_SIZES)
_FF_RAW = -(-8 * D_MODEL // 3)
D_FF = -(-_FF_RAW // 256) * 256
N_MOD = 6
EPS = 1e-6

kernel_name = "hybrid_gmlp_pool_conv_gated_trunk"


def rmsnorm(x, g):
    xf = x.astype(jnp.float32)
    xf = xf * lax.rsqrt(jnp.mean(xf * xf, axis=-1, keepdims=True) + EPS)
    return xf.astype(x.dtype) * g


def layernorm(x, g, b):
    xf = x.astype(jnp.float32)
    mu = jnp.mean(xf, axis=-1, keepdims=True)
    var = jnp.mean(jnp.square(xf - mu), axis=-1, keepdims=True)
    return ((xf - mu) * lax.rsqrt(var + EPS)).astype(x.dtype) * g + b


def modulate(h, shift, scale):
    return h * (1.0 + scale[:, None, :]) + shift[:, None, :]


def split_cols(z):
    parts, off = [], 0
    for n in IN_SPLIT_SIZES:
        parts.append(z[..., off:off + n])
        off += n
    return parts


def gmlp_branch(u, v, ln_g, ln_b, w_s, b_s):
    bsz, seq, _ = v.shape
    v = layernorm(v, ln_g, ln_b)
    vc = v.reshape(bsz, seq // CHUNK, CHUNK, A_HEADS, A_HEAD_DIM)
    mask = jnp.tril(jnp.ones((CHUNK, CHUNK), dtype=w_s.dtype))
    w = w_s * mask[None]
    s = jnp.einsum('hts,bnshd->bnthd', w, vc) + jnp.transpose(b_s)[None, None, :, :, None]
    return u * s.reshape(bsz, seq, A_WIDTH)


def pool_branch(xb, pool_w, pool_scale):
    bsz, seq, _ = xb.shape
    xf = xb.astype(jnp.float32)
    cs = jnp.pad(jnp.cumsum(xf, axis=1), ((0, 0), (1, 0), (0, 0)))
    pos = jnp.arange(seq)
    outs = []
    for g, win in enumerate(POOL_WINDOWS):
        sl = slice(g * B_GROUP_DIM, (g + 1) * B_GROUP_DIM)
        csg = cs[..., sl]
        lag = jnp.pad(csg, ((0, 0), (win - 1, 0), (0, 0)))[:, :seq]
        cnt = jnp.minimum(pos + 1, win).astype(jnp.float32)[None, :, None]
        outs.append((csg[:, 1:] - lag) / cnt - xf[..., sl])
    p = jnp.stack(outs, axis=2).astype(xb.dtype)
    y = jnp.einsum('bsgc,gcd->bsgd', p, pool_w).reshape(bsz, seq, B_WIDTH)
    return y * pool_scale


def conv_branch(bg, cg, h, conv_w):
    seq = h.shape[1]
    z = cg * h
    zp = jnp.pad(z, ((0, 0), (CONV_WIDTH - 1, 0), (0, 0)))
    y = conv_w[0] * zp[:, 0:seq]
    for k in range(1, CONV_WIDTH):
        y = y + conv_w[k] * zp[:, k:k + seq]
    return bg * y


def _fwd_setup_inputs(seed: int = 0) -> dict:
    key = jax.random.key(seed)
    ks = jax.random.split(key, 24)
    f32 = jnp.float32
    nrm = lambda k, shape, scale: (jax.random.normal(k, shape, f32) * scale).astype(f32)
    L, D = DEPTH, D_MODEL
    return {
        "x": nrm(ks[0], (BATCH, SEQ, D), 1.0),
        "c": nrm(ks[1], (BATCH, D), 1.0),
        "w_mod": nrm(ks[2], (L, D, N_MOD * D), 0.5 * D ** -0.5),
        "b_mod": nrm(ks[3], (L, N_MOD * D), 0.02),
        "g_mix": 1.0 + nrm(ks[4], (L, D), 0.05),
        "w_in": nrm(ks[5], (L, D, IN_COLS), D ** -0.5),
        "gm_ln_g": 1.0 + nrm(ks[6], (L, A_WIDTH), 0.05),
        "gm_ln_b": nrm(ks[7], (L, A_WIDTH), 0.02),
        "gm_w_s": nrm(ks[8], (L, A_HEADS, CHUNK, CHUNK), CHUNK ** -0.5),
        "gm_b_s": 1.0 + nrm(ks[9], (L, A_HEADS, CHUNK), 0.05),
        "w_pa": nrm(ks[10], (L, A_WIDTH, D), A_WIDTH ** -0.5),
        "pool_w": nrm(ks[11], (L, len(POOL_WINDOWS), B_GROUP_DIM, B_GROUP_DIM), B_GROUP_DIM ** -0.5),
        "pool_scale": 1.0 + nrm(ks[12], (L, B_WIDTH), 0.1),
        "w_pb": nrm(ks[13], (L, B_WIDTH, D), B_WIDTH ** -0.5),
        "conv_w": nrm(ks[14], (L, CONV_WIDTH, C_WIDTH), CONV_WIDTH ** -0.5),
        "w_pc": nrm(ks[15], (L, C_WIDTH, D), C_WIDTH ** -0.5),
        "w_o": nrm(ks[16], (L, D, D), D ** -0.5),
        "g_ffn": 1.0 + nrm(ks[17], (L, D), 0.05),
        "w_13": nrm(ks[18], (L, D, 2 * D_FF), D ** -0.5),
        "w_2": nrm(ks[19], (L, D_FF, D), D_FF ** -0.5),
        "g_final": 1.0 + nrm(ks[20], (D,), 0.05),
    }


def _fwd_reference(x, c, w_mod, b_mod, g_mix, w_in, gm_ln_g, gm_ln_b, gm_w_s, gm_b_s, w_pa,
              pool_w, pool_scale, w_pb, conv_w, w_pc, w_o, g_ffn, w_13, w_2, g_final):
    bsz, seq, d = x.shape
    c_act = jax.nn.silu(c)
    for l in range(DEPTH):
        mod = (c_act @ w_mod[l] + b_mod[l]).reshape(bsz, N_MOD, d)
        shift1, scale1, gate1 = mod[:, 0], mod[:, 1], mod[:, 2]
        shift2, scale2, gate2 = mod[:, 3], mod[:, 4], mod[:, 5]

        h = modulate(rmsnorm(x, g_mix[l]), shift1, scale1)
        z = h @ w_in[l]
        u_a, v_a, x_b, bg_c, cg_c, h_c, gate_logits = split_cols(z)
        y_a = gmlp_branch(jax.nn.gelu(u_a), jax.nn.gelu(v_a), gm_ln_g[l], gm_ln_b[l],
                          gm_w_s[l], gm_b_s[l]) @ w_pa[l]
        y_b = pool_branch(x_b, pool_w[l], pool_scale[l]) @ w_pb[l]
        y_c = conv_branch(bg_c, cg_c, h_c, conv_w[l]) @ w_pc[l]
        g = jax.nn.sigmoid(gate_logits).reshape(bsz, seq, N_BRANCH, d)
        merged = g[:, :, 0] * y_a + g[:, :, 1] * y_b + g[:, :, 2] * y_c
        x = x + gate1[:, None, :] * (merged @ w_o[l])

        h = modulate(rmsnorm(x, g_ffn[l]), shift2, scale2)
        ab = h @ w_13[l]
        a, b = ab[..., :D_FF], ab[..., D_FF:]
        x = x + gate2[:, None, :] * ((jax.nn.silu(a) * b) @ w_2[l])
    return rmsnorm(x, g_final)


import jax as _jax
import jax.numpy as _jnp

TWIN_FORMAT = 'train_step'
FWD_PARAMS = ['x', 'c', 'w_mod', 'b_mod', 'g_mix', 'w_in', 'gm_ln_g', 'gm_ln_b', 'gm_w_s', 'gm_b_s', 'w_pa', 'pool_w', 'pool_scale', 'w_pb', 'conv_w', 'w_pc', 'w_o', 'g_ffn', 'w_13', 'w_2', 'g_final']
TWIN_WEIGHTS = ['w_mod', 'b_mod', 'g_mix', 'w_in', 'gm_ln_g', 'gm_ln_b', 'gm_w_s', 'gm_b_s', 'w_pa', 'pool_w', 'pool_scale', 'w_pb', 'conv_w', 'w_pc', 'w_o', 'g_ffn', 'w_13', 'w_2', 'g_final']
TWIN_DIFF_INPUT = 'x'
TWIN_INPUTS = ['x', 'c', 'w_mod', 'b_mod', 'g_mix', 'w_in', 'gm_ln_g', 'gm_ln_b', 'gm_w_s', 'gm_b_s', 'w_pa', 'pool_w', 'pool_scale', 'w_pb', 'conv_w', 'w_pc', 'w_o', 'g_ffn', 'w_13', 'w_2', 'g_final', 'loss_target', 'm_w_mod', 'm_b_mod', 'm_g_mix', 'm_w_in', 'm_gm_ln_g', 'm_gm_ln_b', 'm_gm_w_s', 'm_gm_b_s', 'm_w_pa', 'm_pool_w', 'm_pool_scale', 'm_w_pb', 'm_conv_w', 'm_w_pc', 'm_w_o', 'm_g_ffn', 'm_w_13', 'm_w_2', 'm_g_final', 'v_w_mod', 'v_b_mod', 'v_g_mix', 'v_w_in', 'v_gm_ln_g', 'v_gm_ln_b', 'v_gm_w_s', 'v_gm_b_s', 'v_w_pa', 'v_pool_w', 'v_pool_scale', 'v_w_pb', 'v_conv_w', 'v_w_pc', 'v_w_o', 'v_g_ffn', 'v_w_13', 'v_w_2', 'v_g_final']
TWIN_OUTPUTS = ['loss', 'grad_x', 'grad_w_mod', 'grad_b_mod', 'grad_g_mix', 'grad_w_in', 'grad_gm_ln_g', 'grad_gm_ln_b', 'grad_gm_w_s', 'grad_gm_b_s', 'grad_w_pa', 'grad_pool_w', 'grad_pool_scale', 'grad_w_pb', 'grad_conv_w', 'grad_w_pc', 'grad_w_o', 'grad_g_ffn', 'grad_w_13', 'grad_w_2', 'grad_g_final', 'delta_w_mod', 'delta_b_mod', 'delta_g_mix', 'delta_w_in', 'delta_gm_ln_g', 'delta_gm_ln_b', 'delta_gm_w_s', 'delta_gm_b_s', 'delta_w_pa', 'delta_pool_w', 'delta_pool_scale', 'delta_w_pb', 'delta_conv_w', 'delta_w_pc', 'delta_w_o', 'delta_g_ffn', 'delta_w_13', 'delta_w_2', 'delta_g_final', 'new_m_w_mod', 'new_m_b_mod', 'new_m_g_mix', 'new_m_w_in', 'new_m_gm_ln_g', 'new_m_gm_ln_b', 'new_m_gm_w_s', 'new_m_gm_b_s', 'new_m_w_pa', 'new_m_pool_w', 'new_m_pool_scale', 'new_m_w_pb', 'new_m_conv_w', 'new_m_w_pc', 'new_m_w_o', 'new_m_g_ffn', 'new_m_w_13', 'new_m_w_2', 'new_m_g_final', 'new_v_w_mod', 'new_v_b_mod', 'new_v_g_mix', 'new_v_w_in', 'new_v_gm_ln_g', 'new_v_gm_ln_b', 'new_v_gm_w_s', 'new_v_gm_b_s', 'new_v_w_pa', 'new_v_pool_w', 'new_v_pool_scale', 'new_v_w_pb', 'new_v_conv_w', 'new_v_w_pc', 'new_v_w_o', 'new_v_g_ffn', 'new_v_w_13', 'new_v_w_2', 'new_v_g_final']
TWIN_LEAF_KINDS = {'loss': 'loss', 'grad_x': 'grad_x', 'grad_w_mod': 'grad_w', 'grad_b_mod': 'grad_w', 'grad_g_mix': 'grad_w', 'grad_w_in': 'grad_w', 'grad_gm_ln_g': 'grad_w', 'grad_gm_ln_b': 'grad_w', 'grad_gm_w_s': 'grad_w', 'grad_gm_b_s': 'grad_w', 'grad_w_pa': 'grad_w', 'grad_pool_w': 'grad_w', 'grad_pool_scale': 'grad_w', 'grad_w_pb': 'grad_w', 'grad_conv_w': 'grad_w', 'grad_w_pc': 'grad_w', 'grad_w_o': 'grad_w', 'grad_g_ffn': 'grad_w', 'grad_w_13': 'grad_w', 'grad_w_2': 'grad_w', 'grad_g_final': 'grad_w', 'delta_w_mod': 'delta_w', 'delta_b_mod': 'delta_w', 'delta_g_mix': 'delta_w', 'delta_w_in': 'delta_w', 'delta_gm_ln_g': 'delta_w', 'delta_gm_ln_b': 'delta_w', 'delta_gm_w_s': 'delta_w', 'delta_gm_b_s': 'delta_w', 'delta_w_pa': 'delta_w', 'delta_pool_w': 'delta_w', 'delta_pool_scale': 'delta_w', 'delta_w_pb': 'delta_w', 'delta_conv_w': 'delta_w', 'delta_w_pc': 'delta_w', 'delta_w_o': 'delta_w', 'delta_g_ffn': 'delta_w', 'delta_w_13': 'delta_w', 'delta_w_2': 'delta_w', 'delta_g_final': 'delta_w', 'new_m_w_mod': 'new_m', 'new_m_b_mod': 'new_m', 'new_m_g_mix': 'new_m', 'new_m_w_in': 'new_m', 'new_m_gm_ln_g': 'new_m', 'new_m_gm_ln_b': 'new_m', 'new_m_gm_w_s': 'new_m', 'new_m_gm_b_s': 'new_m', 'new_m_w_pa': 'new_m', 'new_m_pool_w': 'new_m', 'new_m_pool_scale': 'new_m', 'new_m_w_pb': 'new_m', 'new_m_conv_w': 'new_m', 'new_m_w_pc': 'new_m', 'new_m_w_o': 'new_m', 'new_m_g_ffn': 'new_m', 'new_m_w_13': 'new_m', 'new_m_w_2': 'new_m', 'new_m_g_final': 'new_m', 'new_v_w_mod': 'new_v', 'new_v_b_mod': 'new_v', 'new_v_g_mix': 'new_v', 'new_v_w_in': 'new_v', 'new_v_gm_ln_g': 'new_v', 'new_v_gm_ln_b': 'new_v', 'new_v_gm_w_s': 'new_v', 'new_v_gm_b_s': 'new_v', 'new_v_w_pa': 'new_v', 'new_v_pool_w': 'new_v', 'new_v_pool_scale': 'new_v', 'new_v_w_pb': 'new_v', 'new_v_conv_w': 'new_v', 'new_v_w_pc': 'new_v', 'new_v_w_o': 'new_v', 'new_v_g_ffn': 'new_v', 'new_v_w_13': 'new_v', 'new_v_w_2': 'new_v', 'new_v_g_final': 'new_v'}


def _forward(args):
    return _fwd_reference(*[args[k] for k in FWD_PARAMS])


def _output_shape():
    out = _jax.eval_shape(lambda: _forward(_fwd_setup_inputs(0)))
    return out.shape, out.dtype

N_MICROBATCH = 1
ADAM_LR = 0.001
ADAM_B1 = 0.9
ADAM_B2 = 0.999
ADAM_EPS = 1e-08
ADAM_WD = 0.01
ADAM_STEP = 10
PER_EXAMPLE_BATCH_AXIS = {'x': 0, 'c': 0, 'loss_target': 0}
SHARED_INPUTS = []
_WEIGHT_DTYPES = {'w_mod': _jnp.float32, 'b_mod': _jnp.float32, 'g_mix': _jnp.float32, 'w_in': _jnp.float32, 'gm_ln_g': _jnp.float32, 'gm_ln_b': _jnp.float32, 'gm_w_s': _jnp.float32, 'gm_b_s': _jnp.float32, 'w_pa': _jnp.float32, 'pool_w': _jnp.float32, 'pool_scale': _jnp.float32, 'w_pb': _jnp.float32, 'conv_w': _jnp.float32, 'w_pc': _jnp.float32, 'w_o': _jnp.float32, 'g_ffn': _jnp.float32, 'w_13': _jnp.float32, 'w_2': _jnp.float32, 'g_final': _jnp.float32}
MOMENT_SCALE = {'w_mod': 1.144067e-01, 'b_mod': 2.045373e-01, 'g_mix': 1.139989e-01, 'w_in': 4.777565e-02, 'gm_ln_g': 3.044099e-02, 'gm_ln_b': 3.184005e-02, 'gm_w_s': 3.046078e-02, 'gm_b_s': 4.251615e-02, 'w_pa': 3.758478e-02, 'pool_w': 5.552033e-02, 'pool_scale': 5.445200e-02, 'w_pb': 3.908581e-02, 'conv_w': 7.777726e-02, 'w_pc': 5.477939e-02, 'w_o': 7.738807e-02, 'g_ffn': 7.469364e-02, 'w_13': 3.282903e-02, 'w_2': 5.359662e-02, 'g_final': 6.447137e+01}


def _to_microbatches(a, axis):
    t = _jnp.moveaxis(a, axis, 0)
    t = t.reshape((N_MICROBATCH, t.shape[0] // N_MICROBATCH) + t.shape[1:])
    return _jnp.moveaxis(t, 1, axis + 1)


def setup_inputs(seed: int = 0) -> dict:
    inp = _fwd_setup_inputs(seed)
    key = _jax.random.fold_in(_jax.random.key(seed), 7919)
    shape, _ = _output_shape()
    out = dict(inp)
    out["loss_target"] = _jax.random.normal(_jax.random.fold_in(key, 0), shape, _jnp.float32)
    for i, name in enumerate(TWIN_WEIGHTS):
        w = inp[name].astype(_jnp.float32)
        if MOMENT_SCALE is None:
            s = _jnp.sqrt(_jnp.mean(_jnp.square(w)) + 1e-30)
        else:
            s = MOMENT_SCALE[name]
        km, kv = _jax.random.split(_jax.random.fold_in(key, i + 1))
        out[name] = w
        out["m_" + name] = s * _jax.random.normal(km, w.shape, _jnp.float32)
        out["v_" + name] = (s * s) * _jax.random.uniform(kv, w.shape, _jnp.float32, 0.5, 1.5)
    if N_MICROBATCH > 1:
        for name, axis in PER_EXAMPLE_BATCH_AXIS.items():
            out[name] = _to_microbatches(out[name], axis)
    return {'x': out['x'], 'c': out['c'], 'w_mod': out['w_mod'], 'b_mod': out['b_mod'], 'g_mix': out['g_mix'], 'w_in': out['w_in'], 'gm_ln_g': out['gm_ln_g'], 'gm_ln_b': out['gm_ln_b'], 'gm_w_s': out['gm_w_s'], 'gm_b_s': out['gm_b_s'], 'w_pa': out['w_pa'], 'pool_w': out['pool_w'], 'pool_scale': out['pool_scale'], 'w_pb': out['w_pb'], 'conv_w': out['conv_w'], 'w_pc': out['w_pc'], 'w_o': out['w_o'], 'g_ffn': out['g_ffn'], 'w_13': out['w_13'], 'w_2': out['w_2'], 'g_final': out['g_final'], 'loss_target': out['loss_target'], 'm_w_mod': out['m_w_mod'], 'm_b_mod': out['m_b_mod'], 'm_g_mix': out['m_g_mix'], 'm_w_in': out['m_w_in'], 'm_gm_ln_g': out['m_gm_ln_g'], 'm_gm_ln_b': out['m_gm_ln_b'], 'm_gm_w_s': out['m_gm_w_s'], 'm_gm_b_s': out['m_gm_b_s'], 'm_w_pa': out['m_w_pa'], 'm_pool_w': out['m_pool_w'], 'm_pool_scale': out['m_pool_scale'], 'm_w_pb': out['m_w_pb'], 'm_conv_w': out['m_conv_w'], 'm_w_pc': out['m_w_pc'], 'm_w_o': out['m_w_o'], 'm_g_ffn': out['m_g_ffn'], 'm_w_13': out['m_w_13'], 'm_w_2': out['m_w_2'], 'm_g_final': out['m_g_final'], 'v_w_mod': out['v_w_mod'], 'v_b_mod': out['v_b_mod'], 'v_g_mix': out['v_g_mix'], 'v_w_in': out['v_w_in'], 'v_gm_ln_g': out['v_gm_ln_g'], 'v_gm_ln_b': out['v_gm_ln_b'], 'v_gm_w_s': out['v_gm_w_s'], 'v_gm_b_s': out['v_gm_b_s'], 'v_w_pa': out['v_w_pa'], 'v_pool_w': out['v_pool_w'], 'v_pool_scale': out['v_pool_scale'], 'v_w_pb': out['v_w_pb'], 'v_conv_w': out['v_conv_w'], 'v_w_pc': out['v_w_pc'], 'v_w_o': out['v_w_o'], 'v_g_ffn': out['v_g_ffn'], 'v_w_13': out['v_w_13'], 'v_w_2': out['v_w_2'], 'v_g_final': out['v_g_final']}


def _loss(weights, diff, rest, loss_target):
    with _jax.named_scope("forward"):
        args = {**rest, TWIN_DIFF_INPUT: diff, **{k: w.astype(_WEIGHT_DTYPES[k]) for k, w in weights.items()}}
        y = _forward(args)
    with _jax.named_scope("loss_head"):
        err = _jnp.square(y.astype(_jnp.float32) - loss_target)
        return 0.5 * _jnp.sum(_jnp.mean(err, axis=-1)) if err.ndim else 0.5 * err


def _adamw(w, g, m, v):
    m = ADAM_B1 * m + (1.0 - ADAM_B1) * g
    v = ADAM_B2 * v + (1.0 - ADAM_B2) * _jnp.square(g)
    m_hat = m / (1.0 - ADAM_B1 ** ADAM_STEP)
    v_hat = v / (1.0 - ADAM_B2 ** ADAM_STEP)
    delta = -ADAM_LR * (m_hat / (_jnp.sqrt(v_hat) + ADAM_EPS) + ADAM_WD * w)
    return delta, m, v


def reference(x, c, w_mod, b_mod, g_mix, w_in, gm_ln_g, gm_ln_b, gm_w_s, gm_b_s, w_pa, pool_w, pool_scale, w_pb, conv_w, w_pc, w_o, g_ffn, w_13, w_2, g_final, loss_target, m_w_mod, m_b_mod, m_g_mix, m_w_in, m_gm_ln_g, m_gm_ln_b, m_gm_w_s, m_gm_b_s, m_w_pa, m_pool_w, m_pool_scale, m_w_pb, m_conv_w, m_w_pc, m_w_o, m_g_ffn, m_w_13, m_w_2, m_g_final, v_w_mod, v_b_mod, v_g_mix, v_w_in, v_gm_ln_g, v_gm_ln_b, v_gm_w_s, v_gm_b_s, v_w_pa, v_pool_w, v_pool_scale, v_w_pb, v_conv_w, v_w_pc, v_w_o, v_g_ffn, v_w_13, v_w_2, v_g_final):
    given = dict(x=x, c=c, w_mod=w_mod, b_mod=b_mod, g_mix=g_mix, w_in=w_in, gm_ln_g=gm_ln_g, gm_ln_b=gm_ln_b, gm_w_s=gm_w_s, gm_b_s=gm_b_s, w_pa=w_pa, pool_w=pool_w, pool_scale=pool_scale, w_pb=w_pb, conv_w=conv_w, w_pc=w_pc, w_o=w_o, g_ffn=g_ffn, w_13=w_13, w_2=w_2, g_final=g_final, loss_target=loss_target, m_w_mod=m_w_mod, m_b_mod=m_b_mod, m_g_mix=m_g_mix, m_w_in=m_w_in, m_gm_ln_g=m_gm_ln_g, m_gm_ln_b=m_gm_ln_b, m_gm_w_s=m_gm_w_s, m_gm_b_s=m_gm_b_s, m_w_pa=m_w_pa, m_pool_w=m_pool_w, m_pool_scale=m_pool_scale, m_w_pb=m_w_pb, m_conv_w=m_conv_w, m_w_pc=m_w_pc, m_w_o=m_w_o, m_g_ffn=m_g_ffn, m_w_13=m_w_13, m_w_2=m_w_2, m_g_final=m_g_final, v_w_mod=v_w_mod, v_b_mod=v_b_mod, v_g_mix=v_g_mix, v_w_in=v_w_in, v_gm_ln_g=v_gm_ln_g, v_gm_ln_b=v_gm_ln_b, v_gm_w_s=v_gm_w_s, v_gm_b_s=v_gm_b_s, v_w_pa=v_w_pa, v_pool_w=v_pool_w, v_pool_scale=v_pool_scale, v_w_pb=v_w_pb, v_conv_w=v_conv_w, v_w_pc=v_w_pc, v_w_o=v_w_o, v_g_ffn=v_g_ffn, v_w_13=v_w_13, v_w_2=v_w_2, v_g_final=v_g_final)
    weights = {n: given[n] for n in TWIN_WEIGHTS}
    shared = {n: given[n] for n in SHARED_INPUTS}
    per_example = {n: given[n] for n in ['x', 'c']}
    grad_fn = _jax.value_and_grad(_loss, argnums=(0, 1))

    def one_microbatch(ex, loss_target):
        ex = dict(ex)
        diff = ex.pop(TWIN_DIFF_INPUT)
        return grad_fn(weights, diff, {**shared, **ex}, loss_target)

    if N_MICROBATCH == 1:
        loss, (grad_w, grad_x) = one_microbatch(per_example, given["loss_target"])
    else:
        def body(carry, xs):
            loss_sum, grad_sum = carry
            l_k, (gw_k, gx_k) = one_microbatch(xs[0], xs[1])
            with _jax.named_scope("update"):
                return (loss_sum + l_k, _jax.tree.map(_jnp.add, grad_sum, gw_k)), gx_k

        init = (_jnp.zeros((), _jnp.float32), _jax.tree.map(_jnp.zeros_like, weights))
        (loss, grad_w), grad_x = _jax.lax.scan(body, init, (per_example, given["loss_target"]))
    with _jax.named_scope("update"):
        delta_w, new_m, new_v = {}, {}, {}
        for n in TWIN_WEIGHTS:
            delta_w[n], new_m[n], new_v[n] = _adamw(weights[n], grad_w[n], given["m_" + n], given["v_" + n])
    return (loss, grad_x, *[grad_w[n] for n in TWIN_WEIGHTS], *[delta_w[n] for n in TWIN_WEIGHTS],
            *[new_m[n] for n in TWIN_WEIGHTS], *[new_v[n] for n in TWIN_WEIGHTS])
```

```python
import jax
import jax.numpy as jnp
from jax import lax
from jax.experimental import pallas as pl
from jax.experimental.pallas import tpu as pltpu

F32 = jnp.float32
BF16 = jnp.bfloat16
MESH = pl.DeviceIdType.MESH

EPS = 1e-6
CHUNK = 128
HEADS = 4
HEAD_DIM = 128
BR_W = 512
N_GROUP = 4
GROUP_DIM = 128
HALO = 16
N_SPLIT = 6 * BR_W
N_CHIP = 4
N_DEV = 8

ADAM_LR = 0.001
ADAM_B1 = 0.9
ADAM_B2 = 0.999
ADAM_EPS = 1e-08
ADAM_WD = 0.01
ADAM_STEP = 10

V7X_VMEM_LIMIT = 48 * 1024 * 1024
LANE = 128
SUBLANE = 8

GELU_K = 0.7978845608028654
GELU_C = 0.044715


def _cparams(sem):
    return pltpu.CompilerParams(dimension_semantics=sem, vmem_limit_bytes=V7X_VMEM_LIMIT)


def _pick(n, cap, q=LANE):
    best = None
    d = q
    while d <= min(n, cap):
        if n % d == 0:
            best = d
        d += q
    return n if best is None else best


def _sigmoid(x):
    return 1.0 / (1.0 + jnp.exp(-x))


def _gelu(x):
    t = jnp.tanh(GELU_K * (x + GELU_C * x * x * x))
    return 0.5 * x * (1.0 + t), t


def _gelu_grad(x, t):
    return 0.5 * (1.0 + t) + 0.5 * x * (1.0 - t * t) * GELU_K * (1.0 + 3.0 * GELU_C * x * x)


def _matmul(a, b, *, mode, name, out_dtype=None, layer=None, a_cols=None, b_cols=None,
            resid=None, tm_cap=1024, tn_cap=1536, tk_cap=1536):
    out_dtype = BF16 if out_dtype is None else out_dtype
    b2 = b.shape[-2:]
    if mode == "nn":
        M, K = a.shape
        N = b2[1]
    elif mode == "nt":
        M, K = a.shape
        N = b2[0]
    else:
        K = a.shape[0]
        M = a.shape[1] if a_cols is None else a_cols[1]
        N = b2[1] if b_cols is None else b_cols[1]
    tm = _pick(M if resid is None else resid[3], tm_cap)
    tn = _pick(N, tn_cap)
    tk = _pick(K, tk_cap)
    nk = K // tk
    a_off = 0 if a_cols is None else a_cols[0] // tm
    b_off = 0 if b_cols is None else b_cols[0] // tn
    if a_cols is not None:
        assert a_cols[0] % tm == 0
    if b_cols is not None:
        assert b_cols[0] % tn == 0

    if mode == "nn":
        a_spec = pl.BlockSpec((tm, tk), lambda i, j, k: (i, k))
        b_blk, b_idx = (tk, tn), (lambda i, j, k: (k, j))
        dims = (((1,), (0,)), ((), ()))
    elif mode == "nt":
        a_spec = pl.BlockSpec((tm, tk), lambda i, j, k: (i, k))
        b_blk, b_idx = (tn, tk), (lambda i, j, k: (j, k))
        dims = (((1,), (1,)), ((), ()))
    else:
        a_spec = pl.BlockSpec((tk, tm), lambda i, j, k: (k, i + a_off))
        b_blk, b_idx = (tk, tn), (lambda i, j, k: (k, j + b_off))
        dims = (((0,), (0,)), ((), ()))
    if layer is None:
        b_spec = pl.BlockSpec(b_blk, b_idx)
    else:
        b_spec = pl.BlockSpec((None,) + b_blk, lambda i, j, k: (layer,) + b_idx(i, j, k))

    in_specs = [a_spec, b_spec]
    operands = [a, b]
    o_spec = pl.BlockSpec((tm, tn), lambda i, j, k: (i, j))
    if resid is not None:
        x, mod, row, seq = resid
        D = mod.shape[-1]
        in_specs += [o_spec, pl.BlockSpec((1, SUBLANE, tn), lambda i, j, k: ((i * tm) // seq, 0, j))]
        operands += [x, mod]
        out_shape = (jax.ShapeDtypeStruct((M, N), F32), jax.ShapeDtypeStruct((M, N), BF16))
        out_specs = (o_spec, o_spec)
        assert seq % tm == 0 and D == N
    else:
        out_shape = jax.ShapeDtypeStruct((M, N), out_dtype)
        out_specs = o_spec

    def finish(acc, refs):
        if resid is not None:
            x_ref, mod_ref, o_ref, p_ref = refs
            o_ref[...] = x_ref[...] + mod_ref[0, row:row + 1, :] * acc
            p_ref[...] = acc.astype(BF16)
        else:
            (o_ref,) = refs
            o_ref[...] = acc.astype(out_dtype)

    def body(a_ref, b_ref, *refs):
        part = lax.dot_general(a_ref[...], b_ref[...], dims, preferred_element_type=F32)
        if nk == 1:
            finish(part, refs)
            return
        acc_ref = refs[-1]
        k = pl.program_id(2)

        @pl.when(k == 0)
        def _():
            acc_ref[...] = part

        @pl.when(k > 0)
        def _():
            acc_ref[...] += part

        @pl.when(k == nk - 1)
        def _():
            finish(acc_ref[...], refs[:-1])

    scratch = [] if nk == 1 else [pltpu.VMEM((tm, tn), F32)]
    return pl.pallas_call(
        body, name=name, grid=(M // tm, N // tn, nk), in_specs=in_specs, out_specs=out_specs,
        out_shape=out_shape, scratch_shapes=scratch,
        compiler_params=_cparams(("parallel", "parallel", "arbitrary")),
    )(*operands)


def _row_tile(seq, cap):
    return _pick(seq, cap, HALO)


def _mod_spec(tm, seq, D):
    return pl.BlockSpec((1, SUBLANE, D), lambda i: ((i * tm) // seq, 0, 0))


def _norm_fwd(x, g, mod, shift_row, scale_row, seq, name):
    T, D = x.shape
    tm = _row_tile(seq, 512)

    def body(x_ref, g_ref, mod_ref, h_ref):
        xv = x_ref[...]
        rstd = lax.rsqrt(jnp.mean(xv * xv, axis=-1, keepdims=True) + EPS)
        n = xv * rstd * g_ref[...]
        h = n * (1.0 + mod_ref[0, scale_row:scale_row + 1, :]) + mod_ref[0, shift_row:shift_row + 1, :]
        h_ref[...] = h.astype(BF16)

    row = pl.BlockSpec((tm, D), lambda i: (i, 0))
    return pl.pallas_call(
        body, name=name, grid=(T // tm,),
        in_specs=[row, pl.BlockSpec((1, D), lambda i: (0, 0)), _mod_spec(tm, seq, D)],
        out_specs=row, out_shape=jax.ShapeDtypeStruct((T, D), BF16),
        compiler_params=_cparams(("parallel",)),
    )(x, g, mod)


def _norm_bwd(x, dh, dres, g, mod, shift_row, scale_row, seq, name):
    T, D = x.shape
    B = mod.shape[0]
    tm = _row_tile(seq, 512)
    per_seq = seq // tm

    def body(x_ref, dh_ref, dres_ref, g_ref, mod_ref, dx_ref, h_ref, pb_ref, pg_ref):
        i = pl.program_id(0)
        xv = x_ref[...]
        dhv = dh_ref[...].astype(F32)
        gv = g_ref[...]
        scale1 = 1.0 + mod_ref[0, scale_row:scale_row + 1, :]
        rstd = lax.rsqrt(jnp.mean(xv * xv, axis=-1, keepdims=True) + EPS)
        xhat = xv * rstd
        n = xhat * gv
        dn = dhv * scale1
        dxhat = dn * gv
        dx = rstd * (dxhat - xhat * jnp.mean(dxhat * xhat, axis=-1, keepdims=True))
        dx_ref[...] = dres_ref[...] + dx
        h_ref[...] = (n * scale1 + mod_ref[0, shift_row:shift_row + 1, :]).astype(BF16)

        @pl.when(i % per_seq == 0)
        def _():
            pb_ref[...] = jnp.zeros_like(pb_ref)

        @pl.when(i == 0)
        def _():
            pg_ref[...] = jnp.zeros_like(pg_ref)

        pb_ref[0, 0:1, :] += jnp.sum(dhv, axis=0, keepdims=True)
        pb_ref[0, 1:2, :] += jnp.sum(dhv * n, axis=0, keepdims=True)
        pg_ref[0:1, :] += jnp.sum(dn * xhat, axis=0, keepdims=True)

    row = pl.BlockSpec((tm, D), lambda i: (i, 0))
    return pl.pallas_call(
        body, name=name, grid=(T // tm,),
        in_specs=[row, row, row, pl.BlockSpec((1, D), lambda i: (0, 0)), _mod_spec(tm, seq, D)],
        out_specs=(row, row, _mod_spec(tm, seq, D), pl.BlockSpec((SUBLANE, D), lambda i: (0, 0))),
        out_shape=(jax.ShapeDtypeStruct((T, D), F32), jax.ShapeDtypeStruct((T, D), BF16),
                   jax.ShapeDtypeStruct((B, SUBLANE, D), F32), jax.ShapeDtypeStruct((SUBLANE, D), F32)),
        compiler_params=_cparams(("arbitrary",)),
    )(x, dh, dres, g, mod)


def _gate_bwd(dx, prod, mod, gate_row, seq, name):
    T, D = dx.shape
    B = mod.shape[0]
    tm = _row_tile(seq, 512)
    per_seq = seq // tm

    def body(dx_ref, p_ref, mod_ref, dp_ref, pb_ref):
        i = pl.program_id(0)
        dxv = dx_ref[...]
        dp_ref[...] = (dxv * mod_ref[0, gate_row:gate_row + 1, :]).astype(BF16)

        @pl.when(i % per_seq == 0)
        def _():
            pb_ref[...] = jnp.zeros_like(pb_ref)

        pb_ref[0, 0:1, :] += jnp.sum(dxv * p_ref[...].astype(F32), axis=0, keepdims=True)

    row = pl.BlockSpec((tm, D), lambda i: (i, 0))
    return pl.pallas_call(
        body, name=name, grid=(T // tm,),
        in_specs=[row, row, _mod_spec(tm, seq, D)],
        out_specs=(row, _mod_spec(tm, seq, D)),
        out_shape=(jax.ShapeDtypeStruct((T, D), BF16), jax.ShapeDtypeStruct((B, SUBLANE, D), F32)),
        compiler_params=_cparams(("arbitrary",)),
    )(dx, prod, mod)


def _swiglu_fwd(ab, name):
    T, F2 = ab.shape
    Fh = F2 // 2
    tm = _pick(T, 256, HALO)

    def body(ab_ref, act_ref):
        a = ab_ref[:, :Fh].astype(F32)
        b = ab_ref[:, Fh:].astype(F32)
        act_ref[...] = (a * _sigmoid(a) * b).astype(BF16)

    return pl.pallas_call(
        body, name=name, grid=(T // tm,),
        in_specs=[pl.BlockSpec((tm, F2), lambda i: (i, 0))],
        out_specs=pl.BlockSpec((tm, Fh), lambda i: (i, 0)),
        out_shape=jax.ShapeDtypeStruct((T, Fh), BF16),
        compiler_params=_cparams(("parallel",)),
    )(ab)


def _swiglu_bwd(dact, ab, name):
    T, F2 = ab.shape
    Fh = F2 // 2
    tm = _pick(T, 256, HALO)

    def body(d_ref, ab_ref, dab_ref):
        d = d_ref[...].astype(F32)
        a = ab_ref[:, :Fh].astype(F32)
        b = ab_ref[:, Fh:].astype(F32)
        sg = _sigmoid(a)
        dab_ref[:, :Fh] = (d * b * sg * (1.0 + a * (1.0 - sg))).astype(BF16)
        dab_ref[:, Fh:] = (d * a * sg).astype(BF16)

    return pl.pallas_call(
        body, name=name, grid=(T // tm,),
        in_specs=[pl.BlockSpec((tm, Fh), lambda i: (i, 0)), pl.BlockSpec((tm, F2), lambda i: (i, 0))],
        out_specs=pl.BlockSpec((tm, F2), lambda i: (i, 0)),
        out_shape=jax.ShapeDtypeStruct((T, F2), BF16),
        compiler_params=_cparams(("parallel",)),
    )(dact, ab)


def _shift_down(v, d):
    return pltpu.roll(v, d, 0)


def _shift_up(v, d):
    return pltpu.roll(v, v.shape[0] - d, 0)


def _tril_mask():
    r = lax.broadcasted_iota(jnp.int32, (CHUNK, CHUNK), 0)
    c = lax.broadcasted_iota(jnp.int32, (CHUNK, CHUNK), 1)
    return c <= r


def _pool_counts(i, tm, seq, rows, first_row):
    r = lax.broadcasted_iota(jnp.int32, (rows, 1), 0) + (i * tm + first_row)
    pos1 = (r % seq + 1).astype(F32)
    return [jnp.minimum(pos1, float(2 << g)) for g in range(N_GROUP)]


def _mixer_forward_values(zt, hxb, hch, i, tm, seq, ln_g, ln_b, ws_ref, bs_ref, pw_ref, pscale, cw_ref):
    u = zt[:, 0 * BR_W:1 * BR_W]
    v = zt[:, 1 * BR_W:2 * BR_W]
    xb = zt[:, 2 * BR_W:3 * BR_W]
    bg = zt[:, 3 * BR_W:4 * BR_W]
    cg = zt[:, 4 * BR_W:5 * BR_W]
    hc = zt[:, 5 * BR_W:6 * BR_W]
    out = {}

    ug, tu = _gelu(u)
    vg, tv = _gelu(v)
    mu = jnp.mean(vg, axis=-1, keepdims=True)
    vc = vg - mu
    rstd = lax.rsqrt(jnp.mean(vc * vc, axis=-1, keepdims=True) + EPS)
    vhat = vc * rstd
    vn = (vhat * ln_g + ln_b).astype(BF16)
    mask = _tril_mask()
    wt = [jnp.where(mask, ws_ref[h], 0.0).astype(BF16) for h in range(HEADS)]
    rows = []
    for n in range(tm // CHUNK):
        blocks = []
        for h in range(HEADS):
            blk = vn[n * CHUNK:(n + 1) * CHUNK, h * HEAD_DIM:(h + 1) * HEAD_DIM]
            sb = jnp.dot(wt[h], blk, preferred_element_type=F32) + bs_ref[:, h:h + 1]
            blocks.append(sb)
        rows.append(jnp.concatenate(blocks, axis=1))
    s = jnp.concatenate(rows, axis=0) if len(rows) > 1 else rows[0]
    out.update(u=u, v=v, ug=ug, tu=tu, tv=tv, rstd=rstd, vhat=vhat, vn=vn, wt=wt, s=s, a_out=ug * s)

    ext = jnp.concatenate([hxb, xb], axis=0)
    cnt = _pool_counts(i, tm, seq, tm, 0)
    p, qs = [], []
    for g in range(N_GROUP):
        e = ext[:, g * GROUP_DIM:(g + 1) * GROUP_DIM]
        acc = e
        for d in (1, 2, 4, 8)[:g + 1]:
            acc = acc + _shift_down(acc, d)
        pg = acc[HALO:, :] / cnt[g] - xb[:, g * GROUP_DIM:(g + 1) * GROUP_DIM]
        p.append(pg.astype(BF16))
        qs.append(jnp.dot(p[g], pw_ref[g].astype(BF16), preferred_element_type=F32))
    q = jnp.concatenate(qs, axis=1)
    out.update(p=p, q=q, b_out=q * pscale)

    zc = cg * hc
    zce = jnp.concatenate([hch[:, :BR_W] * hch[:, BR_W:], zc], axis=0)
    z1 = _shift_down(zce, 1)[HALO:, :]
    z2 = _shift_down(zce, 2)[HALO:, :]
    y = cw_ref[0:1, :] * z2 + cw_ref[1:2, :] * z1 + cw_ref[2:3, :] * zc
    out.update(bg=bg, cg=cg, hc=hc, zc=zc, z1=z1, z2=z2, y=y, c_out=bg * y)
    return out


def _mixer_specs(tm, T):
    nb = T // HALO
    per = tm // HALO
    prev = lambda i: jnp.maximum(i * per - 1, 0)
    nxt = lambda i: jnp.minimum((i + 1) * per, nb - 1)
    return prev, nxt


def _mixer_fwd(z, prm, seq, name):
    T = z.shape[0]
    tm = _row_tile(seq, 256)
    per_seq = seq // tm
    prev, _ = _mixer_specs(tm, T)

    def body(z_ref, hxb_ref, hch_ref, vec_ref, cw_ref, ws_ref, bs_ref, pw_ref, cat_ref):
        i = pl.program_id(0)
        keep = jnp.where(i % per_seq == 0, 0.0, 1.0)
        zt = z_ref[...].astype(F32)
        hxb = hxb_ref[...].astype(F32) * keep
        hch = hch_ref[...].astype(F32) * keep
        o = _mixer_forward_values(zt, hxb, hch, i, tm, seq, vec_ref[0:1, :], vec_ref[1:2, :],
                                  ws_ref, bs_ref, pw_ref, vec_ref[2:3, :], cw_ref)
        cat_ref[:, 0 * BR_W:1 * BR_W] = o["a_out"].astype(BF16)
        cat_ref[:, 1 * BR_W:2 * BR_W] = o["b_out"].astype(BF16)
        cat_ref[:, 2 * BR_W:3 * BR_W] = o["c_out"].astype(BF16)

    full = lambda shape: pl.BlockSpec(shape, lambda i: (0,) * len(shape))
    return pl.pallas_call(
        body, name=name, grid=(T // tm,),
        in_specs=[pl.BlockSpec((tm, N_SPLIT), lambda i: (i, 0)),
                  pl.BlockSpec((HALO, BR_W), lambda i: (prev(i), 2)),
                  pl.BlockSpec((HALO, 2 * BR_W), lambda i: (prev(i), 2)),
                  full((SUBLANE, BR_W)), full((SUBLANE, BR_W)), full((HEADS, CHUNK, CHUNK)),
                  full((CHUNK, LANE)), full((N_GROUP, GROUP_DIM, GROUP_DIM))],
        out_specs=pl.BlockSpec((tm, 3 * BR_W), lambda i: (i, 0)),
        out_shape=jax.ShapeDtypeStruct((T, 3 * BR_W), BF16),
        compiler_params=_cparams(("parallel",)),
    )(z, z, z, prm["vec"], prm["conv"], prm["w_s"], prm["b_s"], prm["pool_w"])


def _mixer_bwd(z, dcat, dgl, prm, seq, name):
    T, IN = z.shape
    GL = IN - N_SPLIT
    tm = _row_tile(seq, 256)
    per_seq = seq // tm
    prev, nxt = _mixer_specs(tm, T)
    nrow = tm + HALO

    def body(z_ref, hxb_ref, hch_ref, nbg_ref, dcat_ref, ndb_ref, ndc_ref, dgl_ref,
             vec_ref, cw_ref, ws_ref, bs_ref, pw_ref,
             dz_ref, pv_ref, dws_ref, dbs_ref, dpw_ref):
        i = pl.program_id(0)
        keep_prev = jnp.where(i % per_seq == 0, 0.0, 1.0)
        keep_next = jnp.where(i % per_seq == per_seq - 1, 0.0, 1.0)
        zt = z_ref[...].astype(F32)
        hxb = hxb_ref[...].astype(F32) * keep_prev
        hch = hch_ref[...].astype(F32) * keep_prev
        ln_g = vec_ref[0:1, :]
        pscale = vec_ref[2:3, :]
        o = _mixer_forward_values(zt, hxb, hch, i, tm, seq, ln_g, vec_ref[1:2, :],
                                  ws_ref, bs_ref, pw_ref, pscale, cw_ref)
        dcv = dcat_ref[...].astype(F32)
        da = dcv[:, 0 * BR_W:1 * BR_W]
        db = dcv[:, 1 * BR_W:2 * BR_W]
        dc = dcv[:, 2 * BR_W:3 * BR_W]
        mask = _tril_mask()

        @pl.when(i == 0)
        def _():
            pv_ref[...] = jnp.zeros_like(pv_ref)
            dws_ref[...] = jnp.zeros_like(dws_ref)
            dbs_ref[...] = jnp.zeros_like(dbs_ref)
            dpw_ref[...] = jnp.zeros_like(dpw_ref)

        d_ug = da * o["s"]
        ds = da * o["ug"]
        ds_b = ds.astype(BF16)
        vn = o["vn"]
        dvn_rows = []
        dws = [jnp.zeros((CHUNK, CHUNK), F32) for _ in range(HEADS)]
        dsum = jnp.zeros((CHUNK, BR_W), F32)
        for n in range(tm // CHUNK):
            blocks = []
            rs = slice(n * CHUNK, (n + 1) * CHUNK)
            dsum = dsum + ds[rs, :]
            for h in range(HEADS):
                cs = slice(h * HEAD_DIM, (h + 1) * HEAD_DIM)
                dsb = ds_b[rs, cs]
                blocks.append(lax.dot_general(o["wt"][h], dsb, (((0,), (0,)), ((), ())),
                                              preferred_element_type=F32))
                dws[h] = dws[h] + lax.dot_general(dsb, vn[rs, cs], (((1,), (1,)), ((), ())),
                                                  preferred_element_type=F32)
            dvn_rows.append(jnp.concatenate(blocks, axis=1))
        dvn = jnp.concatenate(dvn_rows, axis=0) if len(dvn_rows) > 1 else dvn_rows[0]
        lane = lax.broadcasted_iota(jnp.int32, (CHUNK, LANE), 1)
        dbs_t = jnp.zeros((CHUNK, LANE), F32)
        for h in range(HEADS):
            dws_ref[h] += jnp.where(mask, dws[h], 0.0)
            rsum = jnp.sum(dsum[:, h * HEAD_DIM:(h + 1) * HEAD_DIM], axis=1, keepdims=True)
            dbs_t = dbs_t + jnp.where(lane == h, rsum, 0.0)
        dbs_ref[...] += dbs_t
        vhat = o["vhat"]
        pv_ref[0:1, :] += jnp.sum(dvn * vhat, axis=0, keepdims=True)
        pv_ref[1:2, :] += jnp.sum(dvn, axis=0, keepdims=True)
        dvhat = dvn * ln_g
        dvg = o["rstd"] * (dvhat - jnp.mean(dvhat, axis=-1, keepdims=True)
                           - vhat * jnp.mean(dvhat * vhat, axis=-1, keepdims=True))
        du = d_ug * _gelu_grad(o["u"], o["tu"])
        dv = dvg * _gelu_grad(o["v"], o["tv"])

        pv_ref[2:3, :] += jnp.sum(db * o["q"], axis=0, keepdims=True)
        dq = (db * pscale).astype(BF16)
        dqn = (ndb_ref[...].astype(F32) * pscale * keep_next).astype(BF16)
        cnt = _pool_counts(i, tm, seq, nrow, 0)
        dxb = []
        for g in range(N_GROUP):
            cs = slice(g * GROUP_DIM, (g + 1) * GROUP_DIM)
            pwg = pw_ref[g].astype(BF16)
            dpw_ref[g] += lax.dot_general(o["p"][g], dq[:, cs], (((0,), (0,)), ((), ())),
                                          preferred_element_type=F32)
            dp = lax.dot_general(dq[:, cs], pwg, (((1,), (1,)), ((), ())), preferred_element_type=F32)
            dpn = lax.dot_general(dqn[:, cs], pwg, (((1,), (1,)), ((), ())), preferred_element_type=F32)
            acc = jnp.concatenate([dp, dpn], axis=0) / cnt[g]
            for d in (1, 2, 4, 8)[:g + 1]:
                acc = acc + _shift_up(acc, d)
            dxb.append(acc[:tm, :] - dp)
        dxb = jnp.concatenate(dxb, axis=1)

        dbg = dc * o["y"]
        dy = dc * o["bg"]
        pv_ref[3:4, :] += jnp.sum(dy * o["z2"], axis=0, keepdims=True)
        pv_ref[4:5, :] += jnp.sum(dy * o["z1"], axis=0, keepdims=True)
        pv_ref[5:6, :] += jnp.sum(dy * o["zc"], axis=0, keepdims=True)
        dyn = ndc_ref[...].astype(F32) * nbg_ref[...].astype(F32) * keep_next
        dye = jnp.concatenate([dy, dyn], axis=0)
        dzc = (cw_ref[2:3, :] * dy + cw_ref[1:2, :] * _shift_up(dye, 1)[:tm, :]
               + cw_ref[0:1, :] * _shift_up(dye, 2)[:tm, :])
        dcg = dzc * o["hc"]
        dhc = dzc * o["cg"]

        dz_ref[:, 0 * BR_W:1 * BR_W] = du.astype(BF16)
        dz_ref[:, 1 * BR_W:2 * BR_W] = dv.astype(BF16)
        dz_ref[:, 2 * BR_W:3 * BR_W] = dxb.astype(BF16)
        dz_ref[:, 3 * BR_W:4 * BR_W] = dbg.astype(BF16)
        dz_ref[:, 4 * BR_W:5 * BR_W] = dcg.astype(BF16)
        dz_ref[:, 5 * BR_W:6 * BR_W] = dhc.astype(BF16)
        dz_ref[:, N_SPLIT:] = dgl_ref[...]

    full = lambda shape: pl.BlockSpec(shape, lambda i: (0,) * len(shape))
    return pl.pallas_call(
        body, name=name, grid=(T // tm,),
        in_specs=[pl.BlockSpec((tm, N_SPLIT), lambda i: (i, 0)),
                  pl.BlockSpec((HALO, BR_W), lambda i: (prev(i), 2)),
                  pl.BlockSpec((HALO, 2 * BR_W), lambda i: (prev(i), 2)),
                  pl.BlockSpec((HALO, BR_W), lambda i: (nxt(i), 3)),
                  pl.BlockSpec((tm, 3 * BR_W), lambda i: (i, 0)),
                  pl.BlockSpec((HALO, BR_W), lambda i: (nxt(i), 1)),
                  pl.BlockSpec((HALO, BR_W), lambda i: (nxt(i), 2)),
                  pl.BlockSpec((tm, GL), lambda i: (i, 0)),
                  full((SUBLANE, BR_W)), full((SUBLANE, BR_W)), full((HEADS, CHUNK, CHUNK)),
                  full((CHUNK, LANE)), full((N_GROUP, GROUP_DIM, GROUP_DIM))],
        out_specs=(pl.BlockSpec((tm, IN), lambda i: (i, 0)),
                   full((SUBLANE, BR_W)), full((HEADS, CHUNK, CHUNK)), full((CHUNK, LANE)),
                   full((N_GROUP, GROUP_DIM, GROUP_DIM))),
        out_shape=(jax.ShapeDtypeStruct((T, IN), BF16),
                   jax.ShapeDtypeStruct((SUBLANE, BR_W), F32),
                   jax.ShapeDtypeStruct((HEADS, CHUNK, CHUNK), F32),
                   jax.ShapeDtypeStruct((CHUNK, LANE), F32),
                   jax.ShapeDtypeStruct((N_GROUP, GROUP_DIM, GROUP_DIM), F32)),
        compiler_params=_cparams(("arbitrary",)),
    )(z, z, z, z, dcat, dcat, dcat, dgl, prm["vec"], prm["conv"], prm["w_s"], prm["b_s"], prm["pool_w"])


def _proj_fwd(cat, z, w_pa, w_pb, w_pc, layer, D, name):
    T, IN = z.shape
    GL = 3 * D
    assert N_SPLIT % GL == 0
    glb = N_SPLIT // GL
    tm = _pick(T, 256, HALO)

    def body(cat_ref, gl_ref, wa_ref, wb_ref, wc_ref, m_ref):
        acc = jnp.zeros((tm, D), F32)
        for k, w_ref in enumerate((wa_ref, wb_ref, wc_ref)):
            y = jnp.dot(cat_ref[:, k * BR_W:(k + 1) * BR_W], w_ref[...], preferred_element_type=F32)
            acc = acc + _sigmoid(gl_ref[:, k * D:(k + 1) * D].astype(F32)) * y
        m_ref[...] = acc.astype(BF16)

    wspec = pl.BlockSpec((None, BR_W, D), lambda i: (layer, 0, 0))
    return pl.pallas_call(
        body, name=name, grid=(T // tm,),
        in_specs=[pl.BlockSpec((tm, 3 * BR_W), lambda i: (i, 0)),
                  pl.BlockSpec((tm, GL), lambda i: (i, glb)), wspec, wspec, wspec],
        out_specs=pl.BlockSpec((tm, D), lambda i: (i, 0)),
        out_shape=jax.ShapeDtypeStruct((T, D), BF16),
        compiler_params=_cparams(("parallel",)),
    )(cat, z, w_pa, w_pb, w_pc)


def _proj_bwd(dmerged, cat, z, w_pa, w_pb, w_pc, layer, D, name):
    T, IN = z.shape
    GL = 3 * D
    glb = N_SPLIT // GL
    tm = _pick(T, 256, HALO)

    def body(dm_ref, cat_ref, gl_ref, wa_ref, wb_ref, wc_ref, dy_ref, dgl_ref, dcat_ref):
        dm = dm_ref[...].astype(F32)
        for k, w_ref in enumerate((wa_ref, wb_ref, wc_ref)):
            w = w_ref[...]
            y = jnp.dot(cat_ref[:, k * BR_W:(k + 1) * BR_W], w, preferred_element_type=F32)
            sg = _sigmoid(gl_ref[:, k * D:(k + 1) * D].astype(F32))
            dyk = (dm * sg).astype(BF16)
            dy_ref[:, k * D:(k + 1) * D] = dyk
            dgl_ref[:, k * D:(k + 1) * D] = (dm * y * sg * (1.0 - sg)).astype(BF16)
            dcat_ref[:, k * BR_W:(k + 1) * BR_W] = lax.dot_general(
                dyk, w, (((1,), (1,)), ((), ())), preferred_element_type=F32).astype(BF16)

    wspec = pl.BlockSpec((None, BR_W, D), lambda i: (layer, 0, 0))
    return pl.pallas_call(
        body, name=name, grid=(T // tm,),
        in_specs=[pl.BlockSpec((tm, D), lambda i: (i, 0)),
                  pl.BlockSpec((tm, 3 * BR_W), lambda i: (i, 0)),
                  pl.BlockSpec((tm, GL), lambda i: (i, glb)), wspec, wspec, wspec],
        out_specs=(pl.BlockSpec((tm, GL), lambda i: (i, 0)), pl.BlockSpec((tm, GL), lambda i: (i, 0)),
                   pl.BlockSpec((tm, 3 * BR_W), lambda i: (i, 0))),
        out_shape=(jax.ShapeDtypeStruct((T, GL), BF16), jax.ShapeDtypeStruct((T, GL), BF16),
                   jax.ShapeDtypeStruct((T, 3 * BR_W), BF16)),
        compiler_params=_cparams(("parallel",)),
    )(dmerged, cat, z, w_pa, w_pb, w_pc)


def _loss_head(x, target, g, name):
    T, D = x.shape
    tm = _pick(T, 512, SUBLANE)

    def body(x_ref, t_ref, g_ref, dx_ref, part_ref):
        i = pl.program_id(0)
        xv = x_ref[...]
        gv = g_ref[...]
        rstd = lax.rsqrt(jnp.mean(xv * xv, axis=-1, keepdims=True) + EPS)
        xhat = xv * rstd
        err = xhat * gv - t_ref[...]
        dy = err * (1.0 / D)
        dxhat = dy * gv
        dx_ref[...] = rstd * (dxhat - xhat * jnp.mean(dxhat * xhat, axis=-1, keepdims=True))

        @pl.when(i == 0)
        def _():
            part_ref[...] = jnp.zeros_like(part_ref)

        part_ref[0:1, :] += jnp.sum(dy * xhat, axis=0, keepdims=True)
        part_ref[1:2, :] += jnp.zeros((1, D), F32) + (0.5 / D) * jnp.sum(err * err)

    row = pl.BlockSpec((tm, D), lambda i: (i, 0))
    return pl.pallas_call(
        body, name=name, grid=(T // tm,),
        in_specs=[row, row, pl.BlockSpec((1, D), lambda i: (0, 0))],
        out_specs=(row, pl.BlockSpec((SUBLANE, D), lambda i: (0, 0))),
        out_shape=(jax.ShapeDtypeStruct((T, D), F32), jax.ShapeDtypeStruct((SUBLANE, D), F32)),
        compiler_params=_cparams(("arbitrary",)),
    )(x, target, g)


def _mod_fwd(c_all, w_mod, b_cols, name):
    L, D, N4 = w_mod.shape
    Bg = c_all.shape[0]
    tn = _pick(N4, 768)

    def body(c_ref, w_ref, b_ref, o_ref):
        cv = c_ref[...]
        ca = (cv * _sigmoid(cv)).astype(BF16)
        o_ref[...] = jnp.dot(ca, w_ref[...].astype(BF16), preferred_element_type=F32) + b_ref[...]

    return pl.pallas_call(
        body, name=name, grid=(L, N4 // tn),
        in_specs=[pl.BlockSpec((Bg, D), lambda l, j: (0, 0)),
                  pl.BlockSpec((None, D, tn), lambda l, j: (l, 0, j)),
                  pl.BlockSpec((None, 1, tn), lambda l, j: (l, 0, j))],
        out_specs=pl.BlockSpec((None, Bg, tn), lambda l, j: (l, 0, j)),
        out_shape=jax.ShapeDtypeStruct((L, Bg, N4), F32),
        compiler_params=_cparams(("parallel", "parallel")),
    )(c_all, w_mod, b_cols)


def _mod_wgrad(c_all, dmod_cols, name):
    L, Bg, N4 = dmod_cols.shape
    D = c_all.shape[1]
    tn = _pick(N4, 768)

    def body(c_ref, d_ref, o_ref):
        cv = c_ref[...]
        ca = (cv * _sigmoid(cv)).astype(BF16)
        o_ref[...] = lax.dot_general(ca, d_ref[...].astype(BF16), (((0,), (0,)), ((), ())),
                                     preferred_element_type=F32)

    return pl.pallas_call(
        body, name=name, grid=(L, N4 // tn),
        in_specs=[pl.BlockSpec((Bg, D), lambda l, j: (0, 0)),
                  pl.BlockSpec((None, Bg, tn), lambda l, j: (l, 0, j))],
        out_specs=pl.BlockSpec((None, D, tn), lambda l, j: (l, 0, j)),
        out_shape=jax.ShapeDtypeStruct((L, D, N4), F32),
        compiler_params=_cparams(("parallel", "parallel")),
    )(c_all, dmod_cols)


def _rows_tile(R, C):
    return _pick(R, max(SUBLANE, (256 * 1024) // C), SUBLANE)


def _cast_bf16(w, name):
    L, R, C = w.shape
    w2 = w.reshape(L * R, C)
    tr = _pick(L * R, max(HALO, (512 * 1024) // C), HALO)

    def body(w_ref, o_ref):
        o_ref[...] = w_ref[...].astype(BF16)

    spec = pl.BlockSpec((tr, C), lambda i: (i, 0))
    out = pl.pallas_call(
        body, name=name, grid=(L * R // tr,), in_specs=[spec], out_specs=spec,
        out_shape=jax.ShapeDtypeStruct((L * R, C), BF16), compiler_params=_cparams(("parallel",)),
    )(w2)
    return out.reshape(L, R, C)


def _sum_slots(slots, name):
    _, R, C = slots.shape
    tr = _rows_tile(R, C)

    def body(s_ref, o_ref):
        o_ref[...] = ((s_ref[3].astype(F32) + s_ref[0].astype(F32)) + s_ref[1].astype(F32)) + s_ref[2].astype(F32)

    return pl.pallas_call(
        body, name=name, grid=(R // tr,),
        in_specs=[pl.BlockSpec((N_CHIP, tr, C), lambda i: (0, i, 0))],
        out_specs=pl.BlockSpec((tr, C), lambda i: (i, 0)),
        out_shape=jax.ShapeDtypeStruct((R, C), F32), compiler_params=_cparams(("parallel",)),
    )(slots)


def _sum_devices(parts, name):
    n, R, C = parts.shape
    tr = _rows_tile(R, C)

    def body(s_ref, o_ref):
        acc = s_ref[0]
        for d in range(1, n):
            acc = acc + s_ref[d]
        o_ref[...] = acc

    return pl.pallas_call(
        body, name=name, grid=(R // tr,),
        in_specs=[pl.BlockSpec((n, tr, C), lambda i: (0, i, 0))],
        out_specs=pl.BlockSpec((tr, C), lambda i: (i, 0)),
        out_shape=jax.ShapeDtypeStruct((R, C), F32), compiler_params=_cparams(("parallel",)),
    )(parts)


def _adamw(w, m, v, grads, name):
    R, C = w.shape
    tr = _rows_tile(R, C)
    bc1 = 1.0 - ADAM_B1 ** ADAM_STEP
    bc2 = 1.0 - ADAM_B2 ** ADAM_STEP
    ng = len(grads)

    def body(*refs):
        w_ref, m_ref, v_ref = refs[:3]
        g_refs = refs[3:3 + ng]
        g_out, d_out, m_out, v_out = refs[3 + ng:]
        g = g_refs[0][...]
        for r in g_refs[1:]:
            g = g + r[...]
        mn = ADAM_B1 * m_ref[...] + (1.0 - ADAM_B1) * g
        vn = ADAM_B2 * v_ref[...] + (1.0 - ADAM_B2) * (g * g)
        m_hat = mn / bc1
        v_hat = vn / bc2
        g_out[...] = g
        d_out[...] = -ADAM_LR * (m_hat / (jnp.sqrt(v_hat) + ADAM_EPS) + ADAM_WD * w_ref[...])
        m_out[...] = mn
        v_out[...] = vn

    spec = pl.BlockSpec((tr, C), lambda i: (i, 0))
    sds = jax.ShapeDtypeStruct((R, C), F32)
    return pl.pallas_call(
        body, name=name, grid=(R // tr,), in_specs=[spec] * (3 + ng), out_specs=(spec,) * 4,
        out_shape=(sds,) * 4, compiler_params=_cparams(("parallel",)),
    )(w, m, v, *grads)


def _place():
    x, y, c = lax.axis_index("x"), lax.axis_index("y"), lax.axis_index("c")
    chips = [(1 - x, y), (x, 1 - y), (1 - x, 1 - y)]
    return x, y, c, chips


ANY = pl.BlockSpec(memory_space=pl.ANY)


def _allgather8(v, name):
    m_per, n = v.shape

    def body(x_ref, out_ref, send_sems, recv_sems, local_sem):
        x, y, c, chips = _place()
        me, sibling = (x, y, c), (x, y, 1 - c)

        def rows(px, py, pc):
            return out_ref.at[pl.ds((4 * px + 2 * py + pc) * m_per, m_per), :]

        def copy(k, block, to, src=None):
            return pltpu.make_async_remote_copy(
                src_ref=rows(*block) if src is None else src, dst_ref=rows(*block),
                send_sem=send_sems.at[k], recv_sem=recv_sems.at[k], device_id=to, device_id_type=MESH)

        mine = pltpu.make_async_copy(x_ref, rows(*me), local_sem)
        mine.start()
        first = [copy(0, me, sibling, src=x_ref)]
        first += [copy(1 + j, me, (*chip, c), src=x_ref) for j, chip in enumerate(chips)]
        for cp in first:
            cp.start()
        passed = [copy(4 + j, (*chip, c), sibling) for j, chip in enumerate(chips)]
        for j, chip in enumerate(chips):
            copy(1 + j, (*chip, c), me).wait_recv()
            passed[j].start()
        copy(0, sibling, me).wait_recv()
        for j, chip in enumerate(chips):
            copy(4 + j, (*chip, 1 - c), me).wait_recv()
        for cp in first + passed:
            cp.wait_send()
        mine.wait()

    return pl.pallas_call(
        body, name=name, out_shape=jax.ShapeDtypeStruct((N_DEV * m_per, n), v.dtype),
        in_specs=[ANY], out_specs=ANY,
        scratch_shapes=[pltpu.SemaphoreType.DMA((7,)), pltpu.SemaphoreType.DMA((7,)), pltpu.SemaphoreType.DMA],
    )(v)


def _window(ref, col_sharded, q, lead):
    full = (slice(None),) * lead
    if col_sharded:
        width = ref.shape[-1] // N_CHIP
        return ref.at[full + (slice(None), pl.ds(pl.multiple_of(q * width, LANE), width))]
    height = ref.shape[-2] // N_CHIP
    return ref.at[full + (pl.ds(pl.multiple_of(q * height, HALO), height), slice(None))]


def _gather_weights(shards, col_sharded, name):
    n = len(shards)
    outs = []
    for s, cs in zip(shards, col_sharded):
        L, R, C = s.shape
        outs.append(jax.ShapeDtypeStruct((L, R, C * N_CHIP) if cs else (L, R * N_CHIP, C), s.dtype))

    def body(*refs):
        ins, full = refs[:n], refs[n:2 * n]
        send_sems, recv_sems, local_sems = refs[2 * n:]
        x, y, c, chips = _place()
        q = 2 * x + y
        copies, locals_ = [], []
        for w in range(n):
            mine = pltpu.make_async_copy(ins[w], _window(full[w], col_sharded[w], q, 1), local_sems.at[w])
            mine.start()
            locals_.append(mine)
            for k, chip in enumerate(chips):
                cp = pltpu.make_async_remote_copy(
                    src_ref=ins[w], dst_ref=_window(full[w], col_sharded[w], q, 1),
                    send_sem=send_sems.at[3 * w + k], recv_sem=recv_sems.at[3 * w + k],
                    device_id=(*chip, c), device_id_type=MESH)
                cp.start()
                copies.append(cp)
        for cp in copies:
            cp.wait()
        for mine in locals_:
            mine.wait()

    return pl.pallas_call(
        body, name=name, out_shape=tuple(outs), in_specs=[ANY] * n, out_specs=(ANY,) * n,
        scratch_shapes=[pltpu.SemaphoreType.DMA((3 * n,)), pltpu.SemaphoreType.DMA((3 * n,)),
                        pltpu.SemaphoreType.DMA((n,))],
    )(*shards)


def _scatter_grads(grads, col_sharded, name):
    n = len(grads)
    L = len(grads[0])
    flat = [g for per_w in grads for g in per_w]
    outs = []
    for per_w, cs in zip(grads, col_sharded):
        K, N = per_w[0].shape
        outs.append(jax.ShapeDtypeStruct((N_CHIP, L, K, N // N_CHIP) if cs else (N_CHIP, L, K // N_CHIP, N), BF16))

    def body(*refs):
        ins = refs[:n * L]
        slots = refs[n * L:n * L + n]
        send_sems, recv_sems, local_sems = refs[n * L + n:]
        x, y, c, chips = _place()
        q = 2 * x + y
        copies, locals_ = [], []
        for w in range(n):
            for l in range(L):
                g_ref = ins[w * L + l]
                mine = pltpu.make_async_copy(_window(g_ref, col_sharded[w], q, 0), slots[w].at[3, l],
                                             local_sems.at[w * L + l])
                mine.start()
                locals_.append(mine)
                for k, (cx, cy) in enumerate(chips):
                    s = 3 * (w * L + l) + k
                    cp = pltpu.make_async_remote_copy(
                        src_ref=_window(g_ref, col_sharded[w], 2 * cx + cy, 0), dst_ref=slots[w].at[k, l],
                        send_sem=send_sems.at[s], recv_sem=recv_sems.at[s],
                        device_id=(cx, cy, c), device_id_type=MESH)
                    cp.start()
                    copies.append(cp)
        for cp in copies:
            cp.wait()
        for mine in locals_:
            mine.wait()

    return pl.pallas_call(
        body, name=name, out_shape=tuple(outs), in_specs=[ANY] * (n * L), out_specs=(ANY,) * n,
        scratch_shapes=[pltpu.SemaphoreType.DMA((3 * n * L,)), pltpu.SemaphoreType.DMA((3 * n * L,)),
                        pltpu.SemaphoreType.DMA((n * L,))],
    )(*flat)


def _swap_with_sibling(arrays, name):
    n = len(arrays)

    def body(*refs):
        ins, outs = refs[:n], refs[n:2 * n]
        send_sems, recv_sems = refs[2 * n:]
        x, y, c, _ = _place()
        copies = []
        for w in range(n):
            cp = pltpu.make_async_remote_copy(
                src_ref=ins[w], dst_ref=outs[w], send_sem=send_sems.at[w], recv_sem=recv_sems.at[w],
                device_id=(x, y, 1 - c), device_id_type=MESH)
            cp.start()
            copies.append(cp)
        for cp in copies:
            cp.wait()

    return pl.pallas_call(
        body, name=name, out_shape=tuple(jax.ShapeDtypeStruct(a.shape, a.dtype) for a in arrays),
        in_specs=[ANY] * n, out_specs=(ANY,) * n,
        scratch_shapes=[pltpu.SemaphoreType.DMA((n,)), pltpu.SemaphoreType.DMA((n,))],
    )(*arrays)


BIG = ("w_in", "w_pa", "w_pb", "w_pc", "w_o", "w_13", "w_2")
BIG_COL_SHARDED = (True, True, True, True, False, True, False)
SMALL = ("b_mod", "g_mix", "gm_ln_g", "gm_ln_b", "gm_w_s", "gm_b_s", "pool_w", "pool_scale", "conv_w",
         "g_ffn", "g_final")
WEIGHTS = ("w_mod", "b_mod", "g_mix", "w_in", "gm_ln_g", "gm_ln_b", "gm_w_s", "gm_b_s", "w_pa", "pool_w",
           "pool_scale", "w_pb", "conv_w", "w_pc", "w_o", "g_ffn", "w_13", "w_2", "g_final")


def _pack(arrays, width):
    flat = jnp.concatenate([a.reshape(-1) for a in arrays])
    rows = -(-flat.shape[0] // width)
    rows = -(-rows // SUBLANE) * SUBLANE
    flat = jnp.pad(flat, (0, rows * width - flat.shape[0]))
    return flat.reshape(rows, width)


def _unpack(packed, shapes):
    flat = packed.reshape(-1)
    out, off = [], 0
    for s in shapes:
        size = 1
        for d in s:
            size *= d
        out.append(flat[off:off + size].reshape(s))
        off += size
    return out


def kernel(x, c, w_mod, b_mod, g_mix, w_in, gm_ln_g, gm_ln_b, gm_w_s, gm_b_s, w_pa, pool_w, pool_scale, w_pb, conv_w, w_pc, w_o, g_ffn, w_13, w_2, g_final, loss_target, m_w_mod, m_b_mod, m_g_mix, m_w_in, m_gm_ln_g, m_gm_ln_b, m_gm_w_s, m_gm_b_s, m_w_pa, m_pool_w, m_pool_scale, m_w_pb, m_conv_w, m_w_pc, m_w_o, m_g_ffn, m_w_13, m_w_2, m_g_final, v_w_mod, v_b_mod, v_g_mix, v_w_in, v_gm_ln_g, v_gm_ln_b, v_gm_w_s, v_gm_b_s, v_w_pa, v_pool_w, v_pool_scale, v_w_pb, v_conv_w, v_w_pc, v_w_o, v_g_ffn, v_w_13, v_w_2, v_g_final):
    W = dict(w_mod=w_mod, b_mod=b_mod, g_mix=g_mix, w_in=w_in, gm_ln_g=gm_ln_g, gm_ln_b=gm_ln_b, gm_w_s=gm_w_s,
             gm_b_s=gm_b_s, w_pa=w_pa, pool_w=pool_w, pool_scale=pool_scale, w_pb=w_pb, conv_w=conv_w, w_pc=w_pc,
             w_o=w_o, g_ffn=g_ffn, w_13=w_13, w_2=w_2, g_final=g_final)
    Mo = dict(w_mod=m_w_mod, b_mod=m_b_mod, g_mix=m_g_mix, w_in=m_w_in, gm_ln_g=m_gm_ln_g, gm_ln_b=m_gm_ln_b,
              gm_w_s=m_gm_w_s, gm_b_s=m_gm_b_s, w_pa=m_w_pa, pool_w=m_pool_w, pool_scale=m_pool_scale, w_pb=m_w_pb,
              conv_w=m_conv_w, w_pc=m_w_pc, w_o=m_w_o, g_ffn=m_g_ffn, w_13=m_w_13, w_2=m_w_2, g_final=m_g_final)
    Vo = dict(w_mod=v_w_mod, b_mod=v_b_mod, g_mix=v_g_mix, w_in=v_w_in, gm_ln_g=v_gm_ln_g, gm_ln_b=v_gm_ln_b,
              gm_w_s=v_gm_w_s, gm_b_s=v_gm_b_s, w_pa=v_w_pa, pool_w=v_pool_w, pool_scale=v_pool_scale, w_pb=v_w_pb,
              conv_w=v_conv_w, w_pc=v_w_pc, w_o=v_w_o, g_ffn=v_g_ffn, w_13=v_w_13, w_2=v_w_2, g_final=v_g_final)

    B, S, D = x.shape
    T = B * S
    L = w_in.shape[0]
    Bg = B * N_DEV
    N4 = w_mod.shape[2]
    CW = conv_w.shape[2]
    xi, yi, ci = lax.axis_index("x"), lax.axis_index("y"), lax.axis_index("c")
    chip = 2 * xi + yi
    dev = 2 * chip + ci

    head = _pack([c, conv_w], D)
    hrows = head.shape[0]
    got = _allgather8(head, "gather_c_conv").reshape(N_DEV, hrows * D)
    c_all = got[:, :B * D].reshape(Bg, D)
    conv_parts = got[:, B * D:B * D + L * 3 * CW].reshape(N_CHIP, 2, L, 3, CW)[:, 0]
    conv_full = jnp.transpose(conv_parts, (1, 2, 0, 3)).reshape(L, 3, N_CHIP * CW)

    b_cols = lax.dynamic_slice_in_dim(b_mod, chip * N4, N4, axis=1).reshape(L, 1, N4)
    mod_part = _mod_fwd(c_all, w_mod, b_cols, "mod_fwd")
    half = Bg // 2
    mine = lax.dynamic_slice_in_dim(mod_part, ci * half, half, axis=1)
    mod_got = _allgather8(mine.reshape(L * half, N4), "gather_mod").reshape(N_CHIP, 2, L, half, N4)
    mod_full = jnp.transpose(mod_got, (2, 1, 3, 0, 4)).reshape(L, Bg, 6, D)
    mod_mine = lax.dynamic_slice_in_dim(mod_full, dev * B, B, axis=1)
    mod = jnp.pad(mod_mine, ((0, 0), (0, 0), (0, SUBLANE - 6), (0, 0)))

    shards = [_cast_bf16(W[n], "cast_" + n) for n in BIG]
    full = dict(zip(BIG, _gather_weights(shards, BIG_COL_SHARDED, "gather_weights")))

    def mixer_params(l):
        vec = jnp.zeros((SUBLANE, BR_W), F32)
        vec = vec.at[0].set(gm_ln_g[l]).at[1].set(gm_ln_b[l]).at[2].set(pool_scale[l])
        conv = jnp.zeros((SUBLANE, BR_W), F32).at[0:3].set(conv_full[l])
        b_s = jnp.zeros((CHUNK, LANE), F32).at[:, 0:HEADS].set(jnp.transpose(gm_b_s[l]))
        return dict(vec=vec, conv=conv, w_s=gm_w_s[l], b_s=b_s, pool_w=pool_w[l])

    xs = x.reshape(T, D)
    saved = []
    for l in range(L):
        prm = mixer_params(l)
        h = _norm_fwd(xs, g_mix[l].reshape(1, D), mod[l], 0, 1, S, "norm_mix_fwd")
        z = _matmul(h, full["w_in"], mode="nn", name="mm_in", layer=l)
        cat = _mixer_fwd(z, prm, S, "mixer_fwd")
        merged = _proj_fwd(cat, z, full["w_pa"], full["w_pb"], full["w_pc"], l, D, "proj_fwd")
        x1, mo = _matmul(merged, full["w_o"], mode="nn", name="mm_o", layer=l, resid=(xs, mod[l], 2, S))
        h2 = _norm_fwd(x1, g_ffn[l].reshape(1, D), mod[l], 3, 4, S, "norm_ffn_fwd")
        ab = _matmul(h2, full["w_13"], mode="nn", name="mm_13", layer=l)
        act = _swiglu_fwd(ab, "swiglu_fwd")
        x2, ffo = _matmul(act, full["w_2"], mode="nn", name="mm_2", layer=l, resid=(x1, mod[l], 5, S))
        saved.append(dict(prm=prm, x0=xs, z=z, cat=cat, merged=merged, mo=mo, x1=x1, ab=ab, act=act, ffo=ffo))
        xs = x2

    dx, head_part = _loss_head(xs, loss_target.reshape(T, D), g_final.reshape(1, D), "loss_head")
    loss = lax.psum(head_part[1, 0], ("x", "y", "c"))

    big_grads = {n: [None] * L for n in BIG}
    small_part = {n: [None] * L for n in SMALL if n not in ("b_mod", "g_final")}
    dmod = [None] * L
    for l in reversed(range(L)):
        sv = saved[l]
        dffo, pg2 = _gate_bwd(dx, sv["ffo"], mod[l], 5, S, "gate_ffn_bwd")
        dact = _matmul(dffo, full["w_2"], mode="nt", name="mm_2_dgrad", layer=l)
        dab = _swiglu_bwd(dact, sv["ab"], "swiglu_bwd")
        dh2 = _matmul(dab, full["w_13"], mode="nt", name="mm_13_dgrad", layer=l)
        dx1, h2, pb2, pgf = _norm_bwd(sv["x1"], dh2, dx, g_ffn[l].reshape(1, D), mod[l], 3, 4, S, "norm_ffn_bwd")
        big_grads["w_2"][l] = _matmul(sv["act"], dffo, mode="tn", name="mm_2_wgrad")
        big_grads["w_13"][l] = _matmul(h2, dab, mode="tn", name="mm_13_wgrad")

        dmo, pg1 = _gate_bwd(dx1, sv["mo"], mod[l], 2, S, "gate_mix_bwd")
        dmerged = _matmul(dmo, full["w_o"], mode="nt", name="mm_o_dgrad", layer=l)
        big_grads["w_o"][l] = _matmul(sv["merged"], dmo, mode="tn", name="mm_o_wgrad")
        dy, dgl, dcat = _proj_bwd(dmerged, sv["cat"], sv["z"], full["w_pa"], full["w_pb"], full["w_pc"], l, D,
                                  "proj_bwd")
        for k, n in enumerate(("w_pa", "w_pb", "w_pc")):
            big_grads[n][l] = _matmul(sv["cat"], dy, mode="tn", name="mm_proj_wgrad",
                                      a_cols=(k * BR_W, BR_W), b_cols=(k * D, D))
        dz, pv, dws, dbs, dpw = _mixer_bwd(sv["z"], dcat, dgl, sv["prm"], S, "mixer_bwd")
        dh = _matmul(dz, full["w_in"], mode="nt", name="mm_in_dgrad", layer=l)
        dx0, h, pb1, pgm = _norm_bwd(sv["x0"], dh, dx1, g_mix[l].reshape(1, D), mod[l], 0, 1, S, "norm_mix_bwd")
        big_grads["w_in"][l] = _matmul(h, dz, mode="tn", name="mm_in_wgrad")
        dx = dx0

        dmod[l] = jnp.stack([pb1[:, 0], pb1[:, 1], pg1[:, 0], pb2[:, 0], pb2[:, 1], pg2[:, 0]], axis=1)
        small_part["g_mix"][l] = pgm[0]
        small_part["g_ffn"][l] = pgf[0]
        small_part["gm_ln_g"][l] = pv[0]
        small_part["gm_ln_b"][l] = pv[1]
        small_part["pool_scale"][l] = pv[2]
        small_part["conv_w"][l] = pv[3:6]
        small_part["gm_w_s"][l] = dws
        small_part["gm_b_s"][l] = jnp.transpose(dbs[:, 0:HEADS])
        small_part["pool_w"][l] = dpw
    grad_x = dx.reshape(B, S, D)

    slots = _scatter_grads([big_grads[n] for n in BIG], BIG_COL_SHARDED, "scatter_grads")
    sums = []
    for n, sl in zip(BIG, slots):
        _, _, R, C = sl.shape
        sums.append(_sum_slots(sl.reshape(N_CHIP, L * R, C), "sum_slots_" + n))
    others = _swap_with_sibling(sums, "swap_sums")
    results = {}
    for n, own, other in zip(BIG, sums, others):
        _, R, C = W[n].shape
        res = _adamw(W[n].reshape(L * R, C), Mo[n].reshape(L * R, C), Vo[n].reshape(L * R, C), [own, other],
                     "adamw_" + n)
        results[n] = [r.reshape(L, R, C) for r in res]

    dmod_l = jnp.stack(dmod, axis=0).reshape(L * B, 6 * D)
    dmod_rows = -(-(L * B) // SUBLANE) * SUBLANE
    dmod_got = _allgather8(jnp.pad(dmod_l, ((0, dmod_rows - L * B), (0, 0))), "gather_dmod")
    dmod_all = dmod_got.reshape(N_DEV, dmod_rows, 6 * D)[:, :L * B].reshape(N_DEV, L, B, 6 * D)
    dmod_all = jnp.transpose(dmod_all, (1, 0, 2, 3)).reshape(L, Bg, 6 * D)
    dmod_cols = lax.dynamic_slice_in_dim(dmod_all, chip * N4, N4, axis=2)
    g_wmod = _mod_wgrad(c_all, dmod_cols, "mod_wgrad")
    res = _adamw(w_mod.reshape(L * D, N4), m_w_mod.reshape(L * D, N4), v_w_mod.reshape(L * D, N4),
                 [g_wmod.reshape(L * D, N4)], "adamw_w_mod")
    results["w_mod"] = [r.reshape(L, D, N4) for r in res]

    local = dict(b_mod=jnp.sum(jnp.stack(dmod, axis=0), axis=1).reshape(L, 6 * D), g_final=head_part[0])
    for n in small_part:
        local[n] = jnp.stack(small_part[n], axis=0)
    pshapes = [local[n].shape for n in SMALL]
    packed = _pack([local[n] for n in SMALL], LANE)
    prow = packed.shape[0]
    gathered = _allgather8(packed, "gather_small").reshape(N_DEV, prow, LANE)
    g_small = dict(zip(SMALL, _unpack(_sum_devices(gathered, "sum_small"), pshapes)))
    g_small["conv_w"] = lax.dynamic_slice_in_dim(g_small["conv_w"], chip * CW, CW, axis=2)
    wshapes = [W[n].shape for n in SMALL]
    res = _adamw(_pack([W[n] for n in SMALL], LANE), _pack([Mo[n] for n in SMALL], LANE),
                 _pack([Vo[n] for n in SMALL], LANE), [_pack([g_small[n] for n in SMALL], LANE)], "adamw_small")
    small_res = [_unpack(r, wshapes) for r in res]
    for i, n in enumerate(SMALL):
        results[n] = [small_res[j][i] for j in range(4)]

    return (loss, grad_x, *[results[n][0] for n in WEIGHTS], *[results[n][1] for n in WEIGHTS],
            *[results[n][2] for n in WEIGHTS], *[results[n][3] for n in WEIGHTS])
```

```python
import jax
import jax.numpy as jnp
from jax import lax
from jax.experimental import pallas as pl
from jax.experimental.pallas import tpu as pltpu

F32 = jnp.float32
BF16 = jnp.bfloat16
MESH = pl.DeviceIdType.MESH

EPS = 1e-6
CHUNK = 128
HEADS = 4
HEAD_DIM = 128
BR_W = 512
N_GROUP = 4
GROUP_DIM = 128
HALO = 16
N_SPLIT = 6 * BR_W
N_CHIP = 4
N_DEV = 8

ADAM_LR = 0.001
ADAM_B1 = 0.9
ADAM_B2 = 0.999
ADAM_EPS = 1e-08
ADAM_WD = 0.01
ADAM_STEP = 10

V7X_VMEM_LIMIT = 48 * 1024 * 1024
LANE = 128
SUBLANE = 8

GELU_K = 0.7978845608028654
GELU_C = 0.044715


def _cparams(sem):
    return pltpu.CompilerParams(dimension_semantics=sem, vmem_limit_bytes=V7X_VMEM_LIMIT)


def _pick(n, cap, q=LANE):
    best = None
    d = q
    while d <= min(n, cap):
        if n % d == 0:
            best = d
        d += q
    return n if best is None else best


def _sigmoid(x):
    return 1.0 / (1.0 + jnp.exp(-x))


def _gelu(x):
    t = jnp.tanh(GELU_K * (x + GELU_C * x * x * x))
    return 0.5 * x * (1.0 + t), t


def _gelu_grad(x, t):
    return 0.5 * (1.0 + t) + 0.5 * x * (1.0 - t * t) * GELU_K * (1.0 + 3.0 * GELU_C * x * x)


def _matmul(a, b, *, mode, name, out_dtype=None, layer=None, a_cols=None, b_cols=None,
            resid=None, tm_cap=1024, tn_cap=1536, tk_cap=1536):
    out_dtype = BF16 if out_dtype is None else out_dtype
    b2 = b.shape[-2:]
    if mode == "nn":
        M, K = a.shape
        N = b2[1]
    elif mode == "nt":
        M, K = a.shape
        N = b2[0]
    else:
        K = a.shape[0]
        M = a.shape[1] if a_cols is None else a_cols[1]
        N = b2[1] if b_cols is None else b_cols[1]
    tm = _pick(M if resid is None else resid[3], tm_cap)
    tn = _pick(N, tn_cap)
    tk = _pick(K, tk_cap)
    nk = K // tk
    a_off = 0 if a_cols is None else a_cols[0] // tm
    b_off = 0 if b_cols is None else b_cols[0] // tn
    if a_cols is not None:
        assert a_cols[0] % tm == 0
    if b_cols is not None:
        assert b_cols[0] % tn == 0

    if mode == "nn":
        a_spec = pl.BlockSpec((tm, tk), lambda i, j, k: (i, k))
        b_blk, b_idx = (tk, tn), (lambda i, j, k: (k, j))
        dims = (((1,), (0,)), ((), ()))
    elif mode == "nt":
        a_spec = pl.BlockSpec((tm, tk), lambda i, j, k: (i, k))
        b_blk, b_idx = (tn, tk), (lambda i, j, k: (j, k))
        dims = (((1,), (1,)), ((), ()))
    else:
        a_spec = pl.BlockSpec((tk, tm), lambda i, j, k: (k, i + a_off))
        b_blk, b_idx = (tk, tn), (lambda i, j, k: (k, j + b_off))
        dims = (((0,), (0,)), ((), ()))
    if layer is None:
        b_spec = pl.BlockSpec(b_blk, b_idx)
    else:
        b_spec = pl.BlockSpec((None,) + b_blk, lambda i, j, k: (layer,) + b_idx(i, j, k))

    in_specs = [a_spec, b_spec]
    operands = [a, b]
    o_spec = pl.BlockSpec((tm, tn), lambda i, j, k: (i, j))
    if resid is not None:
        x, mod, row, seq = resid
        D = mod.shape[-1]
        in_specs += [o_spec, pl.BlockSpec((1, SUBLANE, tn), lambda i, j, k: ((i * tm) // seq, 0, j))]
        operands += [x, mod]
        out_shape = (jax.ShapeDtypeStruct((M, N), F32), jax.ShapeDtypeStruct((M, N), BF16))
        out_specs = (o_spec, o_spec)
        assert seq % tm == 0 and D == N
    else:
        out_shape = jax.ShapeDtypeStruct((M, N), out_dtype)
        out_specs = o_spec

    def finish(acc, refs):
        if resid is not None:
            x_ref, mod_ref, o_ref, p_ref = refs
            o_ref[...] = x_ref[...] + mod_ref[0, row:row + 1, :] * acc
            p_ref[...] = acc.astype(BF16)
        else:
            (o_ref,) = refs
            o_ref[...] = acc.astype(out_dtype)

    def body(a_ref, b_ref, *refs):
        part = lax.dot_general(a_ref[...], b_ref[...], dims, preferred_element_type=F32)
        if nk == 1:
            finish(part, refs)
            return
        acc_ref = refs[-1]
        k = pl.program_id(2)

        @pl.when(k == 0)
        def _():
            acc_ref[...] = part

        @pl.when(k > 0)
        def _():
            acc_ref[...] += part

        @pl.when(k == nk - 1)
        def _():
            finish(acc_ref[...], refs[:-1])

    scratch = [] if nk == 1 else [pltpu.VMEM((tm, tn), F32)]
    return pl.pallas_call(
        body, name=name, grid=(M // tm, N // tn, nk), in_specs=in_specs, out_specs=out_specs,
        out_shape=out_shape, scratch_shapes=scratch,
        compiler_params=_cparams(("parallel", "parallel", "arbitrary")),
    )(*operands)


def _row_tile(seq, cap):
    return _pick(seq, cap, HALO)


def _mod_spec(tm, seq, D):
    return pl.BlockSpec((1, SUBLANE, D), lambda i: ((i * tm) // seq, 0, 0))


def _norm_fwd(x, g, mod, shift_row, scale_row, seq, name, deps=()):
    T, D = x.shape
    tm = _row_tile(seq, 512)

    def body(x_ref, g_ref, mod_ref, *rest):
        h_ref = rest[-1]
        xv = x_ref[...]
        rstd = lax.rsqrt(jnp.mean(xv * xv, axis=-1, keepdims=True) + EPS)
        n = xv * rstd * g_ref[...]
        h = n * (1.0 + mod_ref[0, scale_row:scale_row + 1, :]) + mod_ref[0, shift_row:shift_row + 1, :]
        h_ref[...] = h.astype(BF16)

    row = pl.BlockSpec((tm, D), lambda i: (i, 0))
    return pl.pallas_call(
        body, name=name, grid=(T // tm,),
        in_specs=[row, pl.BlockSpec((1, D), lambda i: (0, 0)), _mod_spec(tm, seq, D)] + [ANY] * len(deps),
        out_specs=row, out_shape=jax.ShapeDtypeStruct((T, D), BF16),
        compiler_params=_cparams(("parallel",)),
    )(x, g, mod, *deps)


def _norm_bwd(x, dh, dres, g, mod, shift_row, scale_row, seq, name):
    T, D = x.shape
    B = mod.shape[0]
    tm = _row_tile(seq, 512)
    per_seq = seq // tm

    def body(x_ref, dh_ref, dres_ref, g_ref, mod_ref, dx_ref, h_ref, pb_ref, pg_ref):
        i = pl.program_id(0)
        xv = x_ref[...]
        dhv = dh_ref[...].astype(F32)
        gv = g_ref[...]
        scale1 = 1.0 + mod_ref[0, scale_row:scale_row + 1, :]
        rstd = lax.rsqrt(jnp.mean(xv * xv, axis=-1, keepdims=True) + EPS)
        xhat = xv * rstd
        n = xhat * gv
        dn = dhv * scale1
        dxhat = dn * gv
        dx = rstd * (dxhat - xhat * jnp.mean(dxhat * xhat, axis=-1, keepdims=True))
        dx_ref[...] = dres_ref[...] + dx
        h_ref[...] = (n * scale1 + mod_ref[0, shift_row:shift_row + 1, :]).astype(BF16)

        @pl.when(i % per_seq == 0)
        def _():
            pb_ref[...] = jnp.zeros_like(pb_ref)

        @pl.when(i == 0)
        def _():
            pg_ref[...] = jnp.zeros_like(pg_ref)

        pb_ref[0, 0:1, :] += jnp.sum(dhv, axis=0, keepdims=True)
        pb_ref[0, 1:2, :] += jnp.sum(dhv * n, axis=0, keepdims=True)
        pg_ref[0:1, :] += jnp.sum(dn * xhat, axis=0, keepdims=True)

    row = pl.BlockSpec((tm, D), lambda i: (i, 0))
    return pl.pallas_call(
        body, name=name, grid=(T // tm,),
        in_specs=[row, row, row, pl.BlockSpec((1, D), lambda i: (0, 0)), _mod_spec(tm, seq, D)],
        out_specs=(row, row, _mod_spec(tm, seq, D), pl.BlockSpec((SUBLANE, D), lambda i: (0, 0))),
        out_shape=(jax.ShapeDtypeStruct((T, D), F32), jax.ShapeDtypeStruct((T, D), BF16),
                   jax.ShapeDtypeStruct((B, SUBLANE, D), F32), jax.ShapeDtypeStruct((SUBLANE, D), F32)),
        compiler_params=_cparams(("arbitrary",)),
    )(x, dh, dres, g, mod)


def _gate_bwd(dx, prod, mod, gate_row, seq, name, deps=()):
    T, D = dx.shape
    B = mod.shape[0]
    tm = _row_tile(seq, 512)
    per_seq = seq // tm

    def body(dx_ref, p_ref, mod_ref, *rest):
        dp_ref, pb_ref = rest[-2:]
        i = pl.program_id(0)
        dxv = dx_ref[...]
        dp_ref[...] = (dxv * mod_ref[0, gate_row:gate_row + 1, :]).astype(BF16)

        @pl.when(i % per_seq == 0)
        def _():
            pb_ref[...] = jnp.zeros_like(pb_ref)

        pb_ref[0, 0:1, :] += jnp.sum(dxv * p_ref[...].astype(F32), axis=0, keepdims=True)

    row = pl.BlockSpec((tm, D), lambda i: (i, 0))
    return pl.pallas_call(
        body, name=name, grid=(T // tm,),
        in_specs=[row, row, _mod_spec(tm, seq, D)] + [ANY] * len(deps),
        out_specs=(row, _mod_spec(tm, seq, D)),
        out_shape=(jax.ShapeDtypeStruct((T, D), BF16), jax.ShapeDtypeStruct((B, SUBLANE, D), F32)),
        compiler_params=_cparams(("arbitrary",)),
    )(dx, prod, mod, *deps)


def _swiglu_fwd(ab, name):
    T, F2 = ab.shape
    Fh = F2 // 2
    tm = _pick(T, 256, HALO)

    def body(ab_ref, act_ref):
        a = ab_ref[:, :Fh].astype(F32)
        b = ab_ref[:, Fh:].astype(F32)
        act_ref[...] = (a * _sigmoid(a) * b).astype(BF16)

    return pl.pallas_call(
        body, name=name, grid=(T // tm,),
        in_specs=[pl.BlockSpec((tm, F2), lambda i: (i, 0))],
        out_specs=pl.BlockSpec((tm, Fh), lambda i: (i, 0)),
        out_shape=jax.ShapeDtypeStruct((T, Fh), BF16),
        compiler_params=_cparams(("parallel",)),
    )(ab)


def _swiglu_bwd(dact, ab, name):
    T, F2 = ab.shape
    Fh = F2 // 2
    tm = _pick(T, 256, HALO)

    def body(d_ref, ab_ref, dab_ref):
        d = d_ref[...].astype(F32)
        a = ab_ref[:, :Fh].astype(F32)
        b = ab_ref[:, Fh:].astype(F32)
        sg = _sigmoid(a)
        dab_ref[:, :Fh] = (d * b * sg * (1.0 + a * (1.0 - sg))).astype(BF16)
        dab_ref[:, Fh:] = (d * a * sg).astype(BF16)

    return pl.pallas_call(
        body, name=name, grid=(T // tm,),
        in_specs=[pl.BlockSpec((tm, Fh), lambda i: (i, 0)), pl.BlockSpec((tm, F2), lambda i: (i, 0))],
        out_specs=pl.BlockSpec((tm, F2), lambda i: (i, 0)),
        out_shape=jax.ShapeDtypeStruct((T, F2), BF16),
        compiler_params=_cparams(("parallel",)),
    )(dact, ab)


def _shift_down(v, d):
    return pltpu.roll(v, d, 0)


def _shift_up(v, d):
    return pltpu.roll(v, v.shape[0] - d, 0)


def _tril_mask():
    r = lax.broadcasted_iota(jnp.int32, (CHUNK, CHUNK), 0)
    c = lax.broadcasted_iota(jnp.int32, (CHUNK, CHUNK), 1)
    return c <= r


def _pool_counts(i, tm, seq, rows, first_row):
    r = lax.broadcasted_iota(jnp.int32, (rows, 1), 0) + (i * tm + first_row)
    pos1 = (r % seq + 1).astype(F32)
    return [jnp.minimum(pos1, float(2 << g)) for g in range(N_GROUP)]


def _mixer_forward_values(zt, hxb, hch, i, tm, seq, ln_g, ln_b, ws_ref, bs_ref, pw_ref, pscale, cw_ref):
    u = zt[:, 0 * BR_W:1 * BR_W]
    v = zt[:, 1 * BR_W:2 * BR_W]
    xb = zt[:, 2 * BR_W:3 * BR_W]
    bg = zt[:, 3 * BR_W:4 * BR_W]
    cg = zt[:, 4 * BR_W:5 * BR_W]
    hc = zt[:, 5 * BR_W:6 * BR_W]
    out = {}

    ug, tu = _gelu(u)
    vg, tv = _gelu(v)
    mu = jnp.mean(vg, axis=-1, keepdims=True)
    vc = vg - mu
    rstd = lax.rsqrt(jnp.mean(vc * vc, axis=-1, keepdims=True) + EPS)
    vhat = vc * rstd
    vn = (vhat * ln_g + ln_b).astype(BF16)
    mask = _tril_mask()
    wt = [jnp.where(mask, ws_ref[h], 0.0).astype(BF16) for h in range(HEADS)]
    rows = []
    for n in range(tm // CHUNK):
        blocks = []
        for h in range(HEADS):
            blk = vn[n * CHUNK:(n + 1) * CHUNK, h * HEAD_DIM:(h + 1) * HEAD_DIM]
            sb = jnp.dot(wt[h], blk, preferred_element_type=F32) + bs_ref[:, h:h + 1]
            blocks.append(sb)
        rows.append(jnp.concatenate(blocks, axis=1))
    s = jnp.concatenate(rows, axis=0) if len(rows) > 1 else rows[0]
    out.update(u=u, v=v, ug=ug, tu=tu, tv=tv, rstd=rstd, vhat=vhat, vn=vn, wt=wt, s=s, a_out=ug * s)

    ext = jnp.concatenate([hxb, xb], axis=0)
    cnt = _pool_counts(i, tm, seq, tm, 0)
    p, qs = [], []
    for g in range(N_GROUP):
        e = ext[:, g * GROUP_DIM:(g + 1) * GROUP_DIM]
        acc = e
        for d in (1, 2, 4, 8)[:g + 1]:
            acc = acc + _shift_down(acc, d)
        pg = acc[HALO:, :] / cnt[g] - xb[:, g * GROUP_DIM:(g + 1) * GROUP_DIM]
        p.append(pg.astype(BF16))
        qs.append(jnp.dot(p[g], pw_ref[g].astype(BF16), preferred_element_type=F32))
    q = jnp.concatenate(qs, axis=1)
    out.update(p=p, q=q, b_out=q * pscale)

    zc = cg * hc
    zce = jnp.concatenate([hch[:, :BR_W] * hch[:, BR_W:], zc], axis=0)
    z1 = _shift_down(zce, 1)[HALO:, :]
    z2 = _shift_down(zce, 2)[HALO:, :]
    y = cw_ref[0:1, :] * z2 + cw_ref[1:2, :] * z1 + cw_ref[2:3, :] * zc
    out.update(bg=bg, cg=cg, hc=hc, zc=zc, z1=z1, z2=z2, y=y, c_out=bg * y)
    return out


def _mixer_specs(tm, T):
    nb = T // HALO
    per = tm // HALO
    prev = lambda i: jnp.maximum(i * per - 1, 0)
    nxt = lambda i: jnp.minimum((i + 1) * per, nb - 1)
    return prev, nxt


def _mixer_fwd(z, prm, seq, name):
    T = z.shape[0]
    tm = _row_tile(seq, 256)
    per_seq = seq // tm
    prev, _ = _mixer_specs(tm, T)

    def body(z_ref, hxb_ref, hch_ref, vec_ref, cw_ref, ws_ref, bs_ref, pw_ref, cat_ref):
        i = pl.program_id(0)
        keep = jnp.where(i % per_seq == 0, 0.0, 1.0)
        zt = z_ref[...].astype(F32)
        hxb = hxb_ref[...].astype(F32) * keep
        hch = hch_ref[...].astype(F32) * keep
        o = _mixer_forward_values(zt, hxb, hch, i, tm, seq, vec_ref[0:1, :], vec_ref[1:2, :],
                                  ws_ref, bs_ref, pw_ref, vec_ref[2:3, :], cw_ref)
        cat_ref[:, 0 * BR_W:1 * BR_W] = o["a_out"].astype(BF16)
        cat_ref[:, 1 * BR_W:2 * BR_W] = o["b_out"].astype(BF16)
        cat_ref[:, 2 * BR_W:3 * BR_W] = o["c_out"].astype(BF16)

    full = lambda shape: pl.BlockSpec(shape, lambda i: (0,) * len(shape))
    return pl.pallas_call(
        body, name=name, grid=(T // tm,),
        in_specs=[pl.BlockSpec((tm, N_SPLIT), lambda i: (i, 0)),
                  pl.BlockSpec((HALO, BR_W), lambda i: (prev(i), 2)),
                  pl.BlockSpec((HALO, 2 * BR_W), lambda i: (prev(i), 2)),
                  full((SUBLANE, BR_W)), full((SUBLANE, BR_W)), full((HEADS, CHUNK, CHUNK)),
                  full((CHUNK, LANE)), full((N_GROUP, GROUP_DIM, GROUP_DIM))],
        out_specs=pl.BlockSpec((tm, 3 * BR_W), lambda i: (i, 0)),
        out_shape=jax.ShapeDtypeStruct((T, 3 * BR_W), BF16),
        compiler_params=_cparams(("parallel",)),
    )(z, z, z, prm["vec"], prm["conv"], prm["w_s"], prm["b_s"], prm["pool_w"])


def _mixer_bwd(z, dcat, dgl, prm, seq, name):
    T, IN = z.shape
    GL = IN - N_SPLIT
    tm = _row_tile(seq, 256)
    per_seq = seq // tm
    prev, nxt = _mixer_specs(tm, T)
    nrow = tm + HALO

    def body(z_ref, hxb_ref, hch_ref, nbg_ref, dcat_ref, ndb_ref, ndc_ref, dgl_ref,
             vec_ref, cw_ref, ws_ref, bs_ref, pw_ref,
             dz_ref, pv_ref, dws_ref, dbs_ref, dpw_ref):
        i = pl.program_id(0)
        keep_prev = jnp.where(i % per_seq == 0, 0.0, 1.0)
        keep_next = jnp.where(i % per_seq == per_seq - 1, 0.0, 1.0)
        zt = z_ref[...].astype(F32)
        hxb = hxb_ref[...].astype(F32) * keep_prev
        hch = hch_ref[...].astype(F32) * keep_prev
        ln_g = vec_ref[0:1, :]
        pscale = vec_ref[2:3, :]
        o = _mixer_forward_values(zt, hxb, hch, i, tm, seq, ln_g, vec_ref[1:2, :],
                                  ws_ref, bs_ref, pw_ref, pscale, cw_ref)
        dcv = dcat_ref[...].astype(F32)
        da = dcv[:, 0 * BR_W:1 * BR_W]
        db = dcv[:, 1 * BR_W:2 * BR_W]
        dc = dcv[:, 2 * BR_W:3 * BR_W]
        mask = _tril_mask()

        @pl.when(i == 0)
        def _():
            pv_ref[...] = jnp.zeros_like(pv_ref)
            dws_ref[...] = jnp.zeros_like(dws_ref)
            dbs_ref[...] = jnp.zeros_like(dbs_ref)
            dpw_ref[...] = jnp.zeros_like(dpw_ref)

        d_ug = da * o["s"]
        ds = da * o["ug"]
        ds_b = ds.astype(BF16)
        vn = o["vn"]
        dvn_rows = []
        dws = [jnp.zeros((CHUNK, CHUNK), F32) for _ in range(HEADS)]
        dsum = jnp.zeros((CHUNK, BR_W), F32)
        for n in range(tm // CHUNK):
            blocks = []
            rs = slice(n * CHUNK, (n + 1) * CHUNK)
            dsum = dsum + ds[rs, :]
            for h in range(HEADS):
                cs = slice(h * HEAD_DIM, (h + 1) * HEAD_DIM)
                dsb = ds_b[rs, cs]
                blocks.append(lax.dot_general(o["wt"][h], dsb, (((0,), (0,)), ((), ())),
                                              preferred_element_type=F32))
                dws[h] = dws[h] + lax.dot_general(dsb, vn[rs, cs], (((1,), (1,)), ((), ())),
                                                  preferred_element_type=F32)
            dvn_rows.append(jnp.concatenate(blocks, axis=1))
        dvn = jnp.concatenate(dvn_rows, axis=0) if len(dvn_rows) > 1 else dvn_rows[0]
        lane = lax.broadcasted_iota(jnp.int32, (CHUNK, LANE), 1)
        dbs_t = jnp.zeros((CHUNK, LANE), F32)
        for h in range(HEADS):
            dws_ref[h] += jnp.where(mask, dws[h], 0.0)
            rsum = jnp.sum(dsum[:, h * HEAD_DIM:(h + 1) * HEAD_DIM], axis=1, keepdims=True)
            dbs_t = dbs_t + jnp.where(lane == h, rsum, 0.0)
        dbs_ref[...] += dbs_t
        vhat = o["vhat"]
        pv_ref[0:1, :] += jnp.sum(dvn * vhat, axis=0, keepdims=True)
        pv_ref[1:2, :] += jnp.sum(dvn, axis=0, keepdims=True)
        dvhat = dvn * ln_g
        dvg = o["rstd"] * (dvhat - jnp.mean(dvhat, axis=-1, keepdims=True)
                           - vhat * jnp.mean(dvhat * vhat, axis=-1, keepdims=True))
        du = d_ug * _gelu_grad(o["u"], o["tu"])
        dv = dvg * _gelu_grad(o["v"], o["tv"])

        pv_ref[2:3, :] += jnp.sum(db * o["q"], axis=0, keepdims=True)
        dq = (db * pscale).astype(BF16)
        dqn = (ndb_ref[...].astype(F32) * pscale * keep_next).astype(BF16)
        cnt = _pool_counts(i, tm, seq, nrow, 0)
        dxb = []
        for g in range(N_GROUP):
            cs = slice(g * GROUP_DIM, (g + 1) * GROUP_DIM)
            pwg = pw_ref[g].astype(BF16)
            dpw_ref[g] += lax.dot_general(o["p"][g], dq[:, cs], (((0,), (0,)), ((), ())),
                                          preferred_element_type=F32)
            dp = lax.dot_general(dq[:, cs], pwg, (((1,), (1,)), ((), ())), preferred_element_type=F32)
            dpn = lax.dot_general(dqn[:, cs], pwg, (((1,), (1,)), ((), ())), preferred_element_type=F32)
            acc = jnp.concatenate([dp, dpn], axis=0) / cnt[g]
            for d in (1, 2, 4, 8)[:g + 1]:
                acc = acc + _shift_up(acc, d)
            dxb.append(acc[:tm, :] - dp)
        dxb = jnp.concatenate(dxb, axis=1)

        dbg = dc * o["y"]
        dy = dc * o["bg"]
        pv_ref[3:4, :] += jnp.sum(dy * o["z2"], axis=0, keepdims=True)
        pv_ref[4:5, :] += jnp.sum(dy * o["z1"], axis=0, keepdims=True)
        pv_ref[5:6, :] += jnp.sum(dy * o["zc"], axis=0, keepdims=True)
        dyn = ndc_ref[...].astype(F32) * nbg_ref[...].astype(F32) * keep_next
        dye = jnp.concatenate([dy, dyn], axis=0)
        dzc = (cw_ref[2:3, :] * dy + cw_ref[1:2, :] * _shift_up(dye, 1)[:tm, :]
               + cw_ref[0:1, :] * _shift_up(dye, 2)[:tm, :])
        dcg = dzc * o["hc"]
        dhc = dzc * o["cg"]

        dz_ref[:, 0 * BR_W:1 * BR_W] = du.astype(BF16)
        dz_ref[:, 1 * BR_W:2 * BR_W] = dv.astype(BF16)
        dz_ref[:, 2 * BR_W:3 * BR_W] = dxb.astype(BF16)
        dz_ref[:, 3 * BR_W:4 * BR_W] = dbg.astype(BF16)
        dz_ref[:, 4 * BR_W:5 * BR_W] = dcg.astype(BF16)
        dz_ref[:, 5 * BR_W:6 * BR_W] = dhc.astype(BF16)
        dz_ref[:, N_SPLIT:] = dgl_ref[...]

    full = lambda shape: pl.BlockSpec(shape, lambda i: (0,) * len(shape))
    return pl.pallas_call(
        body, name=name, grid=(T // tm,),
        in_specs=[pl.BlockSpec((tm, N_SPLIT), lambda i: (i, 0)),
                  pl.BlockSpec((HALO, BR_W), lambda i: (prev(i), 2)),
                  pl.BlockSpec((HALO, 2 * BR_W), lambda i: (prev(i), 2)),
                  pl.BlockSpec((HALO, BR_W), lambda i: (nxt(i), 3)),
                  pl.BlockSpec((tm, 3 * BR_W), lambda i: (i, 0)),
                  pl.BlockSpec((HALO, BR_W), lambda i: (nxt(i), 1)),
                  pl.BlockSpec((HALO, BR_W), lambda i: (nxt(i), 2)),
                  pl.BlockSpec((tm, GL), lambda i: (i, 0)),
                  full((SUBLANE, BR_W)), full((SUBLANE, BR_W)), full((HEADS, CHUNK, CHUNK)),
                  full((CHUNK, LANE)), full((N_GROUP, GROUP_DIM, GROUP_DIM))],
        out_specs=(pl.BlockSpec((tm, IN), lambda i: (i, 0)),
                   full((SUBLANE, BR_W)), full((HEADS, CHUNK, CHUNK)), full((CHUNK, LANE)),
                   full((N_GROUP, GROUP_DIM, GROUP_DIM))),
        out_shape=(jax.ShapeDtypeStruct((T, IN), BF16),
                   jax.ShapeDtypeStruct((SUBLANE, BR_W), F32),
                   jax.ShapeDtypeStruct((HEADS, CHUNK, CHUNK), F32),
                   jax.ShapeDtypeStruct((CHUNK, LANE), F32),
                   jax.ShapeDtypeStruct((N_GROUP, GROUP_DIM, GROUP_DIM), F32)),
        compiler_params=_cparams(("arbitrary",)),
    )(z, z, z, z, dcat, dcat, dcat, dgl, prm["vec"], prm["conv"], prm["w_s"], prm["b_s"], prm["pool_w"])


def _proj_fwd(cat, z, w_pa, w_pb, w_pc, D, name):
    T, IN = z.shape
    GL = 3 * D
    assert N_SPLIT % GL == 0
    glb = N_SPLIT // GL
    tm = _pick(T, 256, HALO)

    def body(cat_ref, gl_ref, wa_ref, wb_ref, wc_ref, m_ref):
        acc = jnp.zeros((tm, D), F32)
        for k, w_ref in enumerate((wa_ref, wb_ref, wc_ref)):
            y = jnp.dot(cat_ref[:, k * BR_W:(k + 1) * BR_W], w_ref[...], preferred_element_type=F32)
            acc = acc + _sigmoid(gl_ref[:, k * D:(k + 1) * D].astype(F32)) * y
        m_ref[...] = acc.astype(BF16)

    wspec = pl.BlockSpec((BR_W, D), lambda i: (0, 0))
    return pl.pallas_call(
        body, name=name, grid=(T // tm,),
        in_specs=[pl.BlockSpec((tm, 3 * BR_W), lambda i: (i, 0)),
                  pl.BlockSpec((tm, GL), lambda i: (i, glb)), wspec, wspec, wspec],
        out_specs=pl.BlockSpec((tm, D), lambda i: (i, 0)),
        out_shape=jax.ShapeDtypeStruct((T, D), BF16),
        compiler_params=_cparams(("parallel",)),
    )(cat, z, w_pa, w_pb, w_pc)


def _proj_bwd(dmerged, cat, z, w_pa, w_pb, w_pc, D, name):
    T, IN = z.shape
    GL = 3 * D
    glb = N_SPLIT // GL
    tm = _pick(T, 256, HALO)

    def body(dm_ref, cat_ref, gl_ref, wa_ref, wb_ref, wc_ref, dy_ref, dgl_ref, dcat_ref):
        dm = dm_ref[...].astype(F32)
        for k, w_ref in enumerate((wa_ref, wb_ref, wc_ref)):
            w = w_ref[...]
            y = jnp.dot(cat_ref[:, k * BR_W:(k + 1) * BR_W], w, preferred_element_type=F32)
            sg = _sigmoid(gl_ref[:, k * D:(k + 1) * D].astype(F32))
            dyk = (dm * sg).astype(BF16)
            dy_ref[:, k * D:(k + 1) * D] = dyk
            dgl_ref[:, k * D:(k + 1) * D] = (dm * y * sg * (1.0 - sg)).astype(BF16)
            dcat_ref[:, k * BR_W:(k + 1) * BR_W] = lax.dot_general(
                dyk, w, (((1,), (1,)), ((), ())), preferred_element_type=F32).astype(BF16)

    wspec = pl.BlockSpec((BR_W, D), lambda i: (0, 0))
    return pl.pallas_call(
        body, name=name, grid=(T // tm,),
        in_specs=[pl.BlockSpec((tm, D), lambda i: (i, 0)),
                  pl.BlockSpec((tm, 3 * BR_W), lambda i: (i, 0)),
                  pl.BlockSpec((tm, GL), lambda i: (i, glb)), wspec, wspec, wspec],
        out_specs=(pl.BlockSpec((tm, GL), lambda i: (i, 0)), pl.BlockSpec((tm, GL), lambda i: (i, 0)),
                   pl.BlockSpec((tm, 3 * BR_W), lambda i: (i, 0))),
        out_shape=(jax.ShapeDtypeStruct((T, GL), BF16), jax.ShapeDtypeStruct((T, GL), BF16),
                   jax.ShapeDtypeStruct((T, 3 * BR_W), BF16)),
        compiler_params=_cparams(("parallel",)),
    )(dmerged, cat, z, w_pa, w_pb, w_pc)


def _loss_head(x, target, g, name):
    T, D = x.shape
    tm = _pick(T, 512, SUBLANE)

    def body(x_ref, t_ref, g_ref, dx_ref, part_ref):
        i = pl.program_id(0)
        xv = x_ref[...]
        gv = g_ref[...]
        rstd = lax.rsqrt(jnp.mean(xv * xv, axis=-1, keepdims=True) + EPS)
        xhat = xv * rstd
        err = xhat * gv - t_ref[...]
        dy = err * (1.0 / D)
        dxhat = dy * gv
        dx_ref[...] = rstd * (dxhat - xhat * jnp.mean(dxhat * xhat, axis=-1, keepdims=True))

        @pl.when(i == 0)
        def _():
            part_ref[...] = jnp.zeros_like(part_ref)

        part_ref[0:1, :] += jnp.sum(dy * xhat, axis=0, keepdims=True)
        part_ref[1:2, :] += jnp.zeros((1, D), F32) + (0.5 / D) * jnp.sum(err * err)

    row = pl.BlockSpec((tm, D), lambda i: (i, 0))
    return pl.pallas_call(
        body, name=name, grid=(T // tm,),
        in_specs=[row, row, pl.BlockSpec((1, D), lambda i: (0, 0))],
        out_specs=(row, pl.BlockSpec((SUBLANE, D), lambda i: (0, 0))),
        out_shape=(jax.ShapeDtypeStruct((T, D), F32), jax.ShapeDtypeStruct((SUBLANE, D), F32)),
        compiler_params=_cparams(("arbitrary",)),
    )(x, target, g)


def _mod_fwd(c_all, w_mod, b_cols, name):
    L, D, N4 = w_mod.shape
    Bg = c_all.shape[0]
    tn = _pick(N4, 768)

    def body(c_ref, w_ref, b_ref, o_ref):
        cv = c_ref[...]
        ca = (cv * _sigmoid(cv)).astype(BF16)
        o_ref[...] = jnp.dot(ca, w_ref[...].astype(BF16), preferred_element_type=F32) + b_ref[...]

    return pl.pallas_call(
        body, name=name, grid=(L, N4 // tn),
        in_specs=[pl.BlockSpec((Bg, D), lambda l, j: (0, 0)),
                  pl.BlockSpec((None, D, tn), lambda l, j: (l, 0, j)),
                  pl.BlockSpec((None, 1, tn), lambda l, j: (l, 0, j))],
        out_specs=pl.BlockSpec((None, Bg, tn), lambda l, j: (l, 0, j)),
        out_shape=jax.ShapeDtypeStruct((L, Bg, N4), F32),
        compiler_params=_cparams(("parallel", "parallel")),
    )(c_all, w_mod, b_cols)


def _mod_wgrad(c_all, dmod_cols, name):
    L, Bg, N4 = dmod_cols.shape
    D = c_all.shape[1]
    tn = _pick(N4, 768)

    def body(c_ref, d_ref, o_ref):
        cv = c_ref[...]
        ca = (cv * _sigmoid(cv)).astype(BF16)
        o_ref[...] = lax.dot_general(ca, d_ref[...].astype(BF16), (((0,), (0,)), ((), ())),
                                     preferred_element_type=F32)

    return pl.pallas_call(
        body, name=name, grid=(L, N4 // tn),
        in_specs=[pl.BlockSpec((Bg, D), lambda l, j: (0, 0)),
                  pl.BlockSpec((None, Bg, tn), lambda l, j: (l, 0, j))],
        out_specs=pl.BlockSpec((None, D, tn), lambda l, j: (l, 0, j)),
        out_shape=jax.ShapeDtypeStruct((L, D, N4), F32),
        compiler_params=_cparams(("parallel", "parallel")),
    )(c_all, dmod_cols)


def _rows_tile(R, C):
    return _pick(R, max(SUBLANE, (256 * 1024) // C), SUBLANE)


def _cast_into_full(w, layer, col_sharded, chip, name):
    L, R, C = w.shape
    K, N = (R, C * N_CHIP) if col_sharded else (R * N_CHIP, C)
    tr = _pick(R, max(HALO, (512 * 1024) // C), HALO)
    nb = R // tr

    def body(q_ref, w_ref, o_ref):
        o_ref[...] = w_ref[...].astype(BF16)

    out_idx = (lambda i, q: (i, q[0])) if col_sharded else (lambda i, q: (q[0] * nb + i, 0))
    grid_spec = pltpu.PrefetchScalarGridSpec(
        num_scalar_prefetch=1, grid=(nb,),
        in_specs=[pl.BlockSpec((None, tr, C), lambda i, q: (layer, i, 0))],
        out_specs=pl.BlockSpec((tr, C), out_idx))
    return pl.pallas_call(
        body, name=name, grid_spec=grid_spec, out_shape=jax.ShapeDtypeStruct((K, N), BF16),
        compiler_params=_cparams(("arbitrary",)),
    )(chip.reshape(1).astype(jnp.int32), w)


def _sum_into(stack, grad, slots, layer, col_sharded, chip, name):
    _, R, C = slots.shape
    tr = _rows_tile(R, C)
    nb = R // tr

    def body(q_ref, stack_ref, g_ref, s_ref, o_ref):
        o_ref[...] = ((g_ref[...].astype(F32) + s_ref[0].astype(F32)) + s_ref[1].astype(F32)) + s_ref[2].astype(F32)

    g_idx = (lambda i, q: (i, q[0])) if col_sharded else (lambda i, q: (q[0] * nb + i, 0))
    grid_spec = pltpu.PrefetchScalarGridSpec(
        num_scalar_prefetch=1, grid=(nb,),
        in_specs=[ANY, pl.BlockSpec((tr, C), g_idx), pl.BlockSpec((3, tr, C), lambda i, q: (0, i, 0))],
        out_specs=pl.BlockSpec((tr, C), lambda i, q: (layer * nb + i, 0)))
    return pl.pallas_call(
        body, name=name, grid_spec=grid_spec, out_shape=jax.ShapeDtypeStruct(stack.shape, F32),
        input_output_aliases={1: 0}, compiler_params=_cparams(("arbitrary",)),
    )(chip.reshape(1).astype(jnp.int32), stack, grad, slots)


def _sum_devices(parts, name):
    n, R, C = parts.shape
    tr = _rows_tile(R, C)

    def body(s_ref, o_ref):
        acc = s_ref[0]
        for d in range(1, n):
            acc = acc + s_ref[d]
        o_ref[...] = acc

    return pl.pallas_call(
        body, name=name, grid=(R // tr,),
        in_specs=[pl.BlockSpec((n, tr, C), lambda i: (0, i, 0))],
        out_specs=pl.BlockSpec((tr, C), lambda i: (i, 0)),
        out_shape=jax.ShapeDtypeStruct((R, C), F32), compiler_params=_cparams(("parallel",)),
    )(parts)


def _adamw(w, m, v, grads, name):
    R, C = w.shape
    tr = _rows_tile(R, C)
    bc1 = 1.0 - ADAM_B1 ** ADAM_STEP
    bc2 = 1.0 - ADAM_B2 ** ADAM_STEP
    ng = len(grads)

    def body(*refs):
        w_ref, m_ref, v_ref = refs[:3]
        g_refs = refs[3:3 + ng]
        g_out, d_out, m_out, v_out = refs[3 + ng:]
        g = g_refs[0][...]
        for r in g_refs[1:]:
            g = g + r[...]
        mn = ADAM_B1 * m_ref[...] + (1.0 - ADAM_B1) * g
        vn = ADAM_B2 * v_ref[...] + (1.0 - ADAM_B2) * (g * g)
        m_hat = mn / bc1
        v_hat = vn / bc2
        g_out[...] = g
        d_out[...] = -ADAM_LR * (m_hat / (jnp.sqrt(v_hat) + ADAM_EPS) + ADAM_WD * w_ref[...])
        m_out[...] = mn
        v_out[...] = vn

    spec = pl.BlockSpec((tr, C), lambda i: (i, 0))
    sds = jax.ShapeDtypeStruct((R, C), F32)
    return pl.pallas_call(
        body, name=name, grid=(R // tr,), in_specs=[spec] * (3 + ng), out_specs=(spec,) * 4,
        out_shape=(sds,) * 4, compiler_params=_cparams(("parallel",)),
    )(w, m, v, *grads)


def _place():
    x, y, c = lax.axis_index("x"), lax.axis_index("y"), lax.axis_index("c")
    chips = [(1 - x, y), (x, 1 - y), (1 - x, 1 - y)]
    return x, y, c, chips


ANY = pl.BlockSpec(memory_space=pl.ANY)


def _allgather8(v, name):
    m_per, n = v.shape

    def body(x_ref, out_ref, send_sems, recv_sems, local_sem):
        x, y, c, chips = _place()
        me, sibling = (x, y, c), (x, y, 1 - c)

        def rows(px, py, pc):
            return out_ref.at[pl.ds((4 * px + 2 * py + pc) * m_per, m_per), :]

        def copy(k, block, to, src=None):
            return pltpu.make_async_remote_copy(
                src_ref=rows(*block) if src is None else src, dst_ref=rows(*block),
                send_sem=send_sems.at[k], recv_sem=recv_sems.at[k], device_id=to, device_id_type=MESH)

        mine = pltpu.make_async_copy(x_ref, rows(*me), local_sem)
        mine.start()
        first = [copy(0, me, sibling, src=x_ref)]
        first += [copy(1 + j, me, (*chip, c), src=x_ref) for j, chip in enumerate(chips)]
        for cp in first:
            cp.start()
        passed = [copy(4 + j, (*chip, c), sibling) for j, chip in enumerate(chips)]
        for j, chip in enumerate(chips):
            copy(1 + j, (*chip, c), me).wait_recv()
            passed[j].start()
        copy(0, sibling, me).wait_recv()
        for j, chip in enumerate(chips):
            copy(4 + j, (*chip, 1 - c), me).wait_recv()
        for cp in first + passed:
            cp.wait_send()
        mine.wait()

    return pl.pallas_call(
        body, name=name, out_shape=jax.ShapeDtypeStruct((N_DEV * m_per, n), v.dtype),
        in_specs=[ANY], out_specs=ANY,
        scratch_shapes=[pltpu.SemaphoreType.DMA((7,)), pltpu.SemaphoreType.DMA((7,)), pltpu.SemaphoreType.DMA],
    )(v)


def _window(ref, col_sharded, q, lead):
    full = (slice(None),) * lead
    if col_sharded:
        width = ref.shape[-1] // N_CHIP
        return ref.at[full + (slice(None), pl.ds(pl.multiple_of(q * width, LANE), width))]
    height = ref.shape[-2] // N_CHIP
    return ref.at[full + (pl.ds(pl.multiple_of(q * height, HALO), height), slice(None))]


HBM = pl.BlockSpec(memory_space=pltpu.HBM)
SEM = pl.BlockSpec(memory_space=pltpu.SEMAPHORE)
EFFECT = pltpu.SideEffectType.DATAFLOW_SIDE_EFFECTING


def _in_hbm(v):
    return pltpu.with_memory_space_constraint(v, pltpu.HBM)


def _gather_start(bufs, col_sharded, name):
    n = len(bufs)

    def body(*refs):
        ins = refs[:n]
        send_sems, recv_sems = refs[n], refs[n + 1]
        token = refs[-1]
        x, y, c, chips = _place()
        q = 2 * x + y
        for w in range(n):
            for k, chip in enumerate(chips):
                pltpu.make_async_remote_copy(
                    src_ref=_window(ins[w], col_sharded[w], q, 0), dst_ref=_window(ins[w], col_sharded[w], q, 0),
                    send_sem=send_sems.at[3 * w + k], recv_sem=recv_sems.at[3 * w + k],
                    device_id=(*chip, c), device_id_type=MESH).start()
        token[...] = jnp.zeros_like(token)

    out = pl.pallas_call(
        body, name=name,
        out_shape=(pltpu.SemaphoreType.DMA((3 * n,)), pltpu.SemaphoreType.DMA((3 * n,)),
                   *[pltpu.HBM(b.shape, b.dtype) for b in bufs], jax.ShapeDtypeStruct((SUBLANE, LANE), F32)),
        in_specs=(HBM,) * n, out_specs=(SEM, SEM) + (HBM,) * n + (pl.BlockSpec(memory_space=pltpu.VMEM),),
        input_output_aliases={w: 2 + w for w in range(n)},
        compiler_params=pltpu.CompilerParams(has_side_effects=EFFECT),
    )(*[_in_hbm(b) for b in bufs])
    return out[0], out[1], list(out[2:2 + n]), out[-1]


def _gather_wait(send_sems, recv_sems, bufs, col_sharded, after, name):
    n = len(bufs)

    def body(*refs):
        ins = refs[:n]
        send_sems, recv_sems = refs[n], refs[n + 1]
        x, y, c, chips = _place()
        q = 2 * x + y
        for w in range(n):
            for k, (cx, cy) in enumerate(chips):
                cp = pltpu.make_async_remote_copy(
                    src_ref=_window(ins[w], col_sharded[w], q, 0),
                    dst_ref=_window(ins[w], col_sharded[w], 2 * cx + cy, 0),
                    send_sem=send_sems.at[3 * w + k], recv_sem=recv_sems.at[3 * w + k],
                    device_id=(cx, cy, c), device_id_type=MESH)
                cp.wait_send()
                cp.wait_recv()

    out = pl.pallas_call(
        body, name=name, out_shape=tuple(pltpu.HBM(b.shape, b.dtype) for b in bufs),
        in_specs=(HBM,) * n + (SEM, SEM) + (ANY,) * len(after), out_specs=(HBM,) * n,
        input_output_aliases={w: w for w in range(n)},
        compiler_params=pltpu.CompilerParams(has_side_effects=EFFECT),
    )(*bufs, send_sems, recv_sems, *after)
    return list(out)


def _scatter_start(grads, col_sharded, name):
    n = len(grads)
    lands = []
    for g, cs in zip(grads, col_sharded):
        K, N = g.shape
        lands.append(lax.empty((3, K, N // N_CHIP) if cs else (3, K // N_CHIP, N), BF16))

    def body(*refs):
        ins, slots = refs[:n], refs[n:2 * n]
        send_sems, recv_sems = refs[2 * n], refs[2 * n + 1]
        token = refs[-1]
        x, y, c, chips = _place()
        for w in range(n):
            for k, (cx, cy) in enumerate(chips):
                pltpu.make_async_remote_copy(
                    src_ref=_window(ins[w], col_sharded[w], 2 * cx + cy, 0), dst_ref=slots[w].at[k],
                    send_sem=send_sems.at[3 * w + k], recv_sem=recv_sems.at[3 * w + k],
                    device_id=(cx, cy, c), device_id_type=MESH).start()
        token[...] = jnp.zeros_like(token)

    out = pl.pallas_call(
        body, name=name,
        out_shape=(pltpu.SemaphoreType.DMA((3 * n,)), pltpu.SemaphoreType.DMA((3 * n,)),
                   *[pltpu.HBM(b.shape, b.dtype) for b in grads], *[pltpu.HBM(b.shape, b.dtype) for b in lands],
                   jax.ShapeDtypeStruct((SUBLANE, LANE), F32)),
        in_specs=(HBM,) * (2 * n),
        out_specs=(SEM, SEM) + (HBM,) * (2 * n) + (pl.BlockSpec(memory_space=pltpu.VMEM),),
        input_output_aliases={w: 2 + w for w in range(2 * n)},
        compiler_params=pltpu.CompilerParams(has_side_effects=EFFECT),
    )(*[_in_hbm(b) for b in grads], *[_in_hbm(b) for b in lands])
    return out[0], out[1], list(out[2:2 + n]), list(out[2 + n:2 + 2 * n]), out[-1]


def _scatter_wait(send_sems, recv_sems, grads, lands, col_sharded, after, name):
    n = len(grads)

    def body(*refs):
        ins, slots = refs[:n], refs[n:2 * n]
        send_sems, recv_sems = refs[2 * n], refs[2 * n + 1]
        x, y, c, chips = _place()
        for w in range(n):
            for k, (cx, cy) in enumerate(chips):
                cp = pltpu.make_async_remote_copy(
                    src_ref=_window(ins[w], col_sharded[w], 2 * cx + cy, 0), dst_ref=slots[w].at[k],
                    send_sem=send_sems.at[3 * w + k], recv_sem=recv_sems.at[3 * w + k],
                    device_id=(cx, cy, c), device_id_type=MESH)
                cp.wait_send()
                cp.wait_recv()

    out = pl.pallas_call(
        body, name=name, out_shape=tuple(pltpu.HBM(b.shape, b.dtype) for b in list(grads) + list(lands)),
        in_specs=(HBM,) * (2 * n) + (SEM, SEM) + (ANY,) * len(after), out_specs=(HBM,) * (2 * n),
        input_output_aliases={w: w for w in range(2 * n)},
        compiler_params=pltpu.CompilerParams(has_side_effects=EFFECT),
    )(*grads, *lands, send_sems, recv_sems, *after)
    return list(out[:n]), list(out[n:])


def _swap_with_sibling(arrays, name):
    n = len(arrays)

    def body(*refs):
        ins, outs = refs[:n], refs[n:2 * n]
        send_sems, recv_sems = refs[2 * n:]
        x, y, c, _ = _place()
        copies = []
        for w in range(n):
            cp = pltpu.make_async_remote_copy(
                src_ref=ins[w], dst_ref=outs[w], send_sem=send_sems.at[w], recv_sem=recv_sems.at[w],
                device_id=(x, y, 1 - c), device_id_type=MESH)
            cp.start()
            copies.append(cp)
        for cp in copies:
            cp.wait()

    return pl.pallas_call(
        body, name=name, out_shape=tuple(jax.ShapeDtypeStruct(a.shape, a.dtype) for a in arrays),
        in_specs=[ANY] * n, out_specs=(ANY,) * n,
        scratch_shapes=[pltpu.SemaphoreType.DMA((n,)), pltpu.SemaphoreType.DMA((n,))],
    )(*arrays)


BIG = ("w_in", "w_pa", "w_pb", "w_pc", "w_o", "w_13", "w_2")
BIG_COL_SHARDED = (True, True, True, True, False, True, False)
SMALL = ("b_mod", "g_mix", "gm_ln_g", "gm_ln_b", "gm_w_s", "gm_b_s", "pool_w", "pool_scale", "conv_w",
         "g_ffn", "g_final")
WEIGHTS = ("w_mod", "b_mod", "g_mix", "w_in", "gm_ln_g", "gm_ln_b", "gm_w_s", "gm_b_s", "w_pa", "pool_w",
           "pool_scale", "w_pb", "conv_w", "w_pc", "w_o", "g_ffn", "w_13", "w_2", "g_final")


def _pack(arrays, width):
    flat = jnp.concatenate([a.reshape(-1) for a in arrays])
    rows = -(-flat.shape[0] // width)
    rows = -(-rows // SUBLANE) * SUBLANE
    flat = jnp.pad(flat, (0, rows * width - flat.shape[0]))
    return flat.reshape(rows, width)


def _unpack(packed, shapes):
    flat = packed.reshape(-1)
    out, off = [], 0
    for s in shapes:
        size = 1
        for d in s:
            size *= d
        out.append(flat[off:off + size].reshape(s))
        off += size
    return out


def kernel(x, c, w_mod, b_mod, g_mix, w_in, gm_ln_g, gm_ln_b, gm_w_s, gm_b_s, w_pa, pool_w, pool_scale, w_pb, conv_w, w_pc, w_o, g_ffn, w_13, w_2, g_final, loss_target, m_w_mod, m_b_mod, m_g_mix, m_w_in, m_gm_ln_g, m_gm_ln_b, m_gm_w_s, m_gm_b_s, m_w_pa, m_pool_w, m_pool_scale, m_w_pb, m_conv_w, m_w_pc, m_w_o, m_g_ffn, m_w_13, m_w_2, m_g_final, v_w_mod, v_b_mod, v_g_mix, v_w_in, v_gm_ln_g, v_gm_ln_b, v_gm_w_s, v_gm_b_s, v_w_pa, v_pool_w, v_pool_scale, v_w_pb, v_conv_w, v_w_pc, v_w_o, v_g_ffn, v_w_13, v_w_2, v_g_final):
    W = dict(w_mod=w_mod, b_mod=b_mod, g_mix=g_mix, w_in=w_in, gm_ln_g=gm_ln_g, gm_ln_b=gm_ln_b, gm_w_s=gm_w_s,
             gm_b_s=gm_b_s, w_pa=w_pa, pool_w=pool_w, pool_scale=pool_scale, w_pb=w_pb, conv_w=conv_w, w_pc=w_pc,
             w_o=w_o, g_ffn=g_ffn, w_13=w_13, w_2=w_2, g_final=g_final)
    Mo = dict(w_mod=m_w_mod, b_mod=m_b_mod, g_mix=m_g_mix, w_in=m_w_in, gm_ln_g=m_gm_ln_g, gm_ln_b=m_gm_ln_b,
              gm_w_s=m_gm_w_s, gm_b_s=m_gm_b_s, w_pa=m_w_pa, pool_w=m_pool_w, pool_scale=m_pool_scale, w_pb=m_w_pb,
              conv_w=m_conv_w, w_pc=m_w_pc, w_o=m_w_o, g_ffn=m_g_ffn, w_13=m_w_13, w_2=m_w_2, g_final=m_g_final)
    Vo = dict(w_mod=v_w_mod, b_mod=v_b_mod, g_mix=v_g_mix, w_in=v_w_in, gm_ln_g=v_gm_ln_g, gm_ln_b=v_gm_ln_b,
              gm_w_s=v_gm_w_s, gm_b_s=v_gm_b_s, w_pa=v_w_pa, pool_w=v_pool_w, pool_scale=v_pool_scale, w_pb=v_w_pb,
              conv_w=v_conv_w, w_pc=v_w_pc, w_o=v_w_o, g_ffn=v_g_ffn, w_13=v_w_13, w_2=v_w_2, g_final=v_g_final)

    B, S, D = x.shape
    T = B * S
    L = w_in.shape[0]
    Bg = B * N_DEV
    N4 = w_mod.shape[2]
    CW = conv_w.shape[2]
    xi, yi, ci = lax.axis_index("x"), lax.axis_index("y"), lax.axis_index("c")
    chip = 2 * xi + yi
    dev = 2 * chip + ci

    head = _pack([c, conv_w], D)
    hrows = head.shape[0]
    got = _allgather8(head, "gather_c_conv").reshape(N_DEV, hrows * D)
    c_all = got[:, :B * D].reshape(Bg, D)
    conv_parts = got[:, B * D:B * D + L * 3 * CW].reshape(N_CHIP, 2, L, 3, CW)[:, 0]
    conv_full = jnp.transpose(conv_parts, (1, 2, 0, 3)).reshape(L, 3, N_CHIP * CW)

    b_cols = lax.dynamic_slice_in_dim(b_mod, chip * N4, N4, axis=1).reshape(L, 1, N4)
    mod_part = _mod_fwd(c_all, w_mod, b_cols, "mod_fwd")
    half = Bg // 2
    mine = lax.dynamic_slice_in_dim(mod_part, ci * half, half, axis=1)
    mod_got = _allgather8(mine.reshape(L * half, N4), "gather_mod").reshape(N_CHIP, 2, L, half, N4)
    mod_full = jnp.transpose(mod_got, (2, 1, 3, 0, 4)).reshape(L, Bg, 6, D)
    mod_mine = lax.dynamic_slice_in_dim(mod_full, dev * B, B, axis=1)
    mod = jnp.pad(mod_mine, ((0, 0), (0, 0), (0, SUBLANE - 6), (0, 0)))

    gathers, tokens = [], []
    for l in range(L):
        bufs = [_cast_into_full(W[n], l, cs, chip, "cast_" + n) for n, cs in zip(BIG, BIG_COL_SHARDED)]
        ss, rs, bufs, tok = _gather_start(bufs, BIG_COL_SHARDED, f"gather_start_{l}")
        gathers.append((ss, rs, bufs))
        tokens.append(tok)

    def mixer_params(l):
        vec = jnp.zeros((SUBLANE, BR_W), F32)
        vec = vec.at[0].set(gm_ln_g[l]).at[1].set(gm_ln_b[l]).at[2].set(pool_scale[l])
        conv = jnp.zeros((SUBLANE, BR_W), F32).at[0:3].set(conv_full[l])
        b_s = jnp.zeros((CHUNK, LANE), F32).at[:, 0:HEADS].set(jnp.transpose(gm_b_s[l]))
        return dict(vec=vec, conv=conv, w_s=gm_w_s[l], b_s=b_s, pool_w=pool_w[l])

    xs = x.reshape(T, D)
    saved = []
    for l in range(L):
        prm = mixer_params(l)
        ss, rs, bufs = gathers[l]
        full = dict(zip(BIG, _gather_wait(ss, rs, bufs, BIG_COL_SHARDED, tokens if l == 0 else [xs],
                                          f"gather_wait_{l}")))
        h = _norm_fwd(xs, g_mix[l].reshape(1, D), mod[l], 0, 1, S, "norm_mix_fwd",
                      deps=tokens if l == 0 else ())
        z = _matmul(h, full["w_in"], mode="nn", name="mm_in")
        cat = _mixer_fwd(z, prm, S, "mixer_fwd")
        merged = _proj_fwd(cat, z, full["w_pa"], full["w_pb"], full["w_pc"], D, "proj_fwd")
        x1, mo = _matmul(merged, full["w_o"], mode="nn", name="mm_o", resid=(xs, mod[l], 2, S))
        h2 = _norm_fwd(x1, g_ffn[l].reshape(1, D), mod[l], 3, 4, S, "norm_ffn_fwd")
        ab = _matmul(h2, full["w_13"], mode="nn", name="mm_13")
        act = _swiglu_fwd(ab, "swiglu_fwd")
        x2, ffo = _matmul(act, full["w_2"], mode="nn", name="mm_2", resid=(x1, mod[l], 5, S))
        saved.append(dict(prm=prm, full=full, x0=xs, z=z, cat=cat, merged=merged, mo=mo, x1=x1, ab=ab, act=act,
                          ffo=ffo))
        xs = x2

    dx, head_part = _loss_head(xs, loss_target.reshape(T, D), g_final.reshape(1, D), "loss_head")
    loss = lax.psum(head_part[1, 0], ("x", "y", "c"))

    big_grads = {n: [None] * L for n in BIG}
    small_part = {n: [None] * L for n in SMALL if n not in ("b_mod", "g_final")}
    dmod = [None] * L
    scatters = [None] * L
    sent = ()
    for l in reversed(range(L)):
        sv = saved[l]
        full = sv["full"]
        dffo, pg2 = _gate_bwd(dx, sv["ffo"], mod[l], 5, S, "gate_ffn_bwd", deps=sent)
        dact = _matmul(dffo, full["w_2"], mode="nt", name="mm_2_dgrad")
        dab = _swiglu_bwd(dact, sv["ab"], "swiglu_bwd")
        dh2 = _matmul(dab, full["w_13"], mode="nt", name="mm_13_dgrad")
        dx1, h2, pb2, pgf = _norm_bwd(sv["x1"], dh2, dx, g_ffn[l].reshape(1, D), mod[l], 3, 4, S, "norm_ffn_bwd")
        big_grads["w_2"][l] = _matmul(sv["act"], dffo, mode="tn", name="mm_2_wgrad")
        big_grads["w_13"][l] = _matmul(h2, dab, mode="tn", name="mm_13_wgrad")

        dmo, pg1 = _gate_bwd(dx1, sv["mo"], mod[l], 2, S, "gate_mix_bwd")
        dmerged = _matmul(dmo, full["w_o"], mode="nt", name="mm_o_dgrad")
        big_grads["w_o"][l] = _matmul(sv["merged"], dmo, mode="tn", name="mm_o_wgrad")
        dy, dgl, dcat = _proj_bwd(dmerged, sv["cat"], sv["z"], full["w_pa"], full["w_pb"], full["w_pc"], D,
                                  "proj_bwd")
        for k, n in enumerate(("w_pa", "w_pb", "w_pc")):
            big_grads[n][l] = _matmul(sv["cat"], dy, mode="tn", name="mm_proj_wgrad",
                                      a_cols=(k * BR_W, BR_W), b_cols=(k * D, D))
        dz, pv, dws, dbs, dpw = _mixer_bwd(sv["z"], dcat, dgl, sv["prm"], S, "mixer_bwd")
        dh = _matmul(dz, full["w_in"], mode="nt", name="mm_in_dgrad")
        dx0, h, pb1, pgm = _norm_bwd(sv["x0"], dh, dx1, g_mix[l].reshape(1, D), mod[l], 0, 1, S, "norm_mix_bwd")
        big_grads["w_in"][l] = _matmul(h, dz, mode="tn", name="mm_in_wgrad")
        dx = dx0
        ss, rs, gthru, lands, tok = _scatter_start([big_grads[n][l] for n in BIG], BIG_COL_SHARDED,
                                                   f"scatter_start_{l}")
        scatters[l] = (ss, rs, gthru, lands)
        sent = (tok,)

        dmod[l] = jnp.stack([pb1[:, 0], pb1[:, 1], pg1[:, 0], pb2[:, 0], pb2[:, 1], pg2[:, 0]], axis=1)
        small_part["g_mix"][l] = pgm[0]
        small_part["g_ffn"][l] = pgf[0]
        small_part["gm_ln_g"][l] = pv[0]
        small_part["gm_ln_b"][l] = pv[1]
        small_part["pool_scale"][l] = pv[2]
        small_part["conv_w"][l] = pv[3:6]
        small_part["gm_w_s"][l] = dws
        small_part["gm_b_s"][l] = jnp.transpose(dbs[:, 0:HEADS])
        small_part["pool_w"][l] = dpw
    grad_x = dx.reshape(B, S, D)

    results = {}

    dmod_l = jnp.stack(dmod, axis=0).reshape(L * B, 6 * D) + sent[0][0, 0]
    dmod_rows = -(-(L * B) // SUBLANE) * SUBLANE
    dmod_got = _allgather8(jnp.pad(dmod_l, ((0, dmod_rows - L * B), (0, 0))), "gather_dmod")
    dmod_all = dmod_got.reshape(N_DEV, dmod_rows, 6 * D)[:, :L * B].reshape(N_DEV, L, B, 6 * D)
    dmod_all = jnp.transpose(dmod_all, (1, 0, 2, 3)).reshape(L, Bg, 6 * D)
    dmod_cols = lax.dynamic_slice_in_dim(dmod_all, chip * N4, N4, axis=2)
    g_wmod = _mod_wgrad(c_all, dmod_cols, "mod_wgrad")
    res = _adamw(w_mod.reshape(L * D, N4), m_w_mod.reshape(L * D, N4), v_w_mod.reshape(L * D, N4),
                 [g_wmod.reshape(L * D, N4)], "adamw_w_mod")
    results["w_mod"] = [r.reshape(L, D, N4) for r in res]

    local = dict(b_mod=jnp.sum(jnp.stack(dmod, axis=0), axis=1).reshape(L, 6 * D), g_final=head_part[0])
    for n in small_part:
        local[n] = jnp.stack(small_part[n], axis=0)
    pshapes = [local[n].shape for n in SMALL]
    packed = _pack([local[n] for n in SMALL], LANE)
    prow = packed.shape[0]
    gathered = _allgather8(packed, "gather_small").reshape(N_DEV, prow, LANE)
    g_small = dict(zip(SMALL, _unpack(_sum_devices(gathered, "sum_small"), pshapes)))
    g_small["conv_w"] = lax.dynamic_slice_in_dim(g_small["conv_w"], chip * CW, CW, axis=2)
    wshapes = [W[n].shape for n in SMALL]
    res = _adamw(_pack([W[n] for n in SMALL], LANE), _pack([Mo[n] for n in SMALL], LANE),
                 _pack([Vo[n] for n in SMALL], LANE), [_pack([g_small[n] for n in SMALL], LANE)], "adamw_small")
    small_res = [_unpack(r, wshapes) for r in res]
    for i, n in enumerate(SMALL):
        results[n] = [small_res[j][i] for j in range(4)]

    stacks = [lax.empty((W[n].shape[0] * W[n].shape[1], W[n].shape[2]), F32) for n in BIG]
    after = [res[0]]
    for l in reversed(range(L)):
        ss, rs, gthru, lands = scatters[l]
        gthru, lands = _scatter_wait(ss, rs, gthru, lands, BIG_COL_SHARDED, after, f"scatter_wait_{l}")
        stacks = [_sum_into(st, g, ld, l, cs, chip, "sum_" + n)
                  for st, g, ld, cs, n in zip(stacks, gthru, lands, BIG_COL_SHARDED, BIG)]
        after = [stacks[-1]]
    others = _swap_with_sibling(stacks, "swap_sums")
    for n, own, other in zip(BIG, stacks, others):
        _, R, C = W[n].shape
        res = _adamw(W[n].reshape(L * R, C), Mo[n].reshape(L * R, C), Vo[n].reshape(L * R, C), [own, other],
                     "adamw_" + n)
        results[n] = [r.reshape(L, R, C) for r in res]

    return (loss, grad_x, *[results[n][0] for n in WEIGHTS], *[results[n][1] for n in WEIGHTS],
            *[results[n][2] for n in WEIGHTS], *[results[n][3] for n in WEIGHTS])
```

```python
import jax
import jax.numpy as jnp
from jax import lax
from jax.experimental import pallas as pl
from jax.experimental.pallas import tpu as pltpu

F32 = jnp.float32
BF16 = jnp.bfloat16
MESH = pl.DeviceIdType.MESH

EPS = 1e-6
CHUNK = 128
HEADS = 4
HEAD_DIM = 128
BR_W = 512
N_GROUP = 4
GROUP_DIM = 128
HALO = 16
N_SPLIT = 6 * BR_W
N_CHIP = 4
N_DEV = 8

ADAM_LR = 0.001
ADAM_B1 = 0.9
ADAM_B2 = 0.999
ADAM_EPS = 1e-08
ADAM_WD = 0.01
ADAM_STEP = 10

V7X_VMEM_LIMIT = 48 * 1024 * 1024
LANE = 128
SUBLANE = 8

GELU_K = 0.7978845608028654
GELU_C = 0.044715


def _cparams(sem):
    return pltpu.CompilerParams(dimension_semantics=sem, vmem_limit_bytes=V7X_VMEM_LIMIT)


def _pick(n, cap, q=LANE):
    best = None
    d = q
    while d <= min(n, cap):
        if n % d == 0:
            best = d
        d += q
    return n if best is None else best


def _sigmoid(x):
    return 1.0 / (1.0 + jnp.exp(-x))


def _gelu(x):
    t = jnp.tanh(GELU_K * (x + GELU_C * x * x * x))
    return 0.5 * x * (1.0 + t), t


def _gelu_grad(x, t):
    return 0.5 * (1.0 + t) + 0.5 * x * (1.0 - t * t) * GELU_K * (1.0 + 3.0 * GELU_C * x * x)


def _matmul(a, b, *, mode, name, out_dtype=None, layer=None, a_cols=None, b_cols=None,
            resid=None, deps=(), tm_cap=None, tn_cap=1536, tk_cap=1536):
    out_dtype = BF16 if out_dtype is None else out_dtype
    if tm_cap is None:
        tm_cap = 1536 if mode == "tn" else 1024
    b2 = b.shape[-2:]
    if mode == "nn":
        M, K = a.shape
        N = b2[1]
    elif mode == "nt":
        M, K = a.shape
        N = b2[0]
    else:
        K = a.shape[0]
        M = a.shape[1] if a_cols is None else a_cols[1]
        N = b2[1] if b_cols is None else b_cols[1]
    tm = _pick(M if resid is None else resid[3], tm_cap)
    tn = _pick(N, tn_cap)
    tk = _pick(K, tk_cap)
    nk = K // tk
    a_off = 0 if a_cols is None else a_cols[0] // tm
    b_off = 0 if b_cols is None else b_cols[0] // tn
    if a_cols is not None:
        assert a_cols[0] % tm == 0
    if b_cols is not None:
        assert b_cols[0] % tn == 0

    if mode == "nn":
        a_spec = pl.BlockSpec((tm, tk), lambda i, j, k: (i, k))
        b_blk, b_idx = (tk, tn), (lambda i, j, k: (k, j))
        dims = (((1,), (0,)), ((), ()))
    elif mode == "nt":
        a_spec = pl.BlockSpec((tm, tk), lambda i, j, k: (i, k))
        b_blk, b_idx = (tn, tk), (lambda i, j, k: (j, k))
        dims = (((1,), (1,)), ((), ()))
    else:
        a_spec = pl.BlockSpec((tk, tm), lambda i, j, k: (k, i + a_off))
        b_blk, b_idx = (tk, tn), (lambda i, j, k: (k, j + b_off))
        dims = (((0,), (0,)), ((), ()))
    if layer is None:
        b_spec = pl.BlockSpec(b_blk, b_idx)
    else:
        b_spec = pl.BlockSpec((None,) + b_blk, lambda i, j, k: (layer,) + b_idx(i, j, k))

    in_specs = [a_spec, b_spec]
    operands = [a, b]
    o_spec = pl.BlockSpec((tm, tn), lambda i, j, k: (i, j))
    if resid is not None:
        x, mod, row, seq = resid
        D = mod.shape[-1]
        in_specs += [o_spec, pl.BlockSpec((1, SUBLANE, tn), lambda i, j, k: ((i * tm) // seq, 0, j))]
        operands += [x, mod]
        out_shape = (jax.ShapeDtypeStruct((M, N), F32), jax.ShapeDtypeStruct((M, N), BF16))
        out_specs = (o_spec, o_spec)
        assert seq % tm == 0 and D == N
    else:
        out_shape = jax.ShapeDtypeStruct((M, N), out_dtype)
        out_specs = o_spec

    def finish(acc, refs):
        if resid is not None:
            x_ref, mod_ref, o_ref, p_ref = refs
            o_ref[...] = x_ref[...] + mod_ref[0, row:row + 1, :] * acc
            p_ref[...] = acc.astype(BF16)
        else:
            (o_ref,) = refs
            o_ref[...] = acc.astype(out_dtype)

    n_in = len(operands) - 2
    in_specs += [ANY] * len(deps)
    operands += list(deps)

    def body(a_ref, b_ref, *refs):
        refs = refs[:n_in] + refs[n_in + len(deps):]
        part = lax.dot_general(a_ref[...], b_ref[...], dims, preferred_element_type=F32)
        if nk == 1:
            finish(part, refs)
            return
        acc_ref = refs[-1]
        k = pl.program_id(2)

        @pl.when(k == 0)
        def _():
            acc_ref[...] = part

        @pl.when(k > 0)
        def _():
            acc_ref[...] += part

        @pl.when(k == nk - 1)
        def _():
            finish(acc_ref[...], refs[:-1])

    scratch = [] if nk == 1 else [pltpu.VMEM((tm, tn), F32)]
    return pl.pallas_call(
        body, name=name, grid=(M // tm, N // tn, nk), in_specs=in_specs, out_specs=out_specs,
        out_shape=out_shape, scratch_shapes=scratch,
        compiler_params=_cparams(("parallel", "parallel", "arbitrary")),
    )(*operands)


def _row_tile(seq, cap):
    return _pick(seq, cap, HALO)


def _mod_spec(tm, seq, D):
    return pl.BlockSpec((1, SUBLANE, D), lambda i: ((i * tm) // seq, 0, 0))


def _norm_fwd(x, g, mod, shift_row, scale_row, seq, name, deps=()):
    T, D = x.shape
    tm = _row_tile(seq, 512)

    def body(x_ref, g_ref, mod_ref, *rest):
        h_ref = rest[-1]
        xv = x_ref[...]
        rstd = lax.rsqrt(jnp.mean(xv * xv, axis=-1, keepdims=True) + EPS)
        n = xv * rstd * g_ref[...]
        h = n * (1.0 + mod_ref[0, scale_row:scale_row + 1, :]) + mod_ref[0, shift_row:shift_row + 1, :]
        h_ref[...] = h.astype(BF16)

    row = pl.BlockSpec((tm, D), lambda i: (i, 0))
    return pl.pallas_call(
        body, name=name, grid=(T // tm,),
        in_specs=[row, pl.BlockSpec((1, D), lambda i: (0, 0)), _mod_spec(tm, seq, D)] + [ANY] * len(deps),
        out_specs=row, out_shape=jax.ShapeDtypeStruct((T, D), BF16),
        compiler_params=_cparams(("parallel",)),
    )(x, g, mod, *deps)


def _norm_bwd(x, dh, dres, g, mod, shift_row, scale_row, seq, name):
    T, D = x.shape
    B = mod.shape[0]
    tm = _row_tile(seq, 512)
    per_seq = seq // tm

    def body(x_ref, dh_ref, dres_ref, g_ref, mod_ref, dx_ref, h_ref, pb_ref, pg_ref):
        i = pl.program_id(0)
        xv = x_ref[...]
        dhv = dh_ref[...].astype(F32)
        gv = g_ref[...]
        scale1 = 1.0 + mod_ref[0, scale_row:scale_row + 1, :]
        rstd = lax.rsqrt(jnp.mean(xv * xv, axis=-1, keepdims=True) + EPS)
        xhat = xv * rstd
        n = xhat * gv
        dn = dhv * scale1
        dxhat = dn * gv
        dx = rstd * (dxhat - xhat * jnp.mean(dxhat * xhat, axis=-1, keepdims=True))
        dx_ref[...] = dres_ref[...] + dx
        h_ref[...] = (n * scale1 + mod_ref[0, shift_row:shift_row + 1, :]).astype(BF16)

        @pl.when(i % per_seq == 0)
        def _():
            pb_ref[...] = jnp.zeros_like(pb_ref)

        @pl.when(i == 0)
        def _():
            pg_ref[...] = jnp.zeros_like(pg_ref)

        pb_ref[0, 0:1, :] += jnp.sum(dhv, axis=0, keepdims=True)
        pb_ref[0, 1:2, :] += jnp.sum(dhv * n, axis=0, keepdims=True)
        pg_ref[0:1, :] += jnp.sum(dn * xhat, axis=0, keepdims=True)

    row = pl.BlockSpec((tm, D), lambda i: (i, 0))
    return pl.pallas_call(
        body, name=name, grid=(T // tm,),
        in_specs=[row, row, row, pl.BlockSpec((1, D), lambda i: (0, 0)), _mod_spec(tm, seq, D)],
        out_specs=(row, row, _mod_spec(tm, seq, D), pl.BlockSpec((SUBLANE, D), lambda i: (0, 0))),
        out_shape=(jax.ShapeDtypeStruct((T, D), F32), jax.ShapeDtypeStruct((T, D), BF16),
                   jax.ShapeDtypeStruct((B, SUBLANE, D), F32), jax.ShapeDtypeStruct((SUBLANE, D), F32)),
        compiler_params=_cparams(("arbitrary",)),
    )(x, dh, dres, g, mod)


def _gate_bwd(dx, prod, mod, gate_row, seq, name, deps=()):
    T, D = dx.shape
    B = mod.shape[0]
    tm = _row_tile(seq, 512)
    per_seq = seq // tm

    def body(dx_ref, p_ref, mod_ref, *rest):
        dp_ref, pb_ref = rest[-2:]
        i = pl.program_id(0)
        dxv = dx_ref[...]
        dp_ref[...] = (dxv * mod_ref[0, gate_row:gate_row + 1, :]).astype(BF16)

        @pl.when(i % per_seq == 0)
        def _():
            pb_ref[...] = jnp.zeros_like(pb_ref)

        pb_ref[0, 0:1, :] += jnp.sum(dxv * p_ref[...].astype(F32), axis=0, keepdims=True)

    row = pl.BlockSpec((tm, D), lambda i: (i, 0))
    return pl.pallas_call(
        body, name=name, grid=(T // tm,),
        in_specs=[row, row, _mod_spec(tm, seq, D)] + [ANY] * len(deps),
        out_specs=(row, _mod_spec(tm, seq, D)),
        out_shape=(jax.ShapeDtypeStruct((T, D), BF16), jax.ShapeDtypeStruct((B, SUBLANE, D), F32)),
        compiler_params=_cparams(("arbitrary",)),
    )(dx, prod, mod, *deps)


def _swiglu_fwd(ab, name):
    T, F2 = ab.shape
    Fh = F2 // 2
    tm = _pick(T, 256, HALO)

    def body(ab_ref, act_ref):
        a = ab_ref[:, :Fh].astype(F32)
        b = ab_ref[:, Fh:].astype(F32)
        act_ref[...] = (a * _sigmoid(a) * b).astype(BF16)

    return pl.pallas_call(
        body, name=name, grid=(T // tm,),
        in_specs=[pl.BlockSpec((tm, F2), lambda i: (i, 0))],
        out_specs=pl.BlockSpec((tm, Fh), lambda i: (i, 0)),
        out_shape=jax.ShapeDtypeStruct((T, Fh), BF16),
        compiler_params=_cparams(("parallel",)),
    )(ab)


def _swiglu_bwd(dact, ab, name):
    T, F2 = ab.shape
    Fh = F2 // 2
    tm = _pick(T, 256, HALO)

    def body(d_ref, ab_ref, dab_ref):
        d = d_ref[...].astype(F32)
        a = ab_ref[:, :Fh].astype(F32)
        b = ab_ref[:, Fh:].astype(F32)
        sg = _sigmoid(a)
        dab_ref[:, :Fh] = (d * b * sg * (1.0 + a * (1.0 - sg))).astype(BF16)
        dab_ref[:, Fh:] = (d * a * sg).astype(BF16)

    return pl.pallas_call(
        body, name=name, grid=(T // tm,),
        in_specs=[pl.BlockSpec((tm, Fh), lambda i: (i, 0)), pl.BlockSpec((tm, F2), lambda i: (i, 0))],
        out_specs=pl.BlockSpec((tm, F2), lambda i: (i, 0)),
        out_shape=jax.ShapeDtypeStruct((T, F2), BF16),
        compiler_params=_cparams(("parallel",)),
    )(dact, ab)


def _shift_down(v, d):
    return pltpu.roll(v, d, 0)


def _shift_up(v, d):
    return pltpu.roll(v, v.shape[0] - d, 0)


def _tril_mask():
    r = lax.broadcasted_iota(jnp.int32, (CHUNK, CHUNK), 0)
    c = lax.broadcasted_iota(jnp.int32, (CHUNK, CHUNK), 1)
    return c <= r


def _pool_counts(i, tm, seq, rows, first_row):
    r = lax.broadcasted_iota(jnp.int32, (rows, 1), 0) + (i * tm + first_row)
    pos1 = (r % seq + 1).astype(F32)
    return [jnp.minimum(pos1, float(2 << g)) for g in range(N_GROUP)]


def _mixer_forward_values(zt, hxb, hch, i, tm, seq, ln_g, ln_b, ws_ref, bs_ref, pw_ref, pscale, cw_ref):
    u = zt[:, 0 * BR_W:1 * BR_W]
    v = zt[:, 1 * BR_W:2 * BR_W]
    xb = zt[:, 2 * BR_W:3 * BR_W]
    bg = zt[:, 3 * BR_W:4 * BR_W]
    cg = zt[:, 4 * BR_W:5 * BR_W]
    hc = zt[:, 5 * BR_W:6 * BR_W]
    out = {}

    ug, tu = _gelu(u)
    vg, tv = _gelu(v)
    mu = jnp.mean(vg, axis=-1, keepdims=True)
    vc = vg - mu
    rstd = lax.rsqrt(jnp.mean(vc * vc, axis=-1, keepdims=True) + EPS)
    vhat = vc * rstd
    vn = (vhat * ln_g + ln_b).astype(BF16)
    mask = _tril_mask()
    wt = [jnp.where(mask, ws_ref[h], 0.0).astype(BF16) for h in range(HEADS)]
    rows = []
    for n in range(tm // CHUNK):
        blocks = []
        for h in range(HEADS):
            blk = vn[n * CHUNK:(n + 1) * CHUNK, h * HEAD_DIM:(h + 1) * HEAD_DIM]
            sb = jnp.dot(wt[h], blk, preferred_element_type=F32) + bs_ref[:, h:h + 1]
            blocks.append(sb)
        rows.append(jnp.concatenate(blocks, axis=1))
    s = jnp.concatenate(rows, axis=0) if len(rows) > 1 else rows[0]
    out.update(u=u, v=v, ug=ug, tu=tu, tv=tv, rstd=rstd, vhat=vhat, vn=vn, wt=wt, s=s, a_out=ug * s)

    ext = jnp.concatenate([hxb, xb], axis=0)
    cnt = _pool_counts(i, tm, seq, tm, 0)
    p, qs = [], []
    for g in range(N_GROUP):
        e = ext[:, g * GROUP_DIM:(g + 1) * GROUP_DIM]
        acc = e
        for d in (1, 2, 4, 8)[:g + 1]:
            acc = acc + _shift_down(acc, d)
        pg = acc[HALO:, :] / cnt[g] - xb[:, g * GROUP_DIM:(g + 1) * GROUP_DIM]
        p.append(pg.astype(BF16))
        qs.append(jnp.dot(p[g], pw_ref[g].astype(BF16), preferred_element_type=F32))
    q = jnp.concatenate(qs, axis=1)
    out.update(p=p, q=q, b_out=q * pscale)

    zc = cg * hc
    zce = jnp.concatenate([hch[:, :BR_W] * hch[:, BR_W:], zc], axis=0)
    z1 = _shift_down(zce, 1)[HALO:, :]
    z2 = _shift_down(zce, 2)[HALO:, :]
    y = cw_ref[0:1, :] * z2 + cw_ref[1:2, :] * z1 + cw_ref[2:3, :] * zc
    out.update(bg=bg, cg=cg, hc=hc, zc=zc, z1=z1, z2=z2, y=y, c_out=bg * y)
    return out


def _mixer_specs(tm, T):
    nb = T // HALO
    per = tm // HALO
    prev = lambda i: jnp.maximum(i * per - 1, 0)
    nxt = lambda i: jnp.minimum((i + 1) * per, nb - 1)
    return prev, nxt


def _mixer_fwd(z, prm, seq, name):
    T = z.shape[0]
    tm = _row_tile(seq, 256)
    per_seq = seq // tm
    prev, _ = _mixer_specs(tm, T)

    def body(z_ref, hxb_ref, hch_ref, vec_ref, cw_ref, ws_ref, bs_ref, pw_ref, cat_ref):
        i = pl.program_id(0)
        keep = jnp.where(i % per_seq == 0, 0.0, 1.0)
        zt = z_ref[...].astype(F32)
        hxb = hxb_ref[...].astype(F32) * keep
        hch = hch_ref[...].astype(F32) * keep
        o = _mixer_forward_values(zt, hxb, hch, i, tm, seq, vec_ref[0:1, :], vec_ref[1:2, :],
                                  ws_ref, bs_ref, pw_ref, vec_ref[2:3, :], cw_ref)
        cat_ref[:, 0 * BR_W:1 * BR_W] = o["a_out"].astype(BF16)
        cat_ref[:, 1 * BR_W:2 * BR_W] = o["b_out"].astype(BF16)
        cat_ref[:, 2 * BR_W:3 * BR_W] = o["c_out"].astype(BF16)

    full = lambda shape: pl.BlockSpec(shape, lambda i: (0,) * len(shape))
    return pl.pallas_call(
        body, name=name, grid=(T // tm,),
        in_specs=[pl.BlockSpec((tm, N_SPLIT), lambda i: (i, 0)),
                  pl.BlockSpec((HALO, BR_W), lambda i: (prev(i), 2)),
                  pl.BlockSpec((HALO, 2 * BR_W), lambda i: (prev(i), 2)),
                  full((SUBLANE, BR_W)), full((SUBLANE, BR_W)), full((HEADS, CHUNK, CHUNK)),
                  full((CHUNK, LANE)), full((N_GROUP, GROUP_DIM, GROUP_DIM))],
        out_specs=pl.BlockSpec((tm, 3 * BR_W), lambda i: (i, 0)),
        out_shape=jax.ShapeDtypeStruct((T, 3 * BR_W), BF16),
        compiler_params=_cparams(("parallel",)),
    )(z, z, z, prm["vec"], prm["conv"], prm["w_s"], prm["b_s"], prm["pool_w"])


def _mixer_bwd(z, dcat, dgl, prm, seq, name):
    T, IN = z.shape
    GL = IN - N_SPLIT
    tm = _row_tile(seq, 256)
    per_seq = seq // tm
    prev, nxt = _mixer_specs(tm, T)
    nrow = tm + HALO

    def body(z_ref, hxb_ref, hch_ref, nbg_ref, dcat_ref, ndb_ref, ndc_ref, dgl_ref,
             vec_ref, cw_ref, ws_ref, bs_ref, pw_ref,
             dz_ref, pv_ref, dws_ref, dbs_ref, dpw_ref):
        i = pl.program_id(0)
        keep_prev = jnp.where(i % per_seq == 0, 0.0, 1.0)
        keep_next = jnp.where(i % per_seq == per_seq - 1, 0.0, 1.0)
        zt = z_ref[...].astype(F32)
        hxb = hxb_ref[...].astype(F32) * keep_prev
        hch = hch_ref[...].astype(F32) * keep_prev
        ln_g = vec_ref[0:1, :]
        pscale = vec_ref[2:3, :]
        o = _mixer_forward_values(zt, hxb, hch, i, tm, seq, ln_g, vec_ref[1:2, :],
                                  ws_ref, bs_ref, pw_ref, pscale, cw_ref)
        dcv = dcat_ref[...].astype(F32)
        da = dcv[:, 0 * BR_W:1 * BR_W]
        db = dcv[:, 1 * BR_W:2 * BR_W]
        dc = dcv[:, 2 * BR_W:3 * BR_W]
        mask = _tril_mask()

        @pl.when(i == 0)
        def _():
            pv_ref[...] = jnp.zeros_like(pv_ref)
            dws_ref[...] = jnp.zeros_like(dws_ref)
            dbs_ref[...] = jnp.zeros_like(dbs_ref)
            dpw_ref[...] = jnp.zeros_like(dpw_ref)

        d_ug = da * o["s"]
        ds = da * o["ug"]
        ds_b = ds.astype(BF16)
        vn = o["vn"]
        dvn_rows = []
        dws = [jnp.zeros((CHUNK, CHUNK), F32) for _ in range(HEADS)]
        dsum = jnp.zeros((CHUNK, BR_W), F32)
        for n in range(tm // CHUNK):
            blocks = []
            rs = slice(n * CHUNK, (n + 1) * CHUNK)
            dsum = dsum + ds[rs, :]
            for h in range(HEADS):
                cs = slice(h * HEAD_DIM, (h + 1) * HEAD_DIM)
                dsb = ds_b[rs, cs]
                blocks.append(lax.dot_general(o["wt"][h], dsb, (((0,), (0,)), ((), ())),
                                              preferred_element_type=F32))
                dws[h] = dws[h] + lax.dot_general(dsb, vn[rs, cs], (((1,), (1,)), ((), ())),
                                                  preferred_element_type=F32)
            dvn_rows.append(jnp.concatenate(blocks, axis=1))
        dvn = jnp.concatenate(dvn_rows, axis=0) if len(dvn_rows) > 1 else dvn_rows[0]
        lane = lax.broadcasted_iota(jnp.int32, (CHUNK, LANE), 1)
        dbs_t = jnp.zeros((CHUNK, LANE), F32)
        for h in range(HEADS):
            dws_ref[h] += jnp.where(mask, dws[h], 0.0)
            rsum = jnp.sum(dsum[:, h * HEAD_DIM:(h + 1) * HEAD_DIM], axis=1, keepdims=True)
            dbs_t = dbs_t + jnp.where(lane == h, rsum, 0.0)
        dbs_ref[...] += dbs_t
        vhat = o["vhat"]
        pv_ref[0:1, :] += jnp.sum(dvn * vhat, axis=0, keepdims=True)
        pv_ref[1:2, :] += jnp.sum(dvn, axis=0, keepdims=True)
        dvhat = dvn * ln_g
        dvg = o["rstd"] * (dvhat - jnp.mean(dvhat, axis=-1, keepdims=True)
                           - vhat * jnp.mean(dvhat * vhat, axis=-1, keepdims=True))
        du = d_ug * _gelu_grad(o["u"], o["tu"])
        dv = dvg * _gelu_grad(o["v"], o["tv"])

        pv_ref[2:3, :] += jnp.sum(db * o["q"], axis=0, keepdims=True)
        dq = (db * pscale).astype(BF16)
        dqn = (ndb_ref[...].astype(F32) * pscale * keep_next).astype(BF16)
        cnt = _pool_counts(i, tm, seq, nrow, 0)
        dxb = []
        for g in range(N_GROUP):
            cs = slice(g * GROUP_DIM, (g + 1) * GROUP_DIM)
            pwg = pw_ref[g].astype(BF16)
            dpw_ref[g] += lax.dot_general(o["p"][g], dq[:, cs], (((0,), (0,)), ((), ())),
                                          preferred_element_type=F32)
            dp = lax.dot_general(dq[:, cs], pwg, (((1,), (1,)), ((), ())), preferred_element_type=F32)
            dpn = lax.dot_general(dqn[:, cs], pwg, (((1,), (1,)), ((), ())), preferred_element_type=F32)
            acc = jnp.concatenate([dp, dpn], axis=0) / cnt[g]
            for d in (1, 2, 4, 8)[:g + 1]:
                acc = acc + _shift_up(acc, d)
            dxb.append(acc[:tm, :] - dp)
        dxb = jnp.concatenate(dxb, axis=1)

        dbg = dc * o["y"]
        dy = dc * o["bg"]
        pv_ref[3:4, :] += jnp.sum(dy * o["z2"], axis=0, keepdims=True)
        pv_ref[4:5, :] += jnp.sum(dy * o["z1"], axis=0, keepdims=True)
        pv_ref[5:6, :] += jnp.sum(dy * o["zc"], axis=0, keepdims=True)
        dyn = ndc_ref[...].astype(F32) * nbg_ref[...].astype(F32) * keep_next
        dye = jnp.concatenate([dy, dyn], axis=0)
        dzc = (cw_ref[2:3, :] * dy + cw_ref[1:2, :] * _shift_up(dye, 1)[:tm, :]
               + cw_ref[0:1, :] * _shift_up(dye, 2)[:tm, :])
        dcg = dzc * o["hc"]
        dhc = dzc * o["cg"]

        dz_ref[:, 0 * BR_W:1 * BR_W] = du.astype(BF16)
        dz_ref[:, 1 * BR_W:2 * BR_W] = dv.astype(BF16)
        dz_ref[:, 2 * BR_W:3 * BR_W] = dxb.astype(BF16)
        dz_ref[:, 3 * BR_W:4 * BR_W] = dbg.astype(BF16)
        dz_ref[:, 4 * BR_W:5 * BR_W] = dcg.astype(BF16)
        dz_ref[:, 5 * BR_W:6 * BR_W] = dhc.astype(BF16)
        dz_ref[:, N_SPLIT:] = dgl_ref[...]

    full = lambda shape: pl.BlockSpec(shape, lambda i: (0,) * len(shape))
    return pl.pallas_call(
        body, name=name, grid=(T // tm,),
        in_specs=[pl.BlockSpec((tm, N_SPLIT), lambda i: (i, 0)),
                  pl.BlockSpec((HALO, BR_W), lambda i: (prev(i), 2)),
                  pl.BlockSpec((HALO, 2 * BR_W), lambda i: (prev(i), 2)),
                  pl.BlockSpec((HALO, BR_W), lambda i: (nxt(i), 3)),
                  pl.BlockSpec((tm, 3 * BR_W), lambda i: (i, 0)),
                  pl.BlockSpec((HALO, BR_W), lambda i: (nxt(i), 1)),
                  pl.BlockSpec((HALO, BR_W), lambda i: (nxt(i), 2)),
                  pl.BlockSpec((tm, GL), lambda i: (i, 0)),
                  full((SUBLANE, BR_W)), full((SUBLANE, BR_W)), full((HEADS, CHUNK, CHUNK)),
                  full((CHUNK, LANE)), full((N_GROUP, GROUP_DIM, GROUP_DIM))],
        out_specs=(pl.BlockSpec((tm, IN), lambda i: (i, 0)),
                   full((SUBLANE, BR_W)), full((HEADS, CHUNK, CHUNK)), full((CHUNK, LANE)),
                   full((N_GROUP, GROUP_DIM, GROUP_DIM))),
        out_shape=(jax.ShapeDtypeStruct((T, IN), BF16),
                   jax.ShapeDtypeStruct((SUBLANE, BR_W), F32),
                   jax.ShapeDtypeStruct((HEADS, CHUNK, CHUNK), F32),
                   jax.ShapeDtypeStruct((CHUNK, LANE), F32),
                   jax.ShapeDtypeStruct((N_GROUP, GROUP_DIM, GROUP_DIM), F32)),
        compiler_params=_cparams(("arbitrary",)),
    )(z, z, z, z, dcat, dcat, dcat, dgl, prm["vec"], prm["conv"], prm["w_s"], prm["b_s"], prm["pool_w"])


def _proj_fwd(cat, z, w_pa, w_pb, w_pc, D, name):
    T, IN = z.shape
    GL = 3 * D
    assert N_SPLIT % GL == 0
    glb = N_SPLIT // GL
    tm = _pick(T, 256, HALO)

    def body(cat_ref, gl_ref, wa_ref, wb_ref, wc_ref, m_ref):
        acc = jnp.zeros((tm, D), F32)
        for k, w_ref in enumerate((wa_ref, wb_ref, wc_ref)):
            y = jnp.dot(cat_ref[:, k * BR_W:(k + 1) * BR_W], w_ref[...], preferred_element_type=F32)
            acc = acc + _sigmoid(gl_ref[:, k * D:(k + 1) * D].astype(F32)) * y
        m_ref[...] = acc.astype(BF16)

    wspec = pl.BlockSpec((BR_W, D), lambda i: (0, 0))
    return pl.pallas_call(
        body, name=name, grid=(T // tm,),
        in_specs=[pl.BlockSpec((tm, 3 * BR_W), lambda i: (i, 0)),
                  pl.BlockSpec((tm, GL), lambda i: (i, glb)), wspec, wspec, wspec],
        out_specs=pl.BlockSpec((tm, D), lambda i: (i, 0)),
        out_shape=jax.ShapeDtypeStruct((T, D), BF16),
        compiler_params=_cparams(("parallel",)),
    )(cat, z, w_pa, w_pb, w_pc)


def _proj_bwd(dmerged, cat, z, w_pa, w_pb, w_pc, D, name):
    T, IN = z.shape
    GL = 3 * D
    glb = N_SPLIT // GL
    tm = _pick(T, 256, HALO)

    def body(dm_ref, cat_ref, gl_ref, wa_ref, wb_ref, wc_ref, dy_ref, dgl_ref, dcat_ref):
        dm = dm_ref[...].astype(F32)
        for k, w_ref in enumerate((wa_ref, wb_ref, wc_ref)):
            w = w_ref[...]
            y = jnp.dot(cat_ref[:, k * BR_W:(k + 1) * BR_W], w, preferred_element_type=F32)
            sg = _sigmoid(gl_ref[:, k * D:(k + 1) * D].astype(F32))
            dyk = (dm * sg).astype(BF16)
            dy_ref[:, k * D:(k + 1) * D] = dyk
            dgl_ref[:, k * D:(k + 1) * D] = (dm * y * sg * (1.0 - sg)).astype(BF16)
            dcat_ref[:, k * BR_W:(k + 1) * BR_W] = lax.dot_general(
                dyk, w, (((1,), (1,)), ((), ())), preferred_element_type=F32).astype(BF16)

    wspec = pl.BlockSpec((BR_W, D), lambda i: (0, 0))
    return pl.pallas_call(
        body, name=name, grid=(T // tm,),
        in_specs=[pl.BlockSpec((tm, D), lambda i: (i, 0)),
                  pl.BlockSpec((tm, 3 * BR_W), lambda i: (i, 0)),
                  pl.BlockSpec((tm, GL), lambda i: (i, glb)), wspec, wspec, wspec],
        out_specs=(pl.BlockSpec((tm, GL), lambda i: (i, 0)), pl.BlockSpec((tm, GL), lambda i: (i, 0)),
                   pl.BlockSpec((tm, 3 * BR_W), lambda i: (i, 0))),
        out_shape=(jax.ShapeDtypeStruct((T, GL), BF16), jax.ShapeDtypeStruct((T, GL), BF16),
                   jax.ShapeDtypeStruct((T, 3 * BR_W), BF16)),
        compiler_params=_cparams(("parallel",)),
    )(dmerged, cat, z, w_pa, w_pb, w_pc)


def _loss_head(x, target, g, name):
    T, D = x.shape
    tm = _pick(T, 512, SUBLANE)

    def body(x_ref, t_ref, g_ref, dx_ref, part_ref):
        i = pl.program_id(0)
        xv = x_ref[...]
        gv = g_ref[...]
        rstd = lax.rsqrt(jnp.mean(xv * xv, axis=-1, keepdims=True) + EPS)
        xhat = xv * rstd
        err = xhat * gv - t_ref[...]
        dy = err * (1.0 / D)
        dxhat = dy * gv
        dx_ref[...] = rstd * (dxhat - xhat * jnp.mean(dxhat * xhat, axis=-1, keepdims=True))

        @pl.when(i == 0)
        def _():
            part_ref[...] = jnp.zeros_like(part_ref)

        part_ref[0:1, :] += jnp.sum(dy * xhat, axis=0, keepdims=True)
        part_ref[1:2, :] += jnp.zeros((1, D), F32) + (0.5 / D) * jnp.sum(err * err)

    row = pl.BlockSpec((tm, D), lambda i: (i, 0))
    return pl.pallas_call(
        body, name=name, grid=(T // tm,),
        in_specs=[row, row, pl.BlockSpec((1, D), lambda i: (0, 0))],
        out_specs=(row, pl.BlockSpec((SUBLANE, D), lambda i: (0, 0))),
        out_shape=(jax.ShapeDtypeStruct((T, D), F32), jax.ShapeDtypeStruct((SUBLANE, D), F32)),
        compiler_params=_cparams(("arbitrary",)),
    )(x, target, g)


def _mod_fwd(c_all, w_mod, b_cols, name):
    L, D, N4 = w_mod.shape
    Bg = c_all.shape[0]
    tn = _pick(N4, 768)

    def body(c_ref, w_ref, b_ref, o_ref):
        cv = c_ref[...]
        ca = (cv * _sigmoid(cv)).astype(BF16)
        o_ref[...] = jnp.dot(ca, w_ref[...].astype(BF16), preferred_element_type=F32) + b_ref[...]

    return pl.pallas_call(
        body, name=name, grid=(L, N4 // tn),
        in_specs=[pl.BlockSpec((Bg, D), lambda l, j: (0, 0)),
                  pl.BlockSpec((None, D, tn), lambda l, j: (l, 0, j)),
                  pl.BlockSpec((None, 1, tn), lambda l, j: (l, 0, j))],
        out_specs=pl.BlockSpec((None, Bg, tn), lambda l, j: (l, 0, j)),
        out_shape=jax.ShapeDtypeStruct((L, Bg, N4), F32),
        compiler_params=_cparams(("parallel", "parallel")),
    )(c_all, w_mod, b_cols)


def _mod_wgrad(c_all, dmod_cols, name, deps=()):
    L, Bg, N4 = dmod_cols.shape
    D = c_all.shape[1]
    tn = _pick(N4, 768)

    def body(c_ref, d_ref, *rest):
        cv = c_ref[...]
        ca = (cv * _sigmoid(cv)).astype(BF16)
        rest[-1][...] = lax.dot_general(ca, d_ref[...].astype(BF16), (((0,), (0,)), ((), ())),
                                     preferred_element_type=F32)

    return pl.pallas_call(
        body, name=name, grid=(L, N4 // tn),
        in_specs=[pl.BlockSpec((Bg, D), lambda l, j: (0, 0)),
                  pl.BlockSpec((None, Bg, tn), lambda l, j: (l, 0, j))] + [ANY] * len(deps),
        out_specs=pl.BlockSpec((None, D, tn), lambda l, j: (l, 0, j)),
        out_shape=jax.ShapeDtypeStruct((L, D, N4), F32),
        compiler_params=_cparams(("parallel", "parallel")),
    )(c_all, dmod_cols, *deps)


def _rows_tile(R, C):
    return _pick(R, max(SUBLANE, (256 * 1024) // C), SUBLANE)


def _cast_into_full(w, layer, col_sharded, chip, name, deps=()):
    L, R, C = w.shape
    K, N = (R, C * N_CHIP) if col_sharded else (R * N_CHIP, C)
    tr = _pick(R, max(HALO, (512 * 1024) // C), HALO)
    nb = R // tr

    def body(q_ref, w_ref, *rest):
        rest[-1][...] = w_ref[...].astype(BF16)

    out_idx = (lambda i, q: (i, q[0])) if col_sharded else (lambda i, q: (q[0] * nb + i, 0))
    grid_spec = pltpu.PrefetchScalarGridSpec(
        num_scalar_prefetch=1, grid=(nb,),
        in_specs=[pl.BlockSpec((None, tr, C), lambda i, q: (layer, i, 0))] + [ANY] * len(deps),
        out_specs=pl.BlockSpec((tr, C), out_idx))
    return pl.pallas_call(
        body, name=name, grid_spec=grid_spec, out_shape=jax.ShapeDtypeStruct((K, N), BF16),
        compiler_params=_cparams(("arbitrary",)),
    )(chip.reshape(1).astype(jnp.int32), w, *deps)


def _sum_into(stack, grad, slots, layer, col_sharded, chip, name):
    _, R, C = slots.shape
    tr = _rows_tile(R, C)
    nb = R // tr

    def body(q_ref, stack_ref, g_ref, s_ref, o_ref):
        o_ref[...] = ((g_ref[...].astype(F32) + s_ref[0].astype(F32)) + s_ref[1].astype(F32)) + s_ref[2].astype(F32)

    g_idx = (lambda i, q: (i, q[0])) if col_sharded else (lambda i, q: (q[0] * nb + i, 0))
    grid_spec = pltpu.PrefetchScalarGridSpec(
        num_scalar_prefetch=1, grid=(nb,),
        in_specs=[ANY, pl.BlockSpec((tr, C), g_idx), pl.BlockSpec((3, tr, C), lambda i, q: (0, i, 0))],
        out_specs=pl.BlockSpec((tr, C), lambda i, q: (layer * nb + i, 0)))
    return pl.pallas_call(
        body, name=name, grid_spec=grid_spec, out_shape=jax.ShapeDtypeStruct(stack.shape, F32),
        input_output_aliases={1: 0}, compiler_params=_cparams(("arbitrary",)),
    )(chip.reshape(1).astype(jnp.int32), stack, grad, slots)


def _sum_devices(parts, name):
    n, R, C = parts.shape
    tr = _rows_tile(R, C)

    def body(s_ref, o_ref):
        acc = s_ref[0]
        for d in range(1, n):
            acc = acc + s_ref[d]
        o_ref[...] = acc

    return pl.pallas_call(
        body, name=name, grid=(R // tr,),
        in_specs=[pl.BlockSpec((n, tr, C), lambda i: (0, i, 0))],
        out_specs=pl.BlockSpec((tr, C), lambda i: (i, 0)),
        out_shape=jax.ShapeDtypeStruct((R, C), F32), compiler_params=_cparams(("parallel",)),
    )(parts)


def _adamw(w, m, v, grads, name):
    R, C = w.shape
    tr = _rows_tile(R, C)
    bc1 = 1.0 - ADAM_B1 ** ADAM_STEP
    bc2 = 1.0 - ADAM_B2 ** ADAM_STEP
    ng = len(grads)

    def body(*refs):
        w_ref, m_ref, v_ref = refs[:3]
        g_refs = refs[3:3 + ng]
        g_out, d_out, m_out, v_out = refs[3 + ng:]
        g = g_refs[0][...]
        for r in g_refs[1:]:
            g = g + r[...]
        mn = ADAM_B1 * m_ref[...] + (1.0 - ADAM_B1) * g
        vn = ADAM_B2 * v_ref[...] + (1.0 - ADAM_B2) * (g * g)
        m_hat = mn / bc1
        v_hat = vn / bc2
        g_out[...] = g
        d_out[...] = -ADAM_LR * (m_hat / (jnp.sqrt(v_hat) + ADAM_EPS) + ADAM_WD * w_ref[...])
        m_out[...] = mn
        v_out[...] = vn

    spec = pl.BlockSpec((tr, C), lambda i: (i, 0))
    sds = jax.ShapeDtypeStruct((R, C), F32)
    return pl.pallas_call(
        body, name=name, grid=(R // tr,), in_specs=[spec] * (3 + ng), out_specs=(spec,) * 4,
        out_shape=(sds,) * 4, compiler_params=_cparams(("parallel",)),
    )(w, m, v, *grads)


def _place():
    x, y, c = lax.axis_index("x"), lax.axis_index("y"), lax.axis_index("c")
    chips = [(1 - x, y), (x, 1 - y), (1 - x, 1 - y)]
    return x, y, c, chips


ANY = pl.BlockSpec(memory_space=pl.ANY)


def _allgather8(v, name):
    m_per, n = v.shape

    def body(x_ref, out_ref, send_sems, recv_sems, local_sem):
        x, y, c, chips = _place()
        me, sibling = (x, y, c), (x, y, 1 - c)

        def rows(px, py, pc):
            return out_ref.at[pl.ds((4 * px + 2 * py + pc) * m_per, m_per), :]

        def copy(k, block, to, src=None):
            return pltpu.make_async_remote_copy(
                src_ref=rows(*block) if src is None else src, dst_ref=rows(*block),
                send_sem=send_sems.at[k], recv_sem=recv_sems.at[k], device_id=to, device_id_type=MESH)

        mine = pltpu.make_async_copy(x_ref, rows(*me), local_sem)
        mine.start()
        first = [copy(0, me, sibling, src=x_ref)]
        first += [copy(1 + j, me, (*chip, c), src=x_ref) for j, chip in enumerate(chips)]
        for cp in first:
            cp.start()
        passed = [copy(4 + j, (*chip, c), sibling) for j, chip in enumerate(chips)]
        for j, chip in enumerate(chips):
            copy(1 + j, (*chip, c), me).wait_recv()
            passed[j].start()
        copy(0, sibling, me).wait_recv()
        for j, chip in enumerate(chips):
            copy(4 + j, (*chip, 1 - c), me).wait_recv()
        for cp in first + passed:
            cp.wait_send()
        mine.wait()

    return pl.pallas_call(
        body, name=name, out_shape=jax.ShapeDtypeStruct((N_DEV * m_per, n), v.dtype),
        in_specs=[ANY], out_specs=ANY,
        scratch_shapes=[pltpu.SemaphoreType.DMA((7,)), pltpu.SemaphoreType.DMA((7,)), pltpu.SemaphoreType.DMA],
    )(v)


def _window(ref, col_sharded, q, lead):
    full = (slice(None),) * lead
    if col_sharded:
        width = ref.shape[-1] // N_CHIP
        return ref.at[full + (slice(None), pl.ds(pl.multiple_of(q * width, LANE), width))]
    height = ref.shape[-2] // N_CHIP
    return ref.at[full + (pl.ds(pl.multiple_of(q * height, HALO), height), slice(None))]


HBM = pl.BlockSpec(memory_space=pltpu.HBM)
SEM = pl.BlockSpec(memory_space=pltpu.SEMAPHORE)
EFFECT = pltpu.SideEffectType.DATAFLOW_SIDE_EFFECTING


def _in_hbm(v):
    return pltpu.with_memory_space_constraint(v, pltpu.HBM)


def _gather_start(bufs, col_sharded, name):
    n = len(bufs)

    def body(*refs):
        ins = refs[:n]
        send_sems, recv_sems = refs[n], refs[n + 1]
        token = refs[-1]
        x, y, c, chips = _place()
        q = 2 * x + y
        for w in range(n):
            for k, chip in enumerate(chips):
                pltpu.make_async_remote_copy(
                    src_ref=_window(ins[w], col_sharded[w], q, 0), dst_ref=_window(ins[w], col_sharded[w], q, 0),
                    send_sem=send_sems.at[3 * w + k], recv_sem=recv_sems.at[3 * w + k],
                    device_id=(*chip, c), device_id_type=MESH).start()
        token[...] = jnp.zeros_like(token)

    out = pl.pallas_call(
        body, name=name,
        out_shape=(pltpu.SemaphoreType.DMA((3 * n,)), pltpu.SemaphoreType.DMA((3 * n,)),
                   *[pltpu.HBM(b.shape, b.dtype) for b in bufs], jax.ShapeDtypeStruct((SUBLANE, LANE), F32)),
        in_specs=(HBM,) * n, out_specs=(SEM, SEM) + (HBM,) * n + (pl.BlockSpec(memory_space=pltpu.VMEM),),
        input_output_aliases={w: 2 + w for w in range(n)},
        compiler_params=pltpu.CompilerParams(has_side_effects=EFFECT),
    )(*[_in_hbm(b) for b in bufs])
    return out[0], out[1], list(out[2:2 + n]), out[-1]


def _gather_wait(send_sems, recv_sems, bufs, col_sharded, after, name):
    n = len(bufs)

    def body(*refs):
        ins = refs[:n]
        send_sems, recv_sems = refs[n], refs[n + 1]
        x, y, c, chips = _place()
        q = 2 * x + y
        for w in range(n):
            for k, (cx, cy) in enumerate(chips):
                cp = pltpu.make_async_remote_copy(
                    src_ref=_window(ins[w], col_sharded[w], q, 0),
                    dst_ref=_window(ins[w], col_sharded[w], 2 * cx + cy, 0),
                    send_sem=send_sems.at[3 * w + k], recv_sem=recv_sems.at[3 * w + k],
                    device_id=(cx, cy, c), device_id_type=MESH)
                cp.wait_send()
                cp.wait_recv()

    out = pl.pallas_call(
        body, name=name, out_shape=tuple(pltpu.HBM(b.shape, b.dtype) for b in bufs),
        in_specs=(HBM,) * n + (SEM, SEM) + (ANY,) * len(after), out_specs=(HBM,) * n,
        input_output_aliases={w: w for w in range(n)},
        compiler_params=pltpu.CompilerParams(has_side_effects=EFFECT),
    )(*bufs, send_sems, recv_sems, *after)
    return list(out)


def _scatter_start(grads, col_sharded, name):
    n = len(grads)
    lands = []
    for g, cs in zip(grads, col_sharded):
        K, N = g.shape
        lands.append(lax.empty((3, K, N // N_CHIP) if cs else (3, K // N_CHIP, N), BF16))

    def body(*refs):
        ins, slots = refs[:n], refs[n:2 * n]
        send_sems, recv_sems = refs[2 * n], refs[2 * n + 1]
        token = refs[-1]
        x, y, c, chips = _place()
        for w in range(n):
            for k, (cx, cy) in enumerate(chips):
                pltpu.make_async_remote_copy(
                    src_ref=_window(ins[w], col_sharded[w], 2 * cx + cy, 0), dst_ref=slots[w].at[k],
                    send_sem=send_sems.at[3 * w + k], recv_sem=recv_sems.at[3 * w + k],
                    device_id=(cx, cy, c), device_id_type=MESH).start()
        token[...] = jnp.zeros_like(token)

    out = pl.pallas_call(
        body, name=name,
        out_shape=(pltpu.SemaphoreType.DMA((3 * n,)), pltpu.SemaphoreType.DMA((3 * n,)),
                   *[pltpu.HBM(b.shape, b.dtype) for b in grads], *[pltpu.HBM(b.shape, b.dtype) for b in lands],
                   jax.ShapeDtypeStruct((SUBLANE, LANE), F32)),
        in_specs=(HBM,) * (2 * n),
        out_specs=(SEM, SEM) + (HBM,) * (2 * n) + (pl.BlockSpec(memory_space=pltpu.VMEM),),
        input_output_aliases={w: 2 + w for w in range(2 * n)},
        compiler_params=pltpu.CompilerParams(has_side_effects=EFFECT),
    )(*[_in_hbm(b) for b in grads], *[_in_hbm(b) for b in lands])
    return out[0], out[1], list(out[2:2 + n]), list(out[2 + n:2 + 2 * n]), out[-1]


def _scatter_wait(send_sems, recv_sems, grads, lands, col_sharded, after, name):
    n = len(grads)

    def body(*refs):
        ins, slots = refs[:n], refs[n:2 * n]
        send_sems, recv_sems = refs[2 * n], refs[2 * n + 1]
        x, y, c, chips = _place()
        for w in range(n):
            for k, (cx, cy) in enumerate(chips):
                cp = pltpu.make_async_remote_copy(
                    src_ref=_window(ins[w], col_sharded[w], 2 * cx + cy, 0), dst_ref=slots[w].at[k],
                    send_sem=send_sems.at[3 * w + k], recv_sem=recv_sems.at[3 * w + k],
                    device_id=(cx, cy, c), device_id_type=MESH)
                cp.wait_send()
                cp.wait_recv()

    out = pl.pallas_call(
        body, name=name, out_shape=tuple(pltpu.HBM(b.shape, b.dtype) for b in list(grads) + list(lands)),
        in_specs=(HBM,) * (2 * n) + (SEM, SEM) + (ANY,) * len(after), out_specs=(HBM,) * (2 * n),
        input_output_aliases={w: w for w in range(2 * n)},
        compiler_params=pltpu.CompilerParams(has_side_effects=EFFECT),
    )(*grads, *lands, send_sems, recv_sems, *after)
    return list(out[:n]), list(out[n:])


def _swap_with_sibling(arrays, name):
    n = len(arrays)

    def body(*refs):
        ins, outs = refs[:n], refs[n:2 * n]
        send_sems, recv_sems = refs[2 * n:]
        x, y, c, _ = _place()
        copies = []
        for w in range(n):
            cp = pltpu.make_async_remote_copy(
                src_ref=ins[w], dst_ref=outs[w], send_sem=send_sems.at[w], recv_sem=recv_sems.at[w],
                device_id=(x, y, 1 - c), device_id_type=MESH)
            cp.start()
            copies.append(cp)
        for cp in copies:
            cp.wait()

    return pl.pallas_call(
        body, name=name, out_shape=tuple(jax.ShapeDtypeStruct(a.shape, a.dtype) for a in arrays),
        in_specs=[ANY] * n, out_specs=(ANY,) * n,
        scratch_shapes=[pltpu.SemaphoreType.DMA((n,)), pltpu.SemaphoreType.DMA((n,))],
    )(*arrays)


BIG = ("w_in", "w_pa", "w_pb", "w_pc", "w_o", "w_13", "w_2")
BIG_COL_SHARDED = (True, True, True, True, False, True, False)
SMALL = ("b_mod", "g_mix", "gm_ln_g", "gm_ln_b", "gm_w_s", "gm_b_s", "pool_w", "pool_scale", "conv_w",
         "g_ffn", "g_final")
WEIGHTS = ("w_mod", "b_mod", "g_mix", "w_in", "gm_ln_g", "gm_ln_b", "gm_w_s", "gm_b_s", "w_pa", "pool_w",
           "pool_scale", "w_pb", "conv_w", "w_pc", "w_o", "g_ffn", "w_13", "w_2", "g_final")


def _pack(arrays, width):
    flat = jnp.concatenate([a.reshape(-1) for a in arrays])
    rows = -(-flat.shape[0] // width)
    rows = -(-rows // SUBLANE) * SUBLANE
    flat = jnp.pad(flat, (0, rows * width - flat.shape[0]))
    return flat.reshape(rows, width)


def _unpack(packed, shapes):
    flat = packed.reshape(-1)
    out, off = [], 0
    for s in shapes:
        size = 1
        for d in s:
            size *= d
        out.append(flat[off:off + size].reshape(s))
        off += size
    return out


def kernel(x, c, w_mod, b_mod, g_mix, w_in, gm_ln_g, gm_ln_b, gm_w_s, gm_b_s, w_pa, pool_w, pool_scale, w_pb, conv_w, w_pc, w_o, g_ffn, w_13, w_2, g_final, loss_target, m_w_mod, m_b_mod, m_g_mix, m_w_in, m_gm_ln_g, m_gm_ln_b, m_gm_w_s, m_gm_b_s, m_w_pa, m_pool_w, m_pool_scale, m_w_pb, m_conv_w, m_w_pc, m_w_o, m_g_ffn, m_w_13, m_w_2, m_g_final, v_w_mod, v_b_mod, v_g_mix, v_w_in, v_gm_ln_g, v_gm_ln_b, v_gm_w_s, v_gm_b_s, v_w_pa, v_pool_w, v_pool_scale, v_w_pb, v_conv_w, v_w_pc, v_w_o, v_g_ffn, v_w_13, v_w_2, v_g_final):
    W = dict(w_mod=w_mod, b_mod=b_mod, g_mix=g_mix, w_in=w_in, gm_ln_g=gm_ln_g, gm_ln_b=gm_ln_b, gm_w_s=gm_w_s,
             gm_b_s=gm_b_s, w_pa=w_pa, pool_w=pool_w, pool_scale=pool_scale, w_pb=w_pb, conv_w=conv_w, w_pc=w_pc,
             w_o=w_o, g_ffn=g_ffn, w_13=w_13, w_2=w_2, g_final=g_final)
    Mo = dict(w_mod=m_w_mod, b_mod=m_b_mod, g_mix=m_g_mix, w_in=m_w_in, gm_ln_g=m_gm_ln_g, gm_ln_b=m_gm_ln_b,
              gm_w_s=m_gm_w_s, gm_b_s=m_gm_b_s, w_pa=m_w_pa, pool_w=m_pool_w, pool_scale=m_pool_scale, w_pb=m_w_pb,
              conv_w=m_conv_w, w_pc=m_w_pc, w_o=m_w_o, g_ffn=m_g_ffn, w_13=m_w_13, w_2=m_w_2, g_final=m_g_final)
    Vo = dict(w_mod=v_w_mod, b_mod=v_b_mod, g_mix=v_g_mix, w_in=v_w_in, gm_ln_g=v_gm_ln_g, gm_ln_b=v_gm_ln_b,
              gm_w_s=v_gm_w_s, gm_b_s=v_gm_b_s, w_pa=v_w_pa, pool_w=v_pool_w, pool_scale=v_pool_scale, w_pb=v_w_pb,
              conv_w=v_conv_w, w_pc=v_w_pc, w_o=v_w_o, g_ffn=v_g_ffn, w_13=v_w_13, w_2=v_w_2, g_final=v_g_final)

    B, S, D = x.shape
    T = B * S
    L = w_in.shape[0]
    Bg = B * N_DEV
    N4 = w_mod.shape[2]
    CW = conv_w.shape[2]
    xi, yi, ci = lax.axis_index("x"), lax.axis_index("y"), lax.axis_index("c")
    chip = 2 * xi + yi
    dev = 2 * chip + ci

    head = _pack([c, conv_w], D)
    hrows = head.shape[0]
    got = _allgather8(head, "gather_c_conv").reshape(N_DEV, hrows * D)
    c_all = got[:, :B * D].reshape(Bg, D)
    conv_parts = got[:, B * D:B * D + L * 3 * CW].reshape(N_CHIP, 2, L, 3, CW)[:, 0]
    conv_full = jnp.transpose(conv_parts, (1, 2, 0, 3)).reshape(L, 3, N_CHIP * CW)

    b_cols = lax.dynamic_slice_in_dim(b_mod, chip * N4, N4, axis=1).reshape(L, 1, N4)
    mod_part = _mod_fwd(c_all, w_mod, b_cols, "mod_fwd")
    half = Bg // 2
    mine = lax.dynamic_slice_in_dim(mod_part, ci * half, half, axis=1)
    mod_got = _allgather8(mine.reshape(L * half, N4), "gather_mod").reshape(N_CHIP, 2, L, half, N4)
    mod_full = jnp.transpose(mod_got, (2, 1, 3, 0, 4)).reshape(L, Bg, 6, D)
    mod_mine = lax.dynamic_slice_in_dim(mod_full, dev * B, B, axis=1)
    mod = jnp.pad(mod_mine, ((0, 0), (0, 0), (0, SUBLANE - 6), (0, 0)))

    groups = [(("w_in",), 0), (BIG[1:], 0)] + [(BIG, l) for l in range(1, L)]
    gathers = []
    order = (mod,)
    for gi, (names, l) in enumerate(groups):
        cs = [BIG_COL_SHARDED[BIG.index(n)] for n in names]
        bufs = [_cast_into_full(W[n], l, c_, chip, "cast_" + n, deps=order) for n, c_ in zip(names, cs)]
        ss, rs, bufs, tok = _gather_start(bufs, cs, f"gather_start_{gi}")
        gathers.append((ss, rs, bufs, names, cs))
        order = (tok,)

    def gathered(gi, after):
        ss, rs, bufs, names, cs = gathers[gi]
        return dict(zip(names, _gather_wait(ss, rs, bufs, cs, after, f"gather_wait_{gi}")))

    def mixer_params(l):
        vec = jnp.zeros((SUBLANE, BR_W), F32)
        vec = vec.at[0].set(gm_ln_g[l]).at[1].set(gm_ln_b[l]).at[2].set(pool_scale[l])
        conv = jnp.zeros((SUBLANE, BR_W), F32).at[0:3].set(conv_full[l])
        b_s = jnp.zeros((CHUNK, LANE), F32).at[:, 0:HEADS].set(jnp.transpose(gm_b_s[l]))
        return dict(vec=vec, conv=conv, w_s=gm_w_s[l], b_s=b_s, pool_w=pool_w[l])

    xs = x.reshape(T, D)
    saved = []
    for l in range(L):
        prm = mixer_params(l)
        full = gathered(0, list(order)) if l == 0 else gathered(l + 1, [xs])
        h = _norm_fwd(xs, g_mix[l].reshape(1, D), mod[l], 0, 1, S, "norm_mix_fwd", deps=order if l == 0 else ())
        z = _matmul(h, full["w_in"], mode="nn", name="mm_in")
        cat = _mixer_fwd(z, prm, S, "mixer_fwd")
        if l == 0:
            full.update(gathered(1, [cat]))
        merged = _proj_fwd(cat, z, full["w_pa"], full["w_pb"], full["w_pc"], D, "proj_fwd")
        x1, mo = _matmul(merged, full["w_o"], mode="nn", name="mm_o", resid=(xs, mod[l], 2, S))
        h2 = _norm_fwd(x1, g_ffn[l].reshape(1, D), mod[l], 3, 4, S, "norm_ffn_fwd")
        ab = _matmul(h2, full["w_13"], mode="nn", name="mm_13")
        act = _swiglu_fwd(ab, "swiglu_fwd")
        x2, ffo = _matmul(act, full["w_2"], mode="nn", name="mm_2", resid=(x1, mod[l], 5, S))
        saved.append(dict(prm=prm, full=full, x0=xs, z=z, cat=cat, merged=merged, mo=mo, x1=x1, ab=ab, act=act,
                          ffo=ffo))
        xs = x2

    dx, head_part = _loss_head(xs, loss_target.reshape(T, D), g_final.reshape(1, D), "loss_head")
    loss = lax.psum(head_part[1, 0], ("x", "y", "c"))

    big_grads = {n: [None] * L for n in BIG}
    small_part = {n: [None] * L for n in SMALL if n not in ("b_mod", "g_final")}
    dmod = [None] * L
    scatters = []
    FFN, MIX = ("w_13", "w_2"), ("w_in", "w_pa", "w_pb", "w_pc", "w_o")

    def scatter(names, l, tag):
        cs = [BIG_COL_SHARDED[BIG.index(n)] for n in names]
        ss, rs, gthru, lands, tok = _scatter_start([big_grads[n][l] for n in names], cs, f"scatter_start_{l}{tag}")
        scatters.append((ss, rs, gthru, lands, names, cs, l))
        return (tok,)

    def exchange_small():
        dmod_l = jnp.stack(dmod, axis=0).reshape(L * B, 6 * D)
        rows = -(-(L * B) // SUBLANE) * SUBLANE
        dmod_got = _allgather8(jnp.pad(dmod_l, ((0, rows - L * B), (0, 0))), "gather_dmod")
        local = dict(b_mod=jnp.sum(jnp.stack(dmod, axis=0), axis=1).reshape(L, 6 * D), g_final=head_part[0])
        for n in small_part:
            local[n] = jnp.stack(small_part[n], axis=0)
        packed = _pack([local[n] for n in SMALL], LANE)
        return dmod_got, rows, _allgather8(packed, "gather_small"), [local[n].shape for n in SMALL]

    sent = ()
    for l in reversed(range(L)):
        sv = saved[l]
        full = sv["full"]
        dffo, pg2 = _gate_bwd(dx, sv["ffo"], mod[l], 5, S, "gate_ffn_bwd", deps=sent)
        dact = _matmul(dffo, full["w_2"], mode="nt", name="mm_2_dgrad")
        dab = _swiglu_bwd(dact, sv["ab"], "swiglu_bwd")
        dh2 = _matmul(dab, full["w_13"], mode="nt", name="mm_13_dgrad")
        dx1, h2, pb2, pgf = _norm_bwd(sv["x1"], dh2, dx, g_ffn[l].reshape(1, D), mod[l], 3, 4, S, "norm_ffn_bwd")
        big_grads["w_2"][l] = _matmul(sv["act"], dffo, mode="tn", name="mm_2_wgrad")
        big_grads["w_13"][l] = _matmul(h2, dab, mode="tn", name="mm_13_wgrad")
        sent = scatter(FFN, l, "a")

        dmo, pg1 = _gate_bwd(dx1, sv["mo"], mod[l], 2, S, "gate_mix_bwd", deps=sent)
        dmerged = _matmul(dmo, full["w_o"], mode="nt", name="mm_o_dgrad")
        big_grads["w_o"][l] = _matmul(sv["merged"], dmo, mode="tn", name="mm_o_wgrad")
        dy, dgl, dcat = _proj_bwd(dmerged, sv["cat"], sv["z"], full["w_pa"], full["w_pb"], full["w_pc"], D,
                                  "proj_bwd")
        for k, n in enumerate(("w_pa", "w_pb", "w_pc")):
            big_grads[n][l] = _matmul(sv["cat"], dy, mode="tn", name="mm_proj_wgrad",
                                      a_cols=(k * BR_W, BR_W), b_cols=(k * D, D))
        dz, pv, dws, dbs, dpw = _mixer_bwd(sv["z"], dcat, dgl, sv["prm"], S, "mixer_bwd")
        dh = _matmul(dz, full["w_in"], mode="nt", name="mm_in_dgrad")
        dx0, h, pb1, pgm = _norm_bwd(sv["x0"], dh, dx1, g_mix[l].reshape(1, D), mod[l], 0, 1, S, "norm_mix_bwd")
        dx = dx0

        dmod[l] = jnp.stack([pb1[:, 0], pb1[:, 1], pg1[:, 0], pb2[:, 0], pb2[:, 1], pg2[:, 0]], axis=1)
        small_part["g_mix"][l] = pgm[0]
        small_part["g_ffn"][l] = pgf[0]
        small_part["gm_ln_g"][l] = pv[0]
        small_part["gm_ln_b"][l] = pv[1]
        small_part["pool_scale"][l] = pv[2]
        small_part["conv_w"][l] = pv[3:6]
        small_part["gm_w_s"][l] = dws
        small_part["gm_b_s"][l] = jnp.transpose(dbs[:, 0:HEADS])
        small_part["pool_w"][l] = dpw

        if l == 0:
            dmod_got, dmod_rows, small_got, pshapes = exchange_small()
            last = (dmod_got, small_got)
        else:
            last = ()
        big_grads["w_in"][l] = _matmul(h, dz, mode="tn", name="mm_in_wgrad", deps=last)
        sent = scatter(MIX, l, "b")
    grad_x = dx.reshape(B, S, D)

    results = {}

    dmod_all = dmod_got.reshape(N_DEV, dmod_rows, 6 * D)[:, :L * B].reshape(N_DEV, L, B, 6 * D)
    dmod_all = jnp.transpose(dmod_all, (1, 0, 2, 3)).reshape(L, Bg, 6 * D)
    dmod_cols = lax.dynamic_slice_in_dim(dmod_all, chip * N4, N4, axis=2)
    g_wmod = _mod_wgrad(c_all, dmod_cols, "mod_wgrad", deps=sent)
    res = _adamw(w_mod.reshape(L * D, N4), m_w_mod.reshape(L * D, N4), v_w_mod.reshape(L * D, N4),
                 [g_wmod.reshape(L * D, N4)], "adamw_w_mod")
    results["w_mod"] = [r.reshape(L, D, N4) for r in res]

    gathered_small = small_got.reshape(N_DEV, small_got.shape[0] // N_DEV, LANE)
    g_small = dict(zip(SMALL, _unpack(_sum_devices(gathered_small, "sum_small"), pshapes)))
    g_small["conv_w"] = lax.dynamic_slice_in_dim(g_small["conv_w"], chip * CW, CW, axis=2)
    wshapes = [W[n].shape for n in SMALL]
    res = _adamw(_pack([W[n] for n in SMALL], LANE), _pack([Mo[n] for n in SMALL], LANE),
                 _pack([Vo[n] for n in SMALL], LANE), [_pack([g_small[n] for n in SMALL], LANE)], "adamw_small")
    small_res = [_unpack(r, wshapes) for r in res]
    for i, n in enumerate(SMALL):
        results[n] = [small_res[j][i] for j in range(4)]

    stacks = {n: lax.empty((W[n].shape[0] * W[n].shape[1], W[n].shape[2]), F32) for n in BIG}
    after = [res[0], results["w_mod"][0]]
    for ss, rs, gthru, lands, names, cs, l in scatters:
        gthru, lands = _scatter_wait(ss, rs, gthru, lands, cs, after, f"scatter_wait_{l}_{names[0]}")
        for n, g, ld, c_ in zip(names, gthru, lands, cs):
            stacks[n] = _sum_into(stacks[n], g, ld, l, c_, chip, "sum_" + n)
        after = [stacks[names[-1]]]
    others = _swap_with_sibling([stacks[n] for n in BIG], "swap_sums")
    for n, other in zip(BIG, others):
        _, R, C = W[n].shape
        res = _adamw(W[n].reshape(L * R, C), Mo[n].reshape(L * R, C), Vo[n].reshape(L * R, C), [stacks[n], other],
                     "adamw_" + n)
        results[n] = [r.reshape(L, R, C) for r in res]

    return (loss, grad_x, *[results[n][0] for n in WEIGHTS], *[results[n][1] for n in WEIGHTS],
            *[results[n][2] for n in WEIGHTS], *[results[n][3] for n in WEIGHTS])
```

```python
import jax
import jax.numpy as jnp
from jax import lax
from jax.experimental import pallas as pl
from jax.experimental.pallas import tpu as pltpu

F32 = jnp.float32
BF16 = jnp.bfloat16
MESH = pl.DeviceIdType.MESH

EPS = 1e-6
CHUNK = 128
HEADS = 4
HEAD_DIM = 128
BR_W = 512
N_GROUP = 4
GROUP_DIM = 128
HALO = 16
N_SPLIT = 6 * BR_W
N_CHIP = 4
N_DEV = 8

ADAM_LR = 0.001
ADAM_B1 = 0.9
ADAM_B2 = 0.999
ADAM_EPS = 1e-08
ADAM_WD = 0.01
ADAM_STEP = 10

V7X_VMEM_LIMIT = 56 * 1024 * 1024
LANE = 128
SUBLANE = 8

GELU_K = 0.7978845608028654
GELU_C = 0.044715


def _cparams(sem):
    return pltpu.CompilerParams(dimension_semantics=sem, vmem_limit_bytes=V7X_VMEM_LIMIT)


def _pick(n, cap, q=LANE):
    best = None
    d = q
    while d <= min(n, cap):
        if n % d == 0:
            best = d
        d += q
    return n if best is None else best


def _sigmoid(x):
    return 1.0 / (1.0 + jnp.exp(-x))


def _gelu(x):
    t = jnp.tanh(GELU_K * (x + GELU_C * x * x * x))
    return 0.5 * x * (1.0 + t), t


def _gelu_grad(x, t):
    return 0.5 * (1.0 + t) + 0.5 * x * (1.0 - t * t) * GELU_K * (1.0 + 3.0 * GELU_C * x * x)


def _matmul(a, b, *, mode, name, out_dtype=None, layer=None, a_cols=None, b_cols=None,
            resid=None, deps=(), tm_cap=None, tn_cap=1536, tk_cap=1536):
    out_dtype = BF16 if out_dtype is None else out_dtype
    b2 = b.shape[-2:]
    if tm_cap is None:
        k_len = a.shape[0] if mode == "tn" else a.shape[1]
        if mode == "tn":
            tm_cap, tk_cap = 1536, 2048
        else:
            tm_cap, tk_cap = (512, 8192) if k_len > 1536 else (1024, 1536)
    if mode == "nn":
        M, K = a.shape
        N = b2[1]
    elif mode == "nt":
        M, K = a.shape
        N = b2[0]
    else:
        K = a.shape[0]
        M = a.shape[1] if a_cols is None else a_cols[1]
        N = b2[1] if b_cols is None else b_cols[1]
    tm = _pick(M if resid is None else resid[3], tm_cap)
    tn = _pick(N, tn_cap)
    tk = _pick(K, tk_cap)
    nk = K // tk
    a_off = 0 if a_cols is None else a_cols[0] // tm
    b_off = 0 if b_cols is None else b_cols[0] // tn
    if a_cols is not None:
        assert a_cols[0] % tm == 0
    if b_cols is not None:
        assert b_cols[0] % tn == 0

    if mode == "nn":
        a_spec = pl.BlockSpec((tm, tk), lambda i, j, k: (i, k))
        b_blk, b_idx = (tk, tn), (lambda i, j, k: (k, j))
        dims = (((1,), (0,)), ((), ()))
    elif mode == "nt":
        a_spec = pl.BlockSpec((tm, tk), lambda i, j, k: (i, k))
        b_blk, b_idx = (tn, tk), (lambda i, j, k: (j, k))
        dims = (((1,), (1,)), ((), ()))
    else:
        a_spec = pl.BlockSpec((tk, tm), lambda i, j, k: (k, i + a_off))
        b_blk, b_idx = (tk, tn), (lambda i, j, k: (k, j + b_off))
        dims = (((0,), (0,)), ((), ()))
    if layer is None:
        b_spec = pl.BlockSpec(b_blk, b_idx)
    else:
        b_spec = pl.BlockSpec((None,) + b_blk, lambda i, j, k: (layer,) + b_idx(i, j, k))

    in_specs = [a_spec, b_spec]
    operands = [a, b]
    o_spec = pl.BlockSpec((tm, tn), lambda i, j, k: (i, j))
    if resid is not None:
        x, mod, row, seq = resid
        D = mod.shape[-1]
        in_specs += [o_spec, pl.BlockSpec((1, SUBLANE, tn), lambda i, j, k: ((i * tm) // seq, 0, j))]
        operands += [x, mod]
        out_shape = (jax.ShapeDtypeStruct((M, N), F32), jax.ShapeDtypeStruct((M, N), BF16))
        out_specs = (o_spec, o_spec)
        assert seq % tm == 0 and D == N
    else:
        out_shape = jax.ShapeDtypeStruct((M, N), out_dtype)
        out_specs = o_spec

    def finish(acc, refs):
        if resid is not None:
            x_ref, mod_ref, o_ref, p_ref = refs
            o_ref[...] = x_ref[...] + mod_ref[0, row:row + 1, :] * acc
            p_ref[...] = acc.astype(BF16)
        else:
            (o_ref,) = refs
            o_ref[...] = acc.astype(out_dtype)

    n_in = len(operands) - 2
    in_specs += [ANY] * len(deps)
    operands += list(deps)

    def body(a_ref, b_ref, *refs):
        refs = refs[:n_in] + refs[n_in + len(deps):]
        part = lax.dot_general(a_ref[...], b_ref[...], dims, preferred_element_type=F32)
        if nk == 1:
            finish(part, refs)
            return
        acc_ref = refs[-1]
        k = pl.program_id(2)

        @pl.when(k == 0)
        def _():
            acc_ref[...] = part

        @pl.when(k > 0)
        def _():
            acc_ref[...] += part

        @pl.when(k == nk - 1)
        def _():
            finish(acc_ref[...], refs[:-1])

    scratch = [] if nk == 1 else [pltpu.VMEM((tm, tn), F32)]
    return pl.pallas_call(
        body, name=name, grid=(M // tm, N // tn, nk), in_specs=in_specs, out_specs=out_specs,
        out_shape=out_shape, scratch_shapes=scratch,
        compiler_params=_cparams(("parallel", "parallel", "arbitrary")),
    )(*operands)


def _row_tile(seq, cap):
    return _pick(seq, cap, HALO)


def _mod_spec(tm, seq, D):
    return pl.BlockSpec((1, SUBLANE, D), lambda i: ((i * tm) // seq, 0, 0))


def _normed_matmul(x, g, mod, shift_row, scale_row, w, seq, name, swiglu=False, deps=()):
    T, D = x.shape
    N = w.shape[1] // 2 if swiglu else w.shape[1]
    tm = _pick(seq, 512 if swiglu else 1024)
    tn = _pick(N, 1536)
    nj = N // tn
    n_w = 2 if swiglu else 1

    def body(x_ref, g_ref, mod_ref, *rest):
        w_refs = rest[:n_w]
        outs = rest[n_w + len(deps):-1]
        h_ref = rest[-1]

        @pl.when(pl.program_id(1) == 0)
        def _():
            xv = x_ref[...]
            rstd = lax.rsqrt(jnp.mean(xv * xv, axis=-1, keepdims=True) + EPS)
            n = xv * rstd * g_ref[...]
            h = n * (1.0 + mod_ref[0, scale_row:scale_row + 1, :]) + mod_ref[0, shift_row:shift_row + 1, :]
            h_ref[...] = h.astype(BF16)

        h = h_ref[...]
        if not swiglu:
            outs[0][...] = jnp.dot(h, w_refs[0][...], preferred_element_type=F32).astype(BF16)
            return
        a16 = jnp.dot(h, w_refs[0][...], preferred_element_type=F32).astype(BF16)
        b16 = jnp.dot(h, w_refs[1][...], preferred_element_type=F32).astype(BF16)
        a = a16.astype(F32)
        outs[0][...] = a16
        outs[1][...] = b16
        outs[2][...] = (a * _sigmoid(a) * b16.astype(F32)).astype(BF16)

    w_specs = [pl.BlockSpec((D, tn), lambda i, j: (0, j))]
    if swiglu:
        w_specs.append(pl.BlockSpec((D, tn), lambda i, j: (0, j + nj)))
    o_spec = pl.BlockSpec((tm, tn), lambda i, j: (i, j))
    n_out = 3 if swiglu else 1
    out = pl.pallas_call(
        body, name=name, grid=(T // tm, nj),
        in_specs=[pl.BlockSpec((tm, D), lambda i, j: (i, 0)), pl.BlockSpec((1, D), lambda i, j: (0, 0)),
                  pl.BlockSpec((1, SUBLANE, D), lambda i, j: ((i * tm) // seq, 0, 0))] + w_specs + [ANY] * len(deps),
        out_specs=(o_spec,) * n_out, out_shape=(jax.ShapeDtypeStruct((T, N), BF16),) * n_out,
        scratch_shapes=[pltpu.VMEM((tm, D), BF16)],
        compiler_params=_cparams(("parallel", "arbitrary")),
    )(x, g, mod, *([w] * n_w), *deps)
    return out if swiglu else out[0]


def _norm_bwd(x, dh, dres, g, mod, shift_row, scale_row, seq, name):
    T, D = x.shape
    B = mod.shape[0]
    tm = _row_tile(seq, 512)
    per_seq = seq // tm

    def body(x_ref, dh_ref, dres_ref, g_ref, mod_ref, dx_ref, h_ref, pb_ref, pg_ref):
        i = pl.program_id(0)
        xv = x_ref[...]
        dhv = dh_ref[...].astype(F32)
        gv = g_ref[...]
        scale1 = 1.0 + mod_ref[0, scale_row:scale_row + 1, :]
        rstd = lax.rsqrt(jnp.mean(xv * xv, axis=-1, keepdims=True) + EPS)
        xhat = xv * rstd
        n = xhat * gv
        dn = dhv * scale1
        dxhat = dn * gv
        dx = rstd * (dxhat - xhat * jnp.mean(dxhat * xhat, axis=-1, keepdims=True))
        dx_ref[...] = dres_ref[...] + dx
        h_ref[...] = (n * scale1 + mod_ref[0, shift_row:shift_row + 1, :]).astype(BF16)

        @pl.when(i % per_seq == 0)
        def _():
            pb_ref[...] = jnp.zeros_like(pb_ref)

        @pl.when(i == 0)
        def _():
            pg_ref[...] = jnp.zeros_like(pg_ref)

        pb_ref[0, 0:1, :] += jnp.sum(dhv, axis=0, keepdims=True)
        pb_ref[0, 1:2, :] += jnp.sum(dhv * n, axis=0, keepdims=True)
        pg_ref[0:1, :] += jnp.sum(dn * xhat, axis=0, keepdims=True)

    row = pl.BlockSpec((tm, D), lambda i: (i, 0))
    return pl.pallas_call(
        body, name=name, grid=(T // tm,),
        in_specs=[row, row, row, pl.BlockSpec((1, D), lambda i: (0, 0)), _mod_spec(tm, seq, D)],
        out_specs=(row, row, _mod_spec(tm, seq, D), pl.BlockSpec((SUBLANE, D), lambda i: (0, 0))),
        out_shape=(jax.ShapeDtypeStruct((T, D), F32), jax.ShapeDtypeStruct((T, D), BF16),
                   jax.ShapeDtypeStruct((B, SUBLANE, D), F32), jax.ShapeDtypeStruct((SUBLANE, D), F32)),
        compiler_params=_cparams(("arbitrary",)),
    )(x, dh, dres, g, mod)


def _gated_dgrad(dx, prod, mod, gate_row, w, seq, name, gab=None, deps=()):
    T, D = dx.shape
    B = mod.shape[0]
    N = w.shape[0]
    tm = _pick(seq, 512)
    tn = _pick(N, 1536)
    per_seq = seq // tm
    n_gab = 0 if gab is None else 2

    def body(dx_ref, p_ref, mod_ref, w_ref, *rest):
        gab_refs = rest[:n_gab]
        dp_ref, pb_ref = rest[n_gab + len(deps):n_gab + len(deps) + 2]
        outs = rest[n_gab + len(deps) + 2:-1]
        a_ref = rest[-1]
        i = pl.program_id(0)

        @pl.when(pl.program_id(1) == 0)
        def _():
            dxv = dx_ref[...]
            dp = (dxv * mod_ref[0, gate_row:gate_row + 1, :]).astype(BF16)
            a_ref[...] = dp
            dp_ref[...] = dp

            @pl.when(i % per_seq == 0)
            def _():
                pb_ref[...] = jnp.zeros_like(pb_ref)

            pb_ref[0, 0:1, :] += jnp.sum(dxv * p_ref[...].astype(F32), axis=0, keepdims=True)

        du = lax.dot_general(a_ref[...], w_ref[...], (((1,), (1,)), ((), ())), preferred_element_type=F32)
        if gab is None:
            outs[0][...] = du.astype(BF16)
            return
        a = gab_refs[0][...].astype(F32)
        b = gab_refs[1][...].astype(F32)
        sg = _sigmoid(a)
        outs[0][...] = (du * b * sg * (1.0 + a * (1.0 - sg))).astype(BF16)
        outs[1][...] = (du * a * sg).astype(BF16)

    row = pl.BlockSpec((tm, D), lambda i, j: (i, 0))
    tile = pl.BlockSpec((tm, tn), lambda i, j: (i, j))
    mod_spec = pl.BlockSpec((1, SUBLANE, D), lambda i, j: ((i * tm) // seq, 0, 0))
    n_out = 1 if gab is None else 2
    out = pl.pallas_call(
        body, name=name, grid=(T // tm, N // tn),
        in_specs=[row, row, mod_spec, pl.BlockSpec((tn, D), lambda i, j: (j, 0))] + [tile] * n_gab
        + [ANY] * len(deps),
        out_specs=(row, mod_spec) + (tile,) * n_out,
        out_shape=(jax.ShapeDtypeStruct((T, D), BF16), jax.ShapeDtypeStruct((B, SUBLANE, D), F32))
        + (jax.ShapeDtypeStruct((T, N), BF16),) * n_out,
        scratch_shapes=[pltpu.VMEM((tm, D), BF16)],
        compiler_params=_cparams(("arbitrary", "arbitrary")),
    )(dx, prod, mod, w, *(gab or ()), *deps)
    return out


def _dgrad_pair(da, db, w, name):
    T, Fh = da.shape
    D = w.shape[0]
    tm = _pick(T, 512)

    def body(da_ref, db_ref, wa_ref, wb_ref, o_ref):
        dims = (((1,), (1,)), ((), ()))
        acc = lax.dot_general(da_ref[...], wa_ref[...], dims, preferred_element_type=F32)
        acc = acc + lax.dot_general(db_ref[...], wb_ref[...], dims, preferred_element_type=F32)
        o_ref[...] = acc.astype(BF16)

    row = pl.BlockSpec((tm, Fh), lambda i: (i, 0))
    return pl.pallas_call(
        body, name=name, grid=(T // tm,),
        in_specs=[row, row, pl.BlockSpec((D, Fh), lambda i: (0, 0)), pl.BlockSpec((D, Fh), lambda i: (0, 1))],
        out_specs=pl.BlockSpec((tm, D), lambda i: (i, 0)), out_shape=jax.ShapeDtypeStruct((T, D), BF16),
        compiler_params=_cparams(("parallel",)),
    )(da, db, w, w)


def _wgrad_pair(h, da, db, name):
    T, D = h.shape
    Fh = da.shape[1]
    tn = _pick(Fh, 1536)
    tk = _pick(T, 2048)
    half = Fh // tn
    nk = T // tk

    def body(h_ref, da_ref, db_ref, o_ref, acc_ref):
        j, k = pl.program_id(0), pl.program_id(1)
        dims = (((0,), (0,)), ((), ()))

        def accumulate(g_ref):
            part = lax.dot_general(h_ref[...], g_ref[...], dims, preferred_element_type=F32)

            @pl.when(k == 0)
            def _():
                acc_ref[...] = part

            @pl.when(k > 0)
            def _():
                acc_ref[...] += part

        @pl.when(j < half)
        def _():
            accumulate(da_ref)

        @pl.when(j >= half)
        def _():
            accumulate(db_ref)

        @pl.when(k == nk - 1)
        def _():
            o_ref[...] = acc_ref[...].astype(BF16)

    a_idx = lambda j, k: (jnp.where(j < half, k, 0), jnp.minimum(j, half - 1))
    b_idx = lambda j, k: (jnp.where(j >= half, k, 0), jnp.maximum(j - half, 0))
    return pl.pallas_call(
        body, name=name, grid=(2 * half, nk),
        in_specs=[pl.BlockSpec((tk, D), lambda j, k: (k, 0)), pl.BlockSpec((tk, tn), a_idx),
                  pl.BlockSpec((tk, tn), b_idx)],
        out_specs=pl.BlockSpec((D, tn), lambda j, k: (0, j)),
        out_shape=jax.ShapeDtypeStruct((D, 2 * Fh), BF16), scratch_shapes=[pltpu.VMEM((D, tn), F32)],
        compiler_params=_cparams(("parallel", "arbitrary")),
    )(h, da, db)


def _shift_down(v, d):
    return pltpu.roll(v, d, 0)


def _shift_up(v, d):
    return pltpu.roll(v, v.shape[0] - d, 0)


def _tril_mask():
    r = lax.broadcasted_iota(jnp.int32, (CHUNK, CHUNK), 0)
    c = lax.broadcasted_iota(jnp.int32, (CHUNK, CHUNK), 1)
    return c <= r


def _pool_counts(i, tm, seq, rows, first_row):
    r = lax.broadcasted_iota(jnp.int32, (rows, 1), 0) + (i * tm + first_row)
    pos1 = (r % seq + 1).astype(F32)
    return [jnp.minimum(pos1, float(2 << g)) for g in range(N_GROUP)]


def _mixer_forward_values(zt, hxb, hch, i, tm, seq, ln_g, ln_b, ws_ref, bs_ref, pw_ref, pscale, cw_ref):
    u = zt[:, 0 * BR_W:1 * BR_W]
    v = zt[:, 1 * BR_W:2 * BR_W]
    xb = zt[:, 2 * BR_W:3 * BR_W]
    bg = zt[:, 3 * BR_W:4 * BR_W]
    cg = zt[:, 4 * BR_W:5 * BR_W]
    hc = zt[:, 5 * BR_W:6 * BR_W]
    out = {}

    ug, tu = _gelu(u)
    vg, tv = _gelu(v)
    mu = jnp.mean(vg, axis=-1, keepdims=True)
    vc = vg - mu
    rstd = lax.rsqrt(jnp.mean(vc * vc, axis=-1, keepdims=True) + EPS)
    vhat = vc * rstd
    vn = (vhat * ln_g + ln_b).astype(BF16)
    mask = _tril_mask()
    wt = [jnp.where(mask, ws_ref[h], 0.0).astype(BF16) for h in range(HEADS)]
    rows = []
    for n in range(tm // CHUNK):
        blocks = []
        for h in range(HEADS):
            blk = vn[n * CHUNK:(n + 1) * CHUNK, h * HEAD_DIM:(h + 1) * HEAD_DIM]
            sb = jnp.dot(wt[h], blk, preferred_element_type=F32) + bs_ref[:, h:h + 1]
            blocks.append(sb)
        rows.append(jnp.concatenate(blocks, axis=1))
    s = jnp.concatenate(rows, axis=0) if len(rows) > 1 else rows[0]
    out.update(u=u, v=v, ug=ug, tu=tu, tv=tv, rstd=rstd, vhat=vhat, vn=vn, wt=wt, s=s, a_out=ug * s)

    ext = jnp.concatenate([hxb, xb], axis=0)
    cnt = _pool_counts(i, tm, seq, tm, 0)
    p, qs = [], []
    for g in range(N_GROUP):
        e = ext[:, g * GROUP_DIM:(g + 1) * GROUP_DIM]
        acc = e
        for d in (1, 2, 4, 8)[:g + 1]:
            acc = acc + _shift_down(acc, d)
        pg = acc[HALO:, :] / cnt[g] - xb[:, g * GROUP_DIM:(g + 1) * GROUP_DIM]
        p.append(pg.astype(BF16))
        qs.append(jnp.dot(p[g], pw_ref[g].astype(BF16), preferred_element_type=F32))
    q = jnp.concatenate(qs, axis=1)
    out.update(p=p, q=q, b_out=q * pscale)

    zc = cg * hc
    zce = jnp.concatenate([hch[:, :BR_W] * hch[:, BR_W:], zc], axis=0)
    z1 = _shift_down(zce, 1)[HALO:, :]
    z2 = _shift_down(zce, 2)[HALO:, :]
    y = cw_ref[0:1, :] * z2 + cw_ref[1:2, :] * z1 + cw_ref[2:3, :] * zc
    out.update(bg=bg, cg=cg, hc=hc, zc=zc, z1=z1, z2=z2, y=y, c_out=bg * y)
    return out


def _mixer_specs(tm, T):
    nb = T // HALO
    per = tm // HALO
    prev = lambda i: jnp.maximum(i * per - 1, 0)
    nxt = lambda i: jnp.minimum((i + 1) * per, nb - 1)
    return prev, nxt


def _mixer_fwd(z, prm, seq, name):
    T = z.shape[0]
    tm = _row_tile(seq, 256)
    per_seq = seq // tm
    prev, _ = _mixer_specs(tm, T)

    def body(z_ref, hxb_ref, hch_ref, vec_ref, cw_ref, ws_ref, bs_ref, pw_ref, cat_ref):
        i = pl.program_id(0)
        keep = jnp.where(i % per_seq == 0, 0.0, 1.0)
        zt = z_ref[...].astype(F32)
        hxb = hxb_ref[...].astype(F32) * keep
        hch = hch_ref[...].astype(F32) * keep
        o = _mixer_forward_values(zt, hxb, hch, i, tm, seq, vec_ref[0:1, :], vec_ref[1:2, :],
                                  ws_ref, bs_ref, pw_ref, vec_ref[2:3, :], cw_ref)
        cat_ref[:, 0 * BR_W:1 * BR_W] = o["a_out"].astype(BF16)
        cat_ref[:, 1 * BR_W:2 * BR_W] = o["b_out"].astype(BF16)
        cat_ref[:, 2 * BR_W:3 * BR_W] = o["c_out"].astype(BF16)

    full = lambda shape: pl.BlockSpec(shape, lambda i: (0,) * len(shape))
    return pl.pallas_call(
        body, name=name, grid=(T // tm,),
        in_specs=[pl.BlockSpec((tm, N_SPLIT), lambda i: (i, 0)),
                  pl.BlockSpec((HALO, BR_W), lambda i: (prev(i), 2)),
                  pl.BlockSpec((HALO, 2 * BR_W), lambda i: (prev(i), 2)),
                  full((SUBLANE, BR_W)), full((SUBLANE, BR_W)), full((HEADS, CHUNK, CHUNK)),
                  full((CHUNK, LANE)), full((N_GROUP, GROUP_DIM, GROUP_DIM))],
        out_specs=pl.BlockSpec((tm, 3 * BR_W), lambda i: (i, 0)),
        out_shape=jax.ShapeDtypeStruct((T, 3 * BR_W), BF16),
        compiler_params=_cparams(("parallel",)),
    )(z, z, z, prm["vec"], prm["conv"], prm["w_s"], prm["b_s"], prm["pool_w"])


def _mixer_bwd(z, dcat, dgl, prm, seq, name):
    T, IN = z.shape
    GL = IN - N_SPLIT
    tm = _row_tile(seq, 256)
    per_seq = seq // tm
    prev, nxt = _mixer_specs(tm, T)
    nrow = tm + HALO

    def body(z_ref, hxb_ref, hch_ref, nbg_ref, dcat_ref, ndb_ref, ndc_ref, dgl_ref,
             vec_ref, cw_ref, ws_ref, bs_ref, pw_ref,
             dz_ref, pv_ref, dws_ref, dbs_ref, dpw_ref):
        i = pl.program_id(0)
        keep_prev = jnp.where(i % per_seq == 0, 0.0, 1.0)
        keep_next = jnp.where(i % per_seq == per_seq - 1, 0.0, 1.0)
        zt = z_ref[...].astype(F32)
        hxb = hxb_ref[...].astype(F32) * keep_prev
        hch = hch_ref[...].astype(F32) * keep_prev
        ln_g = vec_ref[0:1, :]
        pscale = vec_ref[2:3, :]
        o = _mixer_forward_values(zt, hxb, hch, i, tm, seq, ln_g, vec_ref[1:2, :],
                                  ws_ref, bs_ref, pw_ref, pscale, cw_ref)
        dcv = dcat_ref[...].astype(F32)
        da = dcv[:, 0 * BR_W:1 * BR_W]
        db = dcv[:, 1 * BR_W:2 * BR_W]
        dc = dcv[:, 2 * BR_W:3 * BR_W]
        mask = _tril_mask()

        @pl.when(i == 0)
        def _():
            pv_ref[...] = jnp.zeros_like(pv_ref)
            dws_ref[...] = jnp.zeros_like(dws_ref)
            dbs_ref[...] = jnp.zeros_like(dbs_ref)
            dpw_ref[...] = jnp.zeros_like(dpw_ref)

        d_ug = da * o["s"]
        ds = da * o["ug"]
        ds_b = ds.astype(BF16)
        vn = o["vn"]
        dvn_rows = []
        dws = [jnp.zeros((CHUNK, CHUNK), F32) for _ in range(HEADS)]
        dsum = jnp.zeros((CHUNK, BR_W), F32)
        for n in range(tm // CHUNK):
            blocks = []
            rs = slice(n * CHUNK, (n + 1) * CHUNK)
            dsum = dsum + ds[rs, :]
            for h in range(HEADS):
                cs = slice(h * HEAD_DIM, (h + 1) * HEAD_DIM)
                dsb = ds_b[rs, cs]
                blocks.append(lax.dot_general(o["wt"][h], dsb, (((0,), (0,)), ((), ())),
                                              preferred_element_type=F32))
                dws[h] = dws[h] + lax.dot_general(dsb, vn[rs, cs], (((1,), (1,)), ((), ())),
                                                  preferred_element_type=F32)
            dvn_rows.append(jnp.concatenate(blocks, axis=1))
        dvn = jnp.concatenate(dvn_rows, axis=0) if len(dvn_rows) > 1 else dvn_rows[0]
        lane = lax.broadcasted_iota(jnp.int32, (CHUNK, LANE), 1)
        dbs_t = jnp.zeros((CHUNK, LANE), F32)
        for h in range(HEADS):
            dws_ref[h] += jnp.where(mask, dws[h], 0.0)
            rsum = jnp.sum(dsum[:, h * HEAD_DIM:(h + 1) * HEAD_DIM], axis=1, keepdims=True)
            dbs_t = dbs_t + jnp.where(lane == h, rsum, 0.0)
        dbs_ref[...] += dbs_t
        vhat = o["vhat"]
        pv_ref[0:1, :] += jnp.sum(dvn * vhat, axis=0, keepdims=True)
        pv_ref[1:2, :] += jnp.sum(dvn, axis=0, keepdims=True)
        dvhat = dvn * ln_g
        dvg = o["rstd"] * (dvhat - jnp.mean(dvhat, axis=-1, keepdims=True)
                           - vhat * jnp.mean(dvhat * vhat, axis=-1, keepdims=True))
        du = d_ug * _gelu_grad(o["u"], o["tu"])
        dv = dvg * _gelu_grad(o["v"], o["tv"])

        pv_ref[2:3, :] += jnp.sum(db * o["q"], axis=0, keepdims=True)
        dq = (db * pscale).astype(BF16)
        dqn = (ndb_ref[...].astype(F32) * pscale * keep_next).astype(BF16)
        cnt = _pool_counts(i, tm, seq, nrow, 0)
        dxb = []
        for g in range(N_GROUP):
            cs = slice(g * GROUP_DIM, (g + 1) * GROUP_DIM)
            pwg = pw_ref[g].astype(BF16)
            dpw_ref[g] += lax.dot_general(o["p"][g], dq[:, cs], (((0,), (0,)), ((), ())),
                                          preferred_element_type=F32)
            dp = lax.dot_general(dq[:, cs], pwg, (((1,), (1,)), ((), ())), preferred_element_type=F32)
            dpn = lax.dot_general(dqn[:, cs], pwg, (((1,), (1,)), ((), ())), preferred_element_type=F32)
            acc = jnp.concatenate([dp, dpn], axis=0) / cnt[g]
            for d in (1, 2, 4, 8)[:g + 1]:
                acc = acc + _shift_up(acc, d)
            dxb.append(acc[:tm, :] - dp)
        dxb = jnp.concatenate(dxb, axis=1)

        dbg = dc * o["y"]
        dy = dc * o["bg"]
        pv_ref[3:4, :] += jnp.sum(dy * o["z2"], axis=0, keepdims=True)
        pv_ref[4:5, :] += jnp.sum(dy * o["z1"], axis=0, keepdims=True)
        pv_ref[5:6, :] += jnp.sum(dy * o["zc"], axis=0, keepdims=True)
        dyn = ndc_ref[...].astype(F32) * nbg_ref[...].astype(F32) * keep_next
        dye = jnp.concatenate([dy, dyn], axis=0)
        dzc = (cw_ref[2:3, :] * dy + cw_ref[1:2, :] * _shift_up(dye, 1)[:tm, :]
               + cw_ref[0:1, :] * _shift_up(dye, 2)[:tm, :])
        dcg = dzc * o["hc"]
        dhc = dzc * o["cg"]

        dz_ref[:, 0 * BR_W:1 * BR_W] = du.astype(BF16)
        dz_ref[:, 1 * BR_W:2 * BR_W] = dv.astype(BF16)
        dz_ref[:, 2 * BR_W:3 * BR_W] = dxb.astype(BF16)
        dz_ref[:, 3 * BR_W:4 * BR_W] = dbg.astype(BF16)
        dz_ref[:, 4 * BR_W:5 * BR_W] = dcg.astype(BF16)
        dz_ref[:, 5 * BR_W:6 * BR_W] = dhc.astype(BF16)
        dz_ref[:, N_SPLIT:] = dgl_ref[...]

    full = lambda shape: pl.BlockSpec(shape, lambda i: (0,) * len(shape))
    return pl.pallas_call(
        body, name=name, grid=(T // tm,),
        in_specs=[pl.BlockSpec((tm, N_SPLIT), lambda i: (i, 0)),
                  pl.BlockSpec((HALO, BR_W), lambda i: (prev(i), 2)),
                  pl.BlockSpec((HALO, 2 * BR_W), lambda i: (prev(i), 2)),
                  pl.BlockSpec((HALO, BR_W), lambda i: (nxt(i), 3)),
                  pl.BlockSpec((tm, 3 * BR_W), lambda i: (i, 0)),
                  pl.BlockSpec((HALO, BR_W), lambda i: (nxt(i), 1)),
                  pl.BlockSpec((HALO, BR_W), lambda i: (nxt(i), 2)),
                  pl.BlockSpec((tm, GL), lambda i: (i, 0)),
                  full((SUBLANE, BR_W)), full((SUBLANE, BR_W)), full((HEADS, CHUNK, CHUNK)),
                  full((CHUNK, LANE)), full((N_GROUP, GROUP_DIM, GROUP_DIM))],
        out_specs=(pl.BlockSpec((tm, IN), lambda i: (i, 0)),
                   full((SUBLANE, BR_W)), full((HEADS, CHUNK, CHUNK)), full((CHUNK, LANE)),
                   full((N_GROUP, GROUP_DIM, GROUP_DIM))),
        out_shape=(jax.ShapeDtypeStruct((T, IN), BF16),
                   jax.ShapeDtypeStruct((SUBLANE, BR_W), F32),
                   jax.ShapeDtypeStruct((HEADS, CHUNK, CHUNK), F32),
                   jax.ShapeDtypeStruct((CHUNK, LANE), F32),
                   jax.ShapeDtypeStruct((N_GROUP, GROUP_DIM, GROUP_DIM), F32)),
        compiler_params=_cparams(("arbitrary",)),
    )(z, z, z, z, dcat, dcat, dcat, dgl, prm["vec"], prm["conv"], prm["w_s"], prm["b_s"], prm["pool_w"])


def _proj_fwd(cat, z, w_pa, w_pb, w_pc, D, name):
    T, IN = z.shape
    GL = 3 * D
    assert N_SPLIT % GL == 0
    glb = N_SPLIT // GL
    tm = _pick(T, 256, HALO)

    def body(cat_ref, gl_ref, wa_ref, wb_ref, wc_ref, m_ref):
        acc = jnp.zeros((tm, D), F32)
        for k, w_ref in enumerate((wa_ref, wb_ref, wc_ref)):
            y = jnp.dot(cat_ref[:, k * BR_W:(k + 1) * BR_W], w_ref[...], preferred_element_type=F32)
            acc = acc + _sigmoid(gl_ref[:, k * D:(k + 1) * D].astype(F32)) * y
        m_ref[...] = acc.astype(BF16)

    wspec = pl.BlockSpec((BR_W, D), lambda i: (0, 0))
    return pl.pallas_call(
        body, name=name, grid=(T // tm,),
        in_specs=[pl.BlockSpec((tm, 3 * BR_W), lambda i: (i, 0)),
                  pl.BlockSpec((tm, GL), lambda i: (i, glb)), wspec, wspec, wspec],
        out_specs=pl.BlockSpec((tm, D), lambda i: (i, 0)),
        out_shape=jax.ShapeDtypeStruct((T, D), BF16),
        compiler_params=_cparams(("parallel",)),
    )(cat, z, w_pa, w_pb, w_pc)


def _proj_bwd(dmerged, cat, z, w_pa, w_pb, w_pc, D, name):
    T, IN = z.shape
    GL = 3 * D
    glb = N_SPLIT // GL
    tm = _pick(T, 256, HALO)

    def body(dm_ref, cat_ref, gl_ref, wa_ref, wb_ref, wc_ref, dy_ref, dgl_ref, dcat_ref):
        dm = dm_ref[...].astype(F32)
        for k, w_ref in enumerate((wa_ref, wb_ref, wc_ref)):
            w = w_ref[...]
            y = jnp.dot(cat_ref[:, k * BR_W:(k + 1) * BR_W], w, preferred_element_type=F32)
            sg = _sigmoid(gl_ref[:, k * D:(k + 1) * D].astype(F32))
            dyk = (dm * sg).astype(BF16)
            dy_ref[:, k * D:(k + 1) * D] = dyk
            dgl_ref[:, k * D:(k + 1) * D] = (dm * y * sg * (1.0 - sg)).astype(BF16)
            dcat_ref[:, k * BR_W:(k + 1) * BR_W] = lax.dot_general(
                dyk, w, (((1,), (1,)), ((), ())), preferred_element_type=F32).astype(BF16)

    wspec = pl.BlockSpec((BR_W, D), lambda i: (0, 0))
    return pl.pallas_call(
        body, name=name, grid=(T // tm,),
        in_specs=[pl.BlockSpec((tm, D), lambda i: (i, 0)),
                  pl.BlockSpec((tm, 3 * BR_W), lambda i: (i, 0)),
                  pl.BlockSpec((tm, GL), lambda i: (i, glb)), wspec, wspec, wspec],
        out_specs=(pl.BlockSpec((tm, GL), lambda i: (i, 0)), pl.BlockSpec((tm, GL), lambda i: (i, 0)),
                   pl.BlockSpec((tm, 3 * BR_W), lambda i: (i, 0))),
        out_shape=(jax.ShapeDtypeStruct((T, GL), BF16), jax.ShapeDtypeStruct((T, GL), BF16),
                   jax.ShapeDtypeStruct((T, 3 * BR_W), BF16)),
        compiler_params=_cparams(("parallel",)),
    )(dmerged, cat, z, w_pa, w_pb, w_pc)


def _loss_head(x, target, g, name):
    T, D = x.shape
    tm = _pick(T, 512, SUBLANE)

    def body(x_ref, t_ref, g_ref, dx_ref, part_ref):
        i = pl.program_id(0)
        xv = x_ref[...]
        gv = g_ref[...]
        rstd = lax.rsqrt(jnp.mean(xv * xv, axis=-1, keepdims=True) + EPS)
        xhat = xv * rstd
        err = xhat * gv - t_ref[...]
        dy = err * (1.0 / D)
        dxhat = dy * gv
        dx_ref[...] = rstd * (dxhat - xhat * jnp.mean(dxhat * xhat, axis=-1, keepdims=True))

        @pl.when(i == 0)
        def _():
            part_ref[...] = jnp.zeros_like(part_ref)

        part_ref[0:1, :] += jnp.sum(dy * xhat, axis=0, keepdims=True)
        part_ref[1:2, :] += jnp.zeros((1, D), F32) + (0.5 / D) * jnp.sum(err * err)

    row = pl.BlockSpec((tm, D), lambda i: (i, 0))
    return pl.pallas_call(
        body, name=name, grid=(T // tm,),
        in_specs=[row, row, pl.BlockSpec((1, D), lambda i: (0, 0))],
        out_specs=(row, pl.BlockSpec((SUBLANE, D), lambda i: (0, 0))),
        out_shape=(jax.ShapeDtypeStruct((T, D), F32), jax.ShapeDtypeStruct((SUBLANE, D), F32)),
        compiler_params=_cparams(("arbitrary",)),
    )(x, target, g)


def _mod_fwd(c_all, w_mod, b_cols, name):
    L, D, N4 = w_mod.shape
    Bg = c_all.shape[0]
    tn = _pick(N4, 768)

    def body(c_ref, w_ref, b_ref, o_ref):
        cv = c_ref[...]
        ca = (cv * _sigmoid(cv)).astype(BF16)
        o_ref[...] = jnp.dot(ca, w_ref[...].astype(BF16), preferred_element_type=F32) + b_ref[...]

    return pl.pallas_call(
        body, name=name, grid=(L, N4 // tn),
        in_specs=[pl.BlockSpec((Bg, D), lambda l, j: (0, 0)),
                  pl.BlockSpec((None, D, tn), lambda l, j: (l, 0, j)),
                  pl.BlockSpec((None, 1, tn), lambda l, j: (l, 0, j))],
        out_specs=pl.BlockSpec((None, Bg, tn), lambda l, j: (l, 0, j)),
        out_shape=jax.ShapeDtypeStruct((L, Bg, N4), F32),
        compiler_params=_cparams(("parallel", "parallel")),
    )(c_all, w_mod, b_cols)


def _mod_wgrad(c_all, dmod_cols, name, deps=()):
    L, Bg, N4 = dmod_cols.shape
    D = c_all.shape[1]
    tn = _pick(N4, 768)

    def body(c_ref, d_ref, *rest):
        cv = c_ref[...]
        ca = (cv * _sigmoid(cv)).astype(BF16)
        rest[-1][...] = lax.dot_general(ca, d_ref[...].astype(BF16), (((0,), (0,)), ((), ())),
                                     preferred_element_type=F32)

    return pl.pallas_call(
        body, name=name, grid=(L, N4 // tn),
        in_specs=[pl.BlockSpec((Bg, D), lambda l, j: (0, 0)),
                  pl.BlockSpec((None, Bg, tn), lambda l, j: (l, 0, j))] + [ANY] * len(deps),
        out_specs=pl.BlockSpec((None, D, tn), lambda l, j: (l, 0, j)),
        out_shape=jax.ShapeDtypeStruct((L, D, N4), F32),
        compiler_params=_cparams(("parallel", "parallel")),
    )(c_all, dmod_cols, *deps)


def _rows_tile(R, C):
    return _pick(R, max(SUBLANE, (256 * 1024) // C), SUBLANE)


def _cast_into_full(w, layer, col_sharded, chip, name, deps=()):
    L, R, C = w.shape
    K, N = (R, C * N_CHIP) if col_sharded else (R * N_CHIP, C)
    tr = _pick(R, max(HALO, (512 * 1024) // C), HALO)
    nb = R // tr

    def body(q_ref, w_ref, *rest):
        rest[-1][...] = w_ref[...].astype(BF16)

    out_idx = (lambda i, q: (i, q[0])) if col_sharded else (lambda i, q: (q[0] * nb + i, 0))
    grid_spec = pltpu.PrefetchScalarGridSpec(
        num_scalar_prefetch=1, grid=(nb,),
        in_specs=[pl.BlockSpec((None, tr, C), lambda i, q: (layer, i, 0))] + [ANY] * len(deps),
        out_specs=pl.BlockSpec((tr, C), out_idx))
    return pl.pallas_call(
        body, name=name, grid_spec=grid_spec, out_shape=jax.ShapeDtypeStruct((K, N), BF16),
        compiler_params=_cparams(("arbitrary",)),
    )(chip.reshape(1).astype(jnp.int32), w, *deps)


def _sum_into(stack, grad, slots, layer, col_sharded, chip, name):
    _, R, C = slots.shape
    tr = _rows_tile(R, C)
    nb = R // tr

    def body(q_ref, stack_ref, g_ref, s_ref, o_ref):
        o_ref[...] = ((g_ref[...].astype(F32) + s_ref[0].astype(F32)) + s_ref[1].astype(F32)) + s_ref[2].astype(F32)

    g_idx = (lambda i, q: (i, q[0])) if col_sharded else (lambda i, q: (q[0] * nb + i, 0))
    grid_spec = pltpu.PrefetchScalarGridSpec(
        num_scalar_prefetch=1, grid=(nb,),
        in_specs=[ANY, pl.BlockSpec((tr, C), g_idx), pl.BlockSpec((3, tr, C), lambda i, q: (0, i, 0))],
        out_specs=pl.BlockSpec((tr, C), lambda i, q: (layer * nb + i, 0)))
    return pl.pallas_call(
        body, name=name, grid_spec=grid_spec, out_shape=jax.ShapeDtypeStruct(stack.shape, F32),
        input_output_aliases={1: 0}, compiler_params=_cparams(("arbitrary",)),
    )(chip.reshape(1).astype(jnp.int32), stack, grad, slots)


def _sum_devices(parts, name):
    n, R, C = parts.shape
    tr = _rows_tile(R, C)

    def body(s_ref, o_ref):
        acc = s_ref[0]
        for d in range(1, n):
            acc = acc + s_ref[d]
        o_ref[...] = acc

    return pl.pallas_call(
        body, name=name, grid=(R // tr,),
        in_specs=[pl.BlockSpec((n, tr, C), lambda i: (0, i, 0))],
        out_specs=pl.BlockSpec((tr, C), lambda i: (i, 0)),
        out_shape=jax.ShapeDtypeStruct((R, C), F32), compiler_params=_cparams(("parallel",)),
    )(parts)


def _adamw(w, m, v, grads, name):
    R, C = w.shape
    tr = _rows_tile(R, C)
    bc1 = 1.0 - ADAM_B1 ** ADAM_STEP
    bc2 = 1.0 - ADAM_B2 ** ADAM_STEP
    ng = len(grads)

    def body(*refs):
        w_ref, m_ref, v_ref = refs[:3]
        g_refs = refs[3:3 + ng]
        g_out, d_out, m_out, v_out = refs[3 + ng:]
        g = g_refs[0][...]
        for r in g_refs[1:]:
            g = g + r[...]
        mn = ADAM_B1 * m_ref[...] + (1.0 - ADAM_B1) * g
        vn = ADAM_B2 * v_ref[...] + (1.0 - ADAM_B2) * (g * g)
        m_hat = mn / bc1
        v_hat = vn / bc2
        g_out[...] = g
        d_out[...] = -ADAM_LR * (m_hat / (jnp.sqrt(v_hat) + ADAM_EPS) + ADAM_WD * w_ref[...])
        m_out[...] = mn
        v_out[...] = vn

    spec = pl.BlockSpec((tr, C), lambda i: (i, 0))
    sds = jax.ShapeDtypeStruct((R, C), F32)
    return pl.pallas_call(
        body, name=name, grid=(R // tr,), in_specs=[spec] * (3 + ng), out_specs=(spec,) * 4,
        out_shape=(sds,) * 4, compiler_params=_cparams(("parallel",)),
    )(w, m, v, *grads)


def _place():
    x, y, c = lax.axis_index("x"), lax.axis_index("y"), lax.axis_index("c")
    chips = [(1 - x, y), (x, 1 - y), (1 - x, 1 - y)]
    return x, y, c, chips


ANY = pl.BlockSpec(memory_space=pl.ANY)


def _allgather8(v, name):
    m_per, n = v.shape

    def body(x_ref, out_ref, send_sems, recv_sems, local_sem):
        x, y, c, chips = _place()
        me, sibling = (x, y, c), (x, y, 1 - c)

        def rows(px, py, pc):
            return out_ref.at[pl.ds((4 * px + 2 * py + pc) * m_per, m_per), :]

        def copy(k, block, to, src=None):
            return pltpu.make_async_remote_copy(
                src_ref=rows(*block) if src is None else src, dst_ref=rows(*block),
                send_sem=send_sems.at[k], recv_sem=recv_sems.at[k], device_id=to, device_id_type=MESH)

        mine = pltpu.make_async_copy(x_ref, rows(*me), local_sem)
        mine.start()
        first = [copy(0, me, sibling, src=x_ref)]
        first += [copy(1 + j, me, (*chip, c), src=x_ref) for j, chip in enumerate(chips)]
        for cp in first:
            cp.start()
        passed = [copy(4 + j, (*chip, c), sibling) for j, chip in enumerate(chips)]
        for j, chip in enumerate(chips):
            copy(1 + j, (*chip, c), me).wait_recv()
            passed[j].start()
        copy(0, sibling, me).wait_recv()
        for j, chip in enumerate(chips):
            copy(4 + j, (*chip, 1 - c), me).wait_recv()
        for cp in first + passed:
            cp.wait_send()
        mine.wait()

    return pl.pallas_call(
        body, name=name, out_shape=jax.ShapeDtypeStruct((N_DEV * m_per, n), v.dtype),
        in_specs=[ANY], out_specs=ANY,
        scratch_shapes=[pltpu.SemaphoreType.DMA((7,)), pltpu.SemaphoreType.DMA((7,)), pltpu.SemaphoreType.DMA],
    )(v)


def _window(ref, col_sharded, q, lead):
    full = (slice(None),) * lead
    if col_sharded:
        width = ref.shape[-1] // N_CHIP
        return ref.at[full + (slice(None), pl.ds(pl.multiple_of(q * width, LANE), width))]
    height = ref.shape[-2] // N_CHIP
    return ref.at[full + (pl.ds(pl.multiple_of(q * height, HALO), height), slice(None))]


HBM = pl.BlockSpec(memory_space=pltpu.HBM)
SEM = pl.BlockSpec(memory_space=pltpu.SEMAPHORE)
EFFECT = pltpu.SideEffectType.DATAFLOW_SIDE_EFFECTING


def _in_hbm(v):
    return pltpu.with_memory_space_constraint(v, pltpu.HBM)


def _gather_start(bufs, col_sharded, name):
    n = len(bufs)

    def body(*refs):
        ins = refs[:n]
        send_sems, recv_sems = refs[n], refs[n + 1]
        token = refs[-1]
        x, y, c, chips = _place()
        q = 2 * x + y
        for w in range(n):
            for k, chip in enumerate(chips):
                pltpu.make_async_remote_copy(
                    src_ref=_window(ins[w], col_sharded[w], q, 0), dst_ref=_window(ins[w], col_sharded[w], q, 0),
                    send_sem=send_sems.at[3 * w + k], recv_sem=recv_sems.at[3 * w + k],
                    device_id=(*chip, c), device_id_type=MESH).start()
        token[...] = jnp.zeros_like(token)

    out = pl.pallas_call(
        body, name=name,
        out_shape=(pltpu.SemaphoreType.DMA((3 * n,)), pltpu.SemaphoreType.DMA((3 * n,)),
                   *[pltpu.HBM(b.shape, b.dtype) for b in bufs], jax.ShapeDtypeStruct((SUBLANE, LANE), F32)),
        in_specs=(HBM,) * n, out_specs=(SEM, SEM) + (HBM,) * n + (pl.BlockSpec(memory_space=pltpu.VMEM),),
        input_output_aliases={w: 2 + w for w in range(n)},
        compiler_params=pltpu.CompilerParams(has_side_effects=EFFECT),
    )(*[_in_hbm(b) for b in bufs])
    return out[0], out[1], list(out[2:2 + n]), out[-1]


def _gather_wait(send_sems, recv_sems, bufs, col_sharded, after, name):
    n = len(bufs)

    def body(*refs):
        ins = refs[:n]
        send_sems, recv_sems = refs[n], refs[n + 1]
        x, y, c, chips = _place()
        q = 2 * x + y
        for w in range(n):
            for k, (cx, cy) in enumerate(chips):
                cp = pltpu.make_async_remote_copy(
                    src_ref=_window(ins[w], col_sharded[w], q, 0),
                    dst_ref=_window(ins[w], col_sharded[w], 2 * cx + cy, 0),
                    send_sem=send_sems.at[3 * w + k], recv_sem=recv_sems.at[3 * w + k],
                    device_id=(cx, cy, c), device_id_type=MESH)
                cp.wait_send()
                cp.wait_recv()

    out = pl.pallas_call(
        body, name=name, out_shape=tuple(pltpu.HBM(b.shape, b.dtype) for b in bufs),
        in_specs=(HBM,) * n + (SEM, SEM) + (ANY,) * len(after), out_specs=(HBM,) * n,
        input_output_aliases={w: w for w in range(n)},
        compiler_params=pltpu.CompilerParams(has_side_effects=EFFECT),
    )(*bufs, send_sems, recv_sems, *after)
    return list(out)


def _scatter_start(grads, col_sharded, name):
    n = len(grads)
    lands = []
    for g, cs in zip(grads, col_sharded):
        K, N = g.shape
        lands.append(lax.empty((3, K, N // N_CHIP) if cs else (3, K // N_CHIP, N), BF16))

    def body(*refs):
        ins, slots = refs[:n], refs[n:2 * n]
        send_sems, recv_sems = refs[2 * n], refs[2 * n + 1]
        token = refs[-1]
        x, y, c, chips = _place()
        for w in range(n):
            for k, (cx, cy) in enumerate(chips):
                pltpu.make_async_remote_copy(
                    src_ref=_window(ins[w], col_sharded[w], 2 * cx + cy, 0), dst_ref=slots[w].at[k],
                    send_sem=send_sems.at[3 * w + k], recv_sem=recv_sems.at[3 * w + k],
                    device_id=(cx, cy, c), device_id_type=MESH).start()
        token[...] = jnp.zeros_like(token)

    out = pl.pallas_call(
        body, name=name,
        out_shape=(pltpu.SemaphoreType.DMA((3 * n,)), pltpu.SemaphoreType.DMA((3 * n,)),
                   *[pltpu.HBM(b.shape, b.dtype) for b in grads], *[pltpu.HBM(b.shape, b.dtype) for b in lands],
                   jax.ShapeDtypeStruct((SUBLANE, LANE), F32)),
        in_specs=(HBM,) * (2 * n),
        out_specs=(SEM, SEM) + (HBM,) * (2 * n) + (pl.BlockSpec(memory_space=pltpu.VMEM),),
        input_output_aliases={w: 2 + w for w in range(2 * n)},
        compiler_params=pltpu.CompilerParams(has_side_effects=EFFECT),
    )(*[_in_hbm(b) for b in grads], *[_in_hbm(b) for b in lands])
    return out[0], out[1], list(out[2:2 + n]), list(out[2 + n:2 + 2 * n]), out[-1]


def _scatter_wait(send_sems, recv_sems, grads, lands, col_sharded, after, name):
    n = len(grads)

    def body(*refs):
        ins, slots = refs[:n], refs[n:2 * n]
        send_sems, recv_sems = refs[2 * n], refs[2 * n + 1]
        x, y, c, chips = _place()
        for w in range(n):
            for k, (cx, cy) in enumerate(chips):
                cp = pltpu.make_async_remote_copy(
                    src_ref=_window(ins[w], col_sharded[w], 2 * cx + cy, 0), dst_ref=slots[w].at[k],
                    send_sem=send_sems.at[3 * w + k], recv_sem=recv_sems.at[3 * w + k],
                    device_id=(cx, cy, c), device_id_type=MESH)
                cp.wait_send()
                cp.wait_recv()

    out = pl.pallas_call(
        body, name=name, out_shape=tuple(pltpu.HBM(b.shape, b.dtype) for b in list(grads) + list(lands)),
        in_specs=(HBM,) * (2 * n) + (SEM, SEM) + (ANY,) * len(after), out_specs=(HBM,) * (2 * n),
        input_output_aliases={w: w for w in range(2 * n)},
        compiler_params=pltpu.CompilerParams(has_side_effects=EFFECT),
    )(*grads, *lands, send_sems, recv_sems, *after)
    return list(out[:n]), list(out[n:])


def _swap_with_sibling(arrays, name):
    n = len(arrays)

    def body(*refs):
        ins, outs = refs[:n], refs[n:2 * n]
        send_sems, recv_sems = refs[2 * n:]
        x, y, c, _ = _place()
        copies = []
        for w in range(n):
            cp = pltpu.make_async_remote_copy(
                src_ref=ins[w], dst_ref=outs[w], send_sem=send_sems.at[w], recv_sem=recv_sems.at[w],
                device_id=(x, y, 1 - c), device_id_type=MESH)
            cp.start()
            copies.append(cp)
        for cp in copies:
            cp.wait()

    return pl.pallas_call(
        body, name=name, out_shape=tuple(jax.ShapeDtypeStruct(a.shape, a.dtype) for a in arrays),
        in_specs=[ANY] * n, out_specs=(ANY,) * n,
        scratch_shapes=[pltpu.SemaphoreType.DMA((n,)), pltpu.SemaphoreType.DMA((n,))],
    )(*arrays)


BIG = ("w_in", "w_pa", "w_pb", "w_pc", "w_o", "w_13", "w_2")
BIG_COL_SHARDED = (True, True, True, True, False, True, False)
SMALL = ("b_mod", "g_mix", "gm_ln_g", "gm_ln_b", "gm_w_s", "gm_b_s", "pool_w", "pool_scale", "conv_w",
         "g_ffn", "g_final")
WEIGHTS = ("w_mod", "b_mod", "g_mix", "w_in", "gm_ln_g", "gm_ln_b", "gm_w_s", "gm_b_s", "w_pa", "pool_w",
           "pool_scale", "w_pb", "conv_w", "w_pc", "w_o", "g_ffn", "w_13", "w_2", "g_final")


def _pack(arrays, width):
    flat = jnp.concatenate([a.reshape(-1) for a in arrays])
    rows = -(-flat.shape[0] // width)
    rows = -(-rows // SUBLANE) * SUBLANE
    flat = jnp.pad(flat, (0, rows * width - flat.shape[0]))
    return flat.reshape(rows, width)


def _unpack(packed, shapes):
    flat = packed.reshape(-1)
    out, off = [], 0
    for s in shapes:
        size = 1
        for d in s:
            size *= d
        out.append(flat[off:off + size].reshape(s))
        off += size
    return out


def kernel(x, c, w_mod, b_mod, g_mix, w_in, gm_ln_g, gm_ln_b, gm_w_s, gm_b_s, w_pa, pool_w, pool_scale, w_pb, conv_w, w_pc, w_o, g_ffn, w_13, w_2, g_final, loss_target, m_w_mod, m_b_mod, m_g_mix, m_w_in, m_gm_ln_g, m_gm_ln_b, m_gm_w_s, m_gm_b_s, m_w_pa, m_pool_w, m_pool_scale, m_w_pb, m_conv_w, m_w_pc, m_w_o, m_g_ffn, m_w_13, m_w_2, m_g_final, v_w_mod, v_b_mod, v_g_mix, v_w_in, v_gm_ln_g, v_gm_ln_b, v_gm_w_s, v_gm_b_s, v_w_pa, v_pool_w, v_pool_scale, v_w_pb, v_conv_w, v_w_pc, v_w_o, v_g_ffn, v_w_13, v_w_2, v_g_final):
    W = dict(w_mod=w_mod, b_mod=b_mod, g_mix=g_mix, w_in=w_in, gm_ln_g=gm_ln_g, gm_ln_b=gm_ln_b, gm_w_s=gm_w_s,
             gm_b_s=gm_b_s, w_pa=w_pa, pool_w=pool_w, pool_scale=pool_scale, w_pb=w_pb, conv_w=conv_w, w_pc=w_pc,
             w_o=w_o, g_ffn=g_ffn, w_13=w_13, w_2=w_2, g_final=g_final)
    Mo = dict(w_mod=m_w_mod, b_mod=m_b_mod, g_mix=m_g_mix, w_in=m_w_in, gm_ln_g=m_gm_ln_g, gm_ln_b=m_gm_ln_b,
              gm_w_s=m_gm_w_s, gm_b_s=m_gm_b_s, w_pa=m_w_pa, pool_w=m_pool_w, pool_scale=m_pool_scale, w_pb=m_w_pb,
              conv_w=m_conv_w, w_pc=m_w_pc, w_o=m_w_o, g_ffn=m_g_ffn, w_13=m_w_13, w_2=m_w_2, g_final=m_g_final)
    Vo = dict(w_mod=v_w_mod, b_mod=v_b_mod, g_mix=v_g_mix, w_in=v_w_in, gm_ln_g=v_gm_ln_g, gm_ln_b=v_gm_ln_b,
              gm_w_s=v_gm_w_s, gm_b_s=v_gm_b_s, w_pa=v_w_pa, pool_w=v_pool_w, pool_scale=v_pool_scale, w_pb=v_w_pb,
              conv_w=v_conv_w, w_pc=v_w_pc, w_o=v_w_o, g_ffn=v_g_ffn, w_13=v_w_13, w_2=v_w_2, g_final=v_g_final)

    B, S, D = x.shape
    T = B * S
    L = w_in.shape[0]
    Bg = B * N_DEV
    N4 = w_mod.shape[2]
    CW = conv_w.shape[2]
    xi, yi, ci = lax.axis_index("x"), lax.axis_index("y"), lax.axis_index("c")
    chip = 2 * xi + yi
    dev = 2 * chip + ci

    head = _pack([c, conv_w], D)
    hrows = head.shape[0]
    got = _allgather8(head, "gather_c_conv").reshape(N_DEV, hrows * D)
    c_all = got[:, :B * D].reshape(Bg, D)
    conv_parts = got[:, B * D:B * D + L * 3 * CW].reshape(N_CHIP, 2, L, 3, CW)[:, 0]
    conv_full = jnp.transpose(conv_parts, (1, 2, 0, 3)).reshape(L, 3, N_CHIP * CW)

    b_cols = lax.dynamic_slice_in_dim(b_mod, chip * N4, N4, axis=1).reshape(L, 1, N4)
    mod_part = _mod_fwd(c_all, w_mod, b_cols, "mod_fwd")
    half = Bg // 2
    mine = lax.dynamic_slice_in_dim(mod_part, ci * half, half, axis=1)
    mod_got = _allgather8(mine.reshape(L * half, N4), "gather_mod").reshape(N_CHIP, 2, L, half, N4)
    mod_full = jnp.transpose(mod_got, (2, 1, 3, 0, 4)).reshape(L, Bg, 6, D)
    mod_mine = lax.dynamic_slice_in_dim(mod_full, dev * B, B, axis=1)
    mod = jnp.pad(mod_mine, ((0, 0), (0, 0), (0, SUBLANE - 6), (0, 0)))

    groups = [(("w_in",), 0), (BIG[1:], 0)] + [(BIG, l) for l in range(1, L)]
    gathers = []
    order = (mod,)
    for gi, (names, l) in enumerate(groups):
        cs = [BIG_COL_SHARDED[BIG.index(n)] for n in names]
        bufs = [_cast_into_full(W[n], l, c_, chip, "cast_" + n, deps=order) for n, c_ in zip(names, cs)]
        ss, rs, bufs, tok = _gather_start(bufs, cs, f"gather_start_{gi}")
        gathers.append((ss, rs, bufs, names, cs))
        order = (tok,)

    def gathered(gi, after):
        ss, rs, bufs, names, cs = gathers[gi]
        return dict(zip(names, _gather_wait(ss, rs, bufs, cs, after, f"gather_wait_{gi}")))

    def mixer_params(l):
        vec = jnp.zeros((SUBLANE, BR_W), F32)
        vec = vec.at[0].set(gm_ln_g[l]).at[1].set(gm_ln_b[l]).at[2].set(pool_scale[l])
        conv = jnp.zeros((SUBLANE, BR_W), F32).at[0:3].set(conv_full[l])
        b_s = jnp.zeros((CHUNK, LANE), F32).at[:, 0:HEADS].set(jnp.transpose(gm_b_s[l]))
        return dict(vec=vec, conv=conv, w_s=gm_w_s[l], b_s=b_s, pool_w=pool_w[l])

    xs = x.reshape(T, D)
    saved = []
    for l in range(L):
        prm = mixer_params(l)
        full = gathered(0, list(order)) if l == 0 else gathered(l + 1, [xs])
        z = _normed_matmul(xs, g_mix[l].reshape(1, D), mod[l], 0, 1, full["w_in"], S, "mm_in",
                           deps=order if l == 0 else ())
        cat = _mixer_fwd(z, prm, S, "mixer_fwd")
        if l == 0:
            full.update(gathered(1, [cat]))
        merged = _proj_fwd(cat, z, full["w_pa"], full["w_pb"], full["w_pc"], D, "proj_fwd")
        x1, mo = _matmul(merged, full["w_o"], mode="nn", name="mm_o", resid=(xs, mod[l], 2, S))
        ga, gb, act = _normed_matmul(x1, g_ffn[l].reshape(1, D), mod[l], 3, 4, full["w_13"], S, "mm_13", swiglu=True)
        x2, ffo = _matmul(act, full["w_2"], mode="nn", name="mm_2", resid=(x1, mod[l], 5, S))
        saved.append(dict(prm=prm, full=full, x0=xs, z=z, cat=cat, merged=merged, mo=mo, x1=x1, ga=ga, gb=gb, act=act,
                          ffo=ffo))
        xs = x2

    dx, head_part = _loss_head(xs, loss_target.reshape(T, D), g_final.reshape(1, D), "loss_head")
    loss = lax.psum(head_part[1, 0], ("x", "y", "c"))

    big_grads = {n: [None] * L for n in BIG}
    small_part = {n: [None] * L for n in SMALL if n not in ("b_mod", "g_final")}
    dmod = [None] * L
    scatters = []
    FFN, MIX = ("w_13", "w_2"), ("w_in", "w_pa", "w_pb", "w_pc", "w_o")

    def scatter(names, l, tag):
        cs = [BIG_COL_SHARDED[BIG.index(n)] for n in names]
        ss, rs, gthru, lands, tok = _scatter_start([big_grads[n][l] for n in names], cs, f"scatter_start_{l}{tag}")
        scatters.append((ss, rs, gthru, lands, names, cs, l))
        return (tok,)

    def exchange_small():
        dmod_l = jnp.stack(dmod, axis=0).reshape(L * B, 6 * D)
        rows = -(-(L * B) // SUBLANE) * SUBLANE
        dmod_got = _allgather8(jnp.pad(dmod_l, ((0, rows - L * B), (0, 0))), "gather_dmod")
        local = dict(b_mod=jnp.sum(jnp.stack(dmod, axis=0), axis=1).reshape(L, 6 * D), g_final=head_part[0])
        for n in small_part:
            local[n] = jnp.stack(small_part[n], axis=0)
        packed = _pack([local[n] for n in SMALL], LANE)
        return dmod_got, rows, _allgather8(packed, "gather_small"), [local[n].shape for n in SMALL]

    sent = ()
    for l in reversed(range(L)):
        sv = saved[l]
        full = sv["full"]
        dffo, pg2, da, db = _gated_dgrad(dx, sv["ffo"], mod[l], 5, full["w_2"], S, "mm_2_dgrad",
                                         gab=(sv["ga"], sv["gb"]), deps=sent)
        dh2 = _dgrad_pair(da, db, full["w_13"], "mm_13_dgrad")
        dx1, h2, pb2, pgf = _norm_bwd(sv["x1"], dh2, dx, g_ffn[l].reshape(1, D), mod[l], 3, 4, S, "norm_ffn_bwd")
        big_grads["w_2"][l] = _matmul(sv["act"], dffo, mode="tn", name="mm_2_wgrad")
        big_grads["w_13"][l] = _wgrad_pair(h2, da, db, "mm_13_wgrad")
        sent = scatter(FFN, l, "a")

        dmo, pg1, dmerged = _gated_dgrad(dx1, sv["mo"], mod[l], 2, full["w_o"], S, "mm_o_dgrad", deps=sent)
        big_grads["w_o"][l] = _matmul(sv["merged"], dmo, mode="tn", name="mm_o_wgrad")
        dy, dgl, dcat = _proj_bwd(dmerged, sv["cat"], sv["z"], full["w_pa"], full["w_pb"], full["w_pc"], D,
                                  "proj_bwd")
        for k, n in enumerate(("w_pa", "w_pb", "w_pc")):
            big_grads[n][l] = _matmul(sv["cat"], dy, mode="tn", name="mm_proj_wgrad",
                                      a_cols=(k * BR_W, BR_W), b_cols=(k * D, D))
        dz, pv, dws, dbs, dpw = _mixer_bwd(sv["z"], dcat, dgl, sv["prm"], S, "mixer_bwd")
        dh = _matmul(dz, full["w_in"], mode="nt", name="mm_in_dgrad")
        dx0, h, pb1, pgm = _norm_bwd(sv["x0"], dh, dx1, g_mix[l].reshape(1, D), mod[l], 0, 1, S, "norm_mix_bwd")
        dx = dx0

        dmod[l] = jnp.stack([pb1[:, 0], pb1[:, 1], pg1[:, 0], pb2[:, 0], pb2[:, 1], pg2[:, 0]], axis=1)
        small_part["g_mix"][l] = pgm[0]
        small_part["g_ffn"][l] = pgf[0]
        small_part["gm_ln_g"][l] = pv[0]
        small_part["gm_ln_b"][l] = pv[1]
        small_part["pool_scale"][l] = pv[2]
        small_part["conv_w"][l] = pv[3:6]
        small_part["gm_w_s"][l] = dws
        small_part["gm_b_s"][l] = jnp.transpose(dbs[:, 0:HEADS])
        small_part["pool_w"][l] = dpw

        if l == 0:
            dmod_got, dmod_rows, small_got, pshapes = exchange_small()
            last = (dmod_got, small_got)
        else:
            last = ()
        big_grads["w_in"][l] = _matmul(h, dz, mode="tn", name="mm_in_wgrad", deps=last)
        sent = scatter(MIX, l, "b")
    grad_x = dx.reshape(B, S, D)

    results = {}

    dmod_all = dmod_got.reshape(N_DEV, dmod_rows, 6 * D)[:, :L * B].reshape(N_DEV, L, B, 6 * D)
    dmod_all = jnp.transpose(dmod_all, (1, 0, 2, 3)).reshape(L, Bg, 6 * D)
    dmod_cols = lax.dynamic_slice_in_dim(dmod_all, chip * N4, N4, axis=2)
    g_wmod = _mod_wgrad(c_all, dmod_cols, "mod_wgrad", deps=sent)
    res = _adamw(w_mod.reshape(L * D, N4), m_w_mod.reshape(L * D, N4), v_w_mod.reshape(L * D, N4),
                 [g_wmod.reshape(L * D, N4)], "adamw_w_mod")
    results["w_mod"] = [r.reshape(L, D, N4) for r in res]

    gathered_small = small_got.reshape(N_DEV, small_got.shape[0] // N_DEV, LANE)
    g_small = dict(zip(SMALL, _unpack(_sum_devices(gathered_small, "sum_small"), pshapes)))
    g_small["conv_w"] = lax.dynamic_slice_in_dim(g_small["conv_w"], chip * CW, CW, axis=2)
    wshapes = [W[n].shape for n in SMALL]
    res = _adamw(_pack([W[n] for n in SMALL], LANE), _pack([Mo[n] for n in SMALL], LANE),
                 _pack([Vo[n] for n in SMALL], LANE), [_pack([g_small[n] for n in SMALL], LANE)], "adamw_small")
    small_res = [_unpack(r, wshapes) for r in res]
    for i, n in enumerate(SMALL):
        results[n] = [small_res[j][i] for j in range(4)]

    stacks = {n: lax.empty((W[n].shape[0] * W[n].shape[1], W[n].shape[2]), F32) for n in BIG}
    after = [res[0], results["w_mod"][0]]
    for ss, rs, gthru, lands, names, cs, l in scatters:
        gthru, lands = _scatter_wait(ss, rs, gthru, lands, cs, after, f"scatter_wait_{l}_{names[0]}")
        for n, g, ld, c_ in zip(names, gthru, lands, cs):
            stacks[n] = _sum_into(stacks[n], g, ld, l, c_, chip, "sum_" + n)
        after = [stacks[names[-1]]]
    others = _swap_with_sibling([stacks[n] for n in BIG], "swap_sums")
    for n, other in zip(BIG, others):
        _, R, C = W[n].shape
        res = _adamw(W[n].reshape(L * R, C), Mo[n].reshape(L * R, C), Vo[n].reshape(L * R, C), [stacks[n], other],
                     "adamw_" + n)
        results[n] = [r.reshape(L, R, C) for r in res]

    return (loss, grad_x, *[results[n][0] for n in WEIGHTS], *[results[n][1] for n in WEIGHTS],
            *[results[n][2] for n in WEIGHTS], *[results[n][3] for n in WEIGHTS])
```

```python
import jax
import jax.numpy as jnp
from jax import lax
from jax.experimental import pallas as pl
from jax.experimental.pallas import tpu as pltpu

F32 = jnp.float32
BF16 = jnp.bfloat16
MESH = pl.DeviceIdType.MESH

EPS = 1e-6
CHUNK = 128
HEADS = 4
HEAD_DIM = 128
BR_W = 512
N_GROUP = 4
GROUP_DIM = 128
HALO = 16
N_SPLIT = 6 * BR_W
N_CHIP = 4
N_DEV = 8

ADAM_LR = 0.001
ADAM_B1 = 0.9
ADAM_B2 = 0.999
ADAM_EPS = 1e-08
ADAM_WD = 0.01
ADAM_STEP = 10

V7X_VMEM_LIMIT = 56 * 1024 * 1024
LANE = 128
SUBLANE = 8

GELU_K = 0.7978845608028654
GELU_C = 0.044715


def _cparams(sem):
    return pltpu.CompilerParams(dimension_semantics=sem, vmem_limit_bytes=V7X_VMEM_LIMIT)


def _pick(n, cap, q=LANE):
    best = None
    d = q
    while d <= min(n, cap):
        if n % d == 0:
            best = d
        d += q
    return n if best is None else best


def _sigmoid(x):
    return 0.5 * jnp.tanh(0.5 * x) + 0.5


def _gelu(x):
    t = jnp.tanh(GELU_K * (x + GELU_C * x * x * x))
    return 0.5 * x * (1.0 + t), t


def _gelu_grad(x, t):
    return 0.5 * (1.0 + t) + 0.5 * x * (1.0 - t * t) * GELU_K * (1.0 + 3.0 * GELU_C * x * x)


def _matmul(a, b, *, mode, name, out_dtype=None, layer=None, a_cols=None, b_cols=None,
            resid=None, deps=(), tm_cap=None, tn_cap=1536, tk_cap=1536):
    out_dtype = BF16 if out_dtype is None else out_dtype
    b2 = b.shape[-2:]
    if tm_cap is None:
        k_len = a.shape[0] if mode == "tn" else a.shape[1]
        if mode == "tn":
            tm_cap, tk_cap = 1536, 2048
        else:
            tm_cap, tk_cap = (512, 8192) if k_len > 1536 else (1024, 1536)
    if mode == "nn":
        M, K = a.shape
        N = b2[1]
    elif mode == "nt":
        M, K = a.shape
        N = b2[0]
    else:
        K = a.shape[0]
        M = a.shape[1] if a_cols is None else a_cols[1]
        N = b2[1] if b_cols is None else b_cols[1]
    tm = _pick(M if resid is None else resid[3], tm_cap)
    tn = _pick(N, tn_cap)
    tk = _pick(K, tk_cap)
    nk = K // tk
    a_off = 0 if a_cols is None else a_cols[0] // tm
    b_off = 0 if b_cols is None else b_cols[0] // tn
    if a_cols is not None:
        assert a_cols[0] % tm == 0
    if b_cols is not None:
        assert b_cols[0] % tn == 0

    if mode == "nn":
        a_spec = pl.BlockSpec((tm, tk), lambda i, j, k: (i, k))
        b_blk, b_idx = (tk, tn), (lambda i, j, k: (k, j))
        dims = (((1,), (0,)), ((), ()))
    elif mode == "nt":
        a_spec = pl.BlockSpec((tm, tk), lambda i, j, k: (i, k))
        b_blk, b_idx = (tn, tk), (lambda i, j, k: (j, k))
        dims = (((1,), (1,)), ((), ()))
    else:
        a_spec = pl.BlockSpec((tk, tm), lambda i, j, k: (k, i + a_off))
        b_blk, b_idx = (tk, tn), (lambda i, j, k: (k, j + b_off))
        dims = (((0,), (0,)), ((), ()))
    if layer is None:
        b_spec = pl.BlockSpec(b_blk, b_idx)
    else:
        b_spec = pl.BlockSpec((None,) + b_blk, lambda i, j, k: (layer,) + b_idx(i, j, k))

    in_specs = [a_spec, b_spec]
    operands = [a, b]
    o_spec = pl.BlockSpec((tm, tn), lambda i, j, k: (i, j))
    if resid is not None:
        x, mod, row, seq = resid
        D = mod.shape[-1]
        in_specs += [o_spec, pl.BlockSpec((1, SUBLANE, tn), lambda i, j, k: ((i * tm) // seq, 0, j))]
        operands += [x, mod]
        out_shape = (jax.ShapeDtypeStruct((M, N), F32), jax.ShapeDtypeStruct((M, N), BF16))
        out_specs = (o_spec, o_spec)
        assert seq % tm == 0 and D == N
    else:
        out_shape = jax.ShapeDtypeStruct((M, N), out_dtype)
        out_specs = o_spec

    def finish(acc, refs):
        if resid is not None:
            x_ref, mod_ref, o_ref, p_ref = refs
            o_ref[...] = x_ref[...] + mod_ref[0, row:row + 1, :] * acc
            p_ref[...] = acc.astype(BF16)
        else:
            (o_ref,) = refs
            o_ref[...] = acc.astype(out_dtype)

    n_in = len(operands) - 2
    in_specs += [ANY] * len(deps)
    operands += list(deps)

    def body(a_ref, b_ref, *refs):
        refs = refs[:n_in] + refs[n_in + len(deps):]
        part = lax.dot_general(a_ref[...], b_ref[...], dims, preferred_element_type=F32)
        if nk == 1:
            finish(part, refs)
            return
        acc_ref = refs[-1]
        k = pl.program_id(2)

        @pl.when(k == 0)
        def _():
            acc_ref[...] = part

        @pl.when(k > 0)
        def _():
            acc_ref[...] += part

        @pl.when(k == nk - 1)
        def _():
            finish(acc_ref[...], refs[:-1])

    scratch = [] if nk == 1 else [pltpu.VMEM((tm, tn), F32)]
    return pl.pallas_call(
        body, name=name, grid=(M // tm, N // tn, nk), in_specs=in_specs, out_specs=out_specs,
        out_shape=out_shape, scratch_shapes=scratch,
        compiler_params=_cparams(("parallel", "parallel", "arbitrary")),
    )(*operands)


def _row_tile(seq, cap):
    return _pick(seq, cap, HALO)


def _mod_spec(tm, seq, D):
    return pl.BlockSpec((1, SUBLANE, D), lambda i: ((i * tm) // seq, 0, 0))


def _normed_matmul(x, g, mod, shift_row, scale_row, w, seq, name, swiglu=False, deps=()):
    T, D = x.shape
    N = w.shape[1] // 2 if swiglu else w.shape[1]
    tm = _pick(seq, 256 if swiglu else 1024)
    tn = _pick(N, 4096 if swiglu else 1536)
    nj = N // tn
    n_w = 2 if swiglu else 1

    def body(x_ref, g_ref, mod_ref, *rest):
        w_refs = rest[:n_w]
        outs = rest[n_w + len(deps):-1]
        h_ref = rest[-1]

        @pl.when(pl.program_id(1) == 0)
        def _():
            xv = x_ref[...]
            rstd = lax.rsqrt(jnp.mean(xv * xv, axis=-1, keepdims=True) + EPS)
            n = xv * rstd * g_ref[...]
            h = n * (1.0 + mod_ref[0, scale_row:scale_row + 1, :]) + mod_ref[0, shift_row:shift_row + 1, :]
            h_ref[...] = h.astype(BF16)

        h = h_ref[...]
        if not swiglu:
            outs[0][...] = jnp.dot(h, w_refs[0][...], preferred_element_type=F32).astype(BF16)
            return
        a16 = jnp.dot(h, w_refs[0][...], preferred_element_type=F32).astype(BF16)
        b16 = jnp.dot(h, w_refs[1][...], preferred_element_type=F32).astype(BF16)
        outs[0][...] = a16
        outs[1][...] = b16
        outs[2][...] = a16 * _sigmoid(a16) * b16

    w_specs = [pl.BlockSpec((D, tn), lambda i, j: (0, j))]
    if swiglu:
        w_specs.append(pl.BlockSpec((D, tn), lambda i, j: (0, j + nj)))
    o_spec = pl.BlockSpec((tm, tn), lambda i, j: (i, j))
    n_out = 3 if swiglu else 1
    out = pl.pallas_call(
        body, name=name, grid=(T // tm, nj),
        in_specs=[pl.BlockSpec((tm, D), lambda i, j: (i, 0)), pl.BlockSpec((1, D), lambda i, j: (0, 0)),
                  pl.BlockSpec((1, SUBLANE, D), lambda i, j: ((i * tm) // seq, 0, 0))] + w_specs + [ANY] * len(deps),
        out_specs=(o_spec,) * n_out, out_shape=(jax.ShapeDtypeStruct((T, N), BF16),) * n_out,
        scratch_shapes=[pltpu.VMEM((tm, D), BF16)],
        compiler_params=_cparams(("parallel", "arbitrary")),
    )(x, g, mod, *([w] * n_w), *deps)
    return out if swiglu else out[0]


def _norm_bwd(x, dh, dres, g, mod, shift_row, scale_row, seq, name):
    T, D = x.shape
    B = mod.shape[0]
    tm = _row_tile(seq, 512)
    per_seq = seq // tm

    def body(x_ref, dh_ref, dres_ref, g_ref, mod_ref, dx_ref, h_ref, pb_ref, pg_ref):
        i = pl.program_id(0)
        xv = x_ref[...]
        dhv = dh_ref[...].astype(F32)
        gv = g_ref[...]
        scale1 = 1.0 + mod_ref[0, scale_row:scale_row + 1, :]
        rstd = lax.rsqrt(jnp.mean(xv * xv, axis=-1, keepdims=True) + EPS)
        xhat = xv * rstd
        n = xhat * gv
        dn = dhv * scale1
        dxhat = dn * gv
        dx = rstd * (dxhat - xhat * jnp.mean(dxhat * xhat, axis=-1, keepdims=True))
        dx_ref[...] = dres_ref[...] + dx
        h_ref[...] = (n * scale1 + mod_ref[0, shift_row:shift_row + 1, :]).astype(BF16)

        @pl.when(i % per_seq == 0)
        def _():
            pb_ref[...] = jnp.zeros_like(pb_ref)

        @pl.when(i == 0)
        def _():
            pg_ref[...] = jnp.zeros_like(pg_ref)

        pb_ref[0, 0:1, :] += jnp.sum(dhv, axis=0, keepdims=True)
        pb_ref[0, 1:2, :] += jnp.sum(dhv * n, axis=0, keepdims=True)
        pg_ref[0:1, :] += jnp.sum(dn * xhat, axis=0, keepdims=True)

    row = pl.BlockSpec((tm, D), lambda i: (i, 0))
    return pl.pallas_call(
        body, name=name, grid=(T // tm,),
        in_specs=[row, row, row, pl.BlockSpec((1, D), lambda i: (0, 0)), _mod_spec(tm, seq, D)],
        out_specs=(row, row, _mod_spec(tm, seq, D), pl.BlockSpec((SUBLANE, D), lambda i: (0, 0))),
        out_shape=(jax.ShapeDtypeStruct((T, D), F32), jax.ShapeDtypeStruct((T, D), BF16),
                   jax.ShapeDtypeStruct((B, SUBLANE, D), F32), jax.ShapeDtypeStruct((SUBLANE, D), F32)),
        compiler_params=_cparams(("arbitrary",)),
    )(x, dh, dres, g, mod)


def _gated_dgrad(dx, prod, mod, gate_row, w, seq, name, gab=None, deps=()):
    T, D = dx.shape
    B = mod.shape[0]
    N = w.shape[0]
    tm = _pick(seq, 512 if gab is None else 256)
    tn = _pick(N, 4096)
    per_seq = seq // tm
    n_gab = 0 if gab is None else 2

    def body(dx_ref, p_ref, mod_ref, w_ref, *rest):
        gab_refs = rest[:n_gab]
        dp_ref, pb_ref = rest[n_gab + len(deps):n_gab + len(deps) + 2]
        outs = rest[n_gab + len(deps) + 2:-1]
        a_ref = rest[-1]
        i = pl.program_id(0)

        @pl.when(pl.program_id(1) == 0)
        def _():
            dxv = dx_ref[...]
            dp = (dxv * mod_ref[0, gate_row:gate_row + 1, :]).astype(BF16)
            a_ref[...] = dp
            dp_ref[...] = dp

            @pl.when(i % per_seq == 0)
            def _():
                pb_ref[...] = jnp.zeros_like(pb_ref)

            pb_ref[0, 0:1, :] += jnp.sum(dxv * p_ref[...].astype(F32), axis=0, keepdims=True)

        du = lax.dot_general(a_ref[...], w_ref[...], (((1,), (1,)), ((), ())), preferred_element_type=F32)
        if gab is None:
            outs[0][...] = du.astype(BF16)
            return
        du = du.astype(BF16)
        a = gab_refs[0][...]
        b = gab_refs[1][...]
        sg = _sigmoid(a)
        dsg = du * sg
        outs[0][...] = dsg * b * (1.0 + a * (1.0 - sg))
        outs[1][...] = dsg * a

    row = pl.BlockSpec((tm, D), lambda i, j: (i, 0))
    tile = pl.BlockSpec((tm, tn), lambda i, j: (i, j))
    mod_spec = pl.BlockSpec((1, SUBLANE, D), lambda i, j: ((i * tm) // seq, 0, 0))
    n_out = 1 if gab is None else 2
    out = pl.pallas_call(
        body, name=name, grid=(T // tm, N // tn),
        in_specs=[row, row, mod_spec, pl.BlockSpec((tn, D), lambda i, j: (j, 0))] + [tile] * n_gab
        + [ANY] * len(deps),
        out_specs=(row, mod_spec) + (tile,) * n_out,
        out_shape=(jax.ShapeDtypeStruct((T, D), BF16), jax.ShapeDtypeStruct((B, SUBLANE, D), F32))
        + (jax.ShapeDtypeStruct((T, N), BF16),) * n_out,
        scratch_shapes=[pltpu.VMEM((tm, D), BF16)],
        compiler_params=_cparams(("arbitrary", "arbitrary")),
    )(dx, prod, mod, w, *(gab or ()), *deps)
    return out


def _dgrad_pair(da, db, w, name):
    T, Fh = da.shape
    D = w.shape[0]
    tm = _pick(T, 512)

    def body(da_ref, db_ref, wa_ref, wb_ref, o_ref):
        dims = (((1,), (1,)), ((), ()))
        acc = lax.dot_general(da_ref[...], wa_ref[...], dims, preferred_element_type=F32)
        acc = acc + lax.dot_general(db_ref[...], wb_ref[...], dims, preferred_element_type=F32)
        o_ref[...] = acc.astype(BF16)

    row = pl.BlockSpec((tm, Fh), lambda i: (i, 0))
    return pl.pallas_call(
        body, name=name, grid=(T // tm,),
        in_specs=[row, row, pl.BlockSpec((D, Fh), lambda i: (0, 0)), pl.BlockSpec((D, Fh), lambda i: (0, 1))],
        out_specs=pl.BlockSpec((tm, D), lambda i: (i, 0)), out_shape=jax.ShapeDtypeStruct((T, D), BF16),
        compiler_params=_cparams(("parallel",)),
    )(da, db, w, w)


def _wgrad_pair(h, da, db, name):
    T, D = h.shape
    Fh = da.shape[1]
    tn = _pick(Fh, 1536)
    tk = _pick(T, 2048)
    half = Fh // tn
    nk = T // tk

    def body(h_ref, da_ref, db_ref, o_ref, acc_ref):
        j, k = pl.program_id(0), pl.program_id(1)
        dims = (((0,), (0,)), ((), ()))

        def accumulate(g_ref):
            part = lax.dot_general(h_ref[...], g_ref[...], dims, preferred_element_type=F32)

            @pl.when(k == 0)
            def _():
                acc_ref[...] = part

            @pl.when(k > 0)
            def _():
                acc_ref[...] += part

        @pl.when(j < half)
        def _():
            accumulate(da_ref)

        @pl.when(j >= half)
        def _():
            accumulate(db_ref)

        @pl.when(k == nk - 1)
        def _():
            o_ref[...] = acc_ref[...].astype(BF16)

    a_idx = lambda j, k: (jnp.where(j < half, k, 0), jnp.minimum(j, half - 1))
    b_idx = lambda j, k: (jnp.where(j >= half, k, 0), jnp.maximum(j - half, 0))
    return pl.pallas_call(
        body, name=name, grid=(2 * half, nk),
        in_specs=[pl.BlockSpec((tk, D), lambda j, k: (k, 0)), pl.BlockSpec((tk, tn), a_idx),
                  pl.BlockSpec((tk, tn), b_idx)],
        out_specs=pl.BlockSpec((D, tn), lambda j, k: (0, j)),
        out_shape=jax.ShapeDtypeStruct((D, 2 * Fh), BF16), scratch_shapes=[pltpu.VMEM((D, tn), F32)],
        compiler_params=_cparams(("parallel", "arbitrary")),
    )(h, da, db)


def _shift_down(v, d):
    return pltpu.roll(v, d, 0)


def _shift_up(v, d):
    return pltpu.roll(v, v.shape[0] - d, 0)


def _tril_mask():
    r = lax.broadcasted_iota(jnp.int32, (CHUNK, CHUNK), 0)
    c = lax.broadcasted_iota(jnp.int32, (CHUNK, CHUNK), 1)
    return c <= r


def _pool_counts(i, tm, seq, rows, first_row):
    r = lax.broadcasted_iota(jnp.int32, (rows, 1), 0) + (i * tm + first_row)
    pos1 = (r % seq + 1).astype(F32)
    return [jnp.minimum(pos1, float(2 << g)) for g in range(N_GROUP)]


def _mixer_forward_values(zt, hxb, hch, i, tm, seq, ln_g, ln_b, ws_ref, bs_ref, pw_ref, pscale, cw_ref):
    u = zt[:, 0 * BR_W:1 * BR_W]
    v = zt[:, 1 * BR_W:2 * BR_W]
    xb = zt[:, 2 * BR_W:3 * BR_W]
    bg = zt[:, 3 * BR_W:4 * BR_W]
    cg = zt[:, 4 * BR_W:5 * BR_W]
    hc = zt[:, 5 * BR_W:6 * BR_W]
    out = {}

    ug, tu = _gelu(u)
    vg, tv = _gelu(v)
    mu = jnp.mean(vg, axis=-1, keepdims=True)
    vc = vg - mu
    rstd = lax.rsqrt(jnp.mean(vc * vc, axis=-1, keepdims=True) + EPS)
    vhat = vc * rstd
    vn = (vhat * ln_g + ln_b).astype(BF16)
    mask = _tril_mask()
    wt = [jnp.where(mask, ws_ref[h], 0.0).astype(BF16) for h in range(HEADS)]
    rows = []
    for n in range(tm // CHUNK):
        blocks = []
        for h in range(HEADS):
            blk = vn[n * CHUNK:(n + 1) * CHUNK, h * HEAD_DIM:(h + 1) * HEAD_DIM]
            sb = jnp.dot(wt[h], blk, preferred_element_type=F32) + bs_ref[:, h:h + 1]
            blocks.append(sb)
        rows.append(jnp.concatenate(blocks, axis=1))
    s = jnp.concatenate(rows, axis=0) if len(rows) > 1 else rows[0]
    out.update(u=u, v=v, ug=ug, tu=tu, tv=tv, rstd=rstd, vhat=vhat, vn=vn, wt=wt, s=s, a_out=ug * s)

    ext = jnp.concatenate([hxb, xb], axis=0)
    cnt = _pool_counts(i, tm, seq, tm, 0)
    p, qs = [], []
    for g in range(N_GROUP):
        e = ext[:, g * GROUP_DIM:(g + 1) * GROUP_DIM]
        acc = e
        for d in (1, 2, 4, 8)[:g + 1]:
            acc = acc + _shift_down(acc, d)
        pg = acc[HALO:, :] / cnt[g] - xb[:, g * GROUP_DIM:(g + 1) * GROUP_DIM]
        p.append(pg.astype(BF16))
        qs.append(jnp.dot(p[g], pw_ref[g].astype(BF16), preferred_element_type=F32))
    q = jnp.concatenate(qs, axis=1)
    out.update(p=p, q=q, b_out=q * pscale)

    zc = cg * hc
    zce = jnp.concatenate([hch[:, :BR_W] * hch[:, BR_W:], zc], axis=0)
    z1 = _shift_down(zce, 1)[HALO:, :]
    z2 = _shift_down(zce, 2)[HALO:, :]
    y = cw_ref[0:1, :] * z2 + cw_ref[1:2, :] * z1 + cw_ref[2:3, :] * zc
    out.update(bg=bg, cg=cg, hc=hc, zc=zc, z1=z1, z2=z2, y=y, c_out=bg * y)
    return out


def _mixer_specs(tm, T):
    nb = T // HALO
    per = tm // HALO
    prev = lambda i: jnp.maximum(i * per - 1, 0)
    nxt = lambda i: jnp.minimum((i + 1) * per, nb - 1)
    return prev, nxt


def _mixer_fwd(z, prm, seq, name):
    T = z.shape[0]
    tm = _row_tile(seq, 256)
    per_seq = seq // tm
    prev, _ = _mixer_specs(tm, T)

    def body(z_ref, hxb_ref, hch_ref, vec_ref, cw_ref, ws_ref, bs_ref, pw_ref, cat_ref):
        i = pl.program_id(0)
        keep = jnp.where(i % per_seq == 0, 0.0, 1.0)
        zt = z_ref[...].astype(F32)
        hxb = hxb_ref[...].astype(F32) * keep
        hch = hch_ref[...].astype(F32) * keep
        o = _mixer_forward_values(zt, hxb, hch, i, tm, seq, vec_ref[0:1, :], vec_ref[1:2, :],
                                  ws_ref, bs_ref, pw_ref, vec_ref[2:3, :], cw_ref)
        cat_ref[:, 0 * BR_W:1 * BR_W] = o["a_out"].astype(BF16)
        cat_ref[:, 1 * BR_W:2 * BR_W] = o["b_out"].astype(BF16)
        cat_ref[:, 2 * BR_W:3 * BR_W] = o["c_out"].astype(BF16)

    full = lambda shape: pl.BlockSpec(shape, lambda i: (0,) * len(shape))
    return pl.pallas_call(
        body, name=name, grid=(T // tm,),
        in_specs=[pl.BlockSpec((tm, N_SPLIT), lambda i: (i, 0)),
                  pl.BlockSpec((HALO, BR_W), lambda i: (prev(i), 2)),
                  pl.BlockSpec((HALO, 2 * BR_W), lambda i: (prev(i), 2)),
                  full((SUBLANE, BR_W)), full((SUBLANE, BR_W)), full((HEADS, CHUNK, CHUNK)),
                  full((CHUNK, LANE)), full((N_GROUP, GROUP_DIM, GROUP_DIM))],
        out_specs=pl.BlockSpec((tm, 3 * BR_W), lambda i: (i, 0)),
        out_shape=jax.ShapeDtypeStruct((T, 3 * BR_W), BF16),
        compiler_params=_cparams(("parallel",)),
    )(z, z, z, prm["vec"], prm["conv"], prm["w_s"], prm["b_s"], prm["pool_w"])


def _mixer_bwd(z, dcat, dgl, prm, seq, name):
    T, IN = z.shape
    GL = IN - N_SPLIT
    tm = _row_tile(seq, 256)
    per_seq = seq // tm
    prev, nxt = _mixer_specs(tm, T)
    nrow = tm + HALO

    def body(z_ref, hxb_ref, hch_ref, nbg_ref, dcat_ref, ndb_ref, ndc_ref, dgl_ref,
             vec_ref, cw_ref, ws_ref, bs_ref, pw_ref,
             dz_ref, pv_ref, dws_ref, dbs_ref, dpw_ref):
        i = pl.program_id(0)
        keep_prev = jnp.where(i % per_seq == 0, 0.0, 1.0)
        keep_next = jnp.where(i % per_seq == per_seq - 1, 0.0, 1.0)
        zt = z_ref[...].astype(F32)
        hxb = hxb_ref[...].astype(F32) * keep_prev
        hch = hch_ref[...].astype(F32) * keep_prev
        ln_g = vec_ref[0:1, :]
        pscale = vec_ref[2:3, :]
        o = _mixer_forward_values(zt, hxb, hch, i, tm, seq, ln_g, vec_ref[1:2, :],
                                  ws_ref, bs_ref, pw_ref, pscale, cw_ref)
        dcv = dcat_ref[...].astype(F32)
        da = dcv[:, 0 * BR_W:1 * BR_W]
        db = dcv[:, 1 * BR_W:2 * BR_W]
        dc = dcv[:, 2 * BR_W:3 * BR_W]
        mask = _tril_mask()

        @pl.when(i == 0)
        def _():
            pv_ref[...] = jnp.zeros_like(pv_ref)
            dws_ref[...] = jnp.zeros_like(dws_ref)
            dbs_ref[...] = jnp.zeros_like(dbs_ref)
            dpw_ref[...] = jnp.zeros_like(dpw_ref)

        d_ug = da * o["s"]
        ds = da * o["ug"]
        ds_b = ds.astype(BF16)
        vn = o["vn"]
        dvn_rows = []
        dws = [jnp.zeros((CHUNK, CHUNK), F32) for _ in range(HEADS)]
        dsum = jnp.zeros((CHUNK, BR_W), F32)
        for n in range(tm // CHUNK):
            blocks = []
            rs = slice(n * CHUNK, (n + 1) * CHUNK)
            dsum = dsum + ds[rs, :]
            for h in range(HEADS):
                cs = slice(h * HEAD_DIM, (h + 1) * HEAD_DIM)
                dsb = ds_b[rs, cs]
                blocks.append(lax.dot_general(o["wt"][h], dsb, (((0,), (0,)), ((), ())),
                                              preferred_element_type=F32))
                dws[h] = dws[h] + lax.dot_general(dsb, vn[rs, cs], (((1,), (1,)), ((), ())),
                                                  preferred_element_type=F32)
            dvn_rows.append(jnp.concatenate(blocks, axis=1))
        dvn = jnp.concatenate(dvn_rows, axis=0) if len(dvn_rows) > 1 else dvn_rows[0]
        lane = lax.broadcasted_iota(jnp.int32, (CHUNK, LANE), 1)
        dbs_t = jnp.zeros((CHUNK, LANE), F32)
        for h in range(HEADS):
            dws_ref[h] += jnp.where(mask, dws[h], 0.0)
            rsum = jnp.sum(dsum[:, h * HEAD_DIM:(h + 1) * HEAD_DIM], axis=1, keepdims=True)
            dbs_t = dbs_t + jnp.where(lane == h, rsum, 0.0)
        dbs_ref[...] += dbs_t
        vhat = o["vhat"]
        pv_ref[0:1, :] += jnp.sum(dvn * vhat, axis=0, keepdims=True)
        pv_ref[1:2, :] += jnp.sum(dvn, axis=0, keepdims=True)
        dvhat = dvn * ln_g
        dvg = o["rstd"] * (dvhat - jnp.mean(dvhat, axis=-1, keepdims=True)
                           - vhat * jnp.mean(dvhat * vhat, axis=-1, keepdims=True))
        du = d_ug * _gelu_grad(o["u"], o["tu"])
        dv = dvg * _gelu_grad(o["v"], o["tv"])

        pv_ref[2:3, :] += jnp.sum(db * o["q"], axis=0, keepdims=True)
        dq = (db * pscale).astype(BF16)
        dqn = (ndb_ref[...].astype(F32) * pscale * keep_next).astype(BF16)
        cnt = _pool_counts(i, tm, seq, nrow, 0)
        dxb = []
        for g in range(N_GROUP):
            cs = slice(g * GROUP_DIM, (g + 1) * GROUP_DIM)
            pwg = pw_ref[g].astype(BF16)
            dpw_ref[g] += lax.dot_general(o["p"][g], dq[:, cs], (((0,), (0,)), ((), ())),
                                          preferred_element_type=F32)
            dp = lax.dot_general(dq[:, cs], pwg, (((1,), (1,)), ((), ())), preferred_element_type=F32)
            dpn = lax.dot_general(dqn[:, cs], pwg, (((1,), (1,)), ((), ())), preferred_element_type=F32)
            acc = jnp.concatenate([dp, dpn], axis=0) / cnt[g]
            for d in (1, 2, 4, 8)[:g + 1]:
                acc = acc + _shift_up(acc, d)
            dxb.append(acc[:tm, :] - dp)
        dxb = jnp.concatenate(dxb, axis=1)

        dbg = dc * o["y"]
        dy = dc * o["bg"]
        pv_ref[3:4, :] += jnp.sum(dy * o["z2"], axis=0, keepdims=True)
        pv_ref[4:5, :] += jnp.sum(dy * o["z1"], axis=0, keepdims=True)
        pv_ref[5:6, :] += jnp.sum(dy * o["zc"], axis=0, keepdims=True)
        dyn = ndc_ref[...].astype(F32) * nbg_ref[...].astype(F32) * keep_next
        dye = jnp.concatenate([dy, dyn], axis=0)
        dzc = (cw_ref[2:3, :] * dy + cw_ref[1:2, :] * _shift_up(dye, 1)[:tm, :]
               + cw_ref[0:1, :] * _shift_up(dye, 2)[:tm, :])
        dcg = dzc * o["hc"]
        dhc = dzc * o["cg"]

        dz_ref[:, 0 * BR_W:1 * BR_W] = du.astype(BF16)
        dz_ref[:, 1 * BR_W:2 * BR_W] = dv.astype(BF16)
        dz_ref[:, 2 * BR_W:3 * BR_W] = dxb.astype(BF16)
        dz_ref[:, 3 * BR_W:4 * BR_W] = dbg.astype(BF16)
        dz_ref[:, 4 * BR_W:5 * BR_W] = dcg.astype(BF16)
        dz_ref[:, 5 * BR_W:6 * BR_W] = dhc.astype(BF16)
        dz_ref[:, N_SPLIT:] = dgl_ref[...]

    full = lambda shape: pl.BlockSpec(shape, lambda i: (0,) * len(shape))
    return pl.pallas_call(
        body, name=name, grid=(T // tm,),
        in_specs=[pl.BlockSpec((tm, N_SPLIT), lambda i: (i, 0)),
                  pl.BlockSpec((HALO, BR_W), lambda i: (prev(i), 2)),
                  pl.BlockSpec((HALO, 2 * BR_W), lambda i: (prev(i), 2)),
                  pl.BlockSpec((HALO, BR_W), lambda i: (nxt(i), 3)),
                  pl.BlockSpec((tm, 3 * BR_W), lambda i: (i, 0)),
                  pl.BlockSpec((HALO, BR_W), lambda i: (nxt(i), 1)),
                  pl.BlockSpec((HALO, BR_W), lambda i: (nxt(i), 2)),
                  pl.BlockSpec((tm, GL), lambda i: (i, 0)),
                  full((SUBLANE, BR_W)), full((SUBLANE, BR_W)), full((HEADS, CHUNK, CHUNK)),
                  full((CHUNK, LANE)), full((N_GROUP, GROUP_DIM, GROUP_DIM))],
        out_specs=(pl.BlockSpec((tm, IN), lambda i: (i, 0)),
                   full((SUBLANE, BR_W)), full((HEADS, CHUNK, CHUNK)), full((CHUNK, LANE)),
                   full((N_GROUP, GROUP_DIM, GROUP_DIM))),
        out_shape=(jax.ShapeDtypeStruct((T, IN), BF16),
                   jax.ShapeDtypeStruct((SUBLANE, BR_W), F32),
                   jax.ShapeDtypeStruct((HEADS, CHUNK, CHUNK), F32),
                   jax.ShapeDtypeStruct((CHUNK, LANE), F32),
                   jax.ShapeDtypeStruct((N_GROUP, GROUP_DIM, GROUP_DIM), F32)),
        compiler_params=_cparams(("arbitrary",)),
    )(z, z, z, z, dcat, dcat, dcat, dgl, prm["vec"], prm["conv"], prm["w_s"], prm["b_s"], prm["pool_w"])


def _proj_fwd(cat, z, w_pa, w_pb, w_pc, D, name):
    T, IN = z.shape
    GL = 3 * D
    assert N_SPLIT % GL == 0
    glb = N_SPLIT // GL
    tm = _pick(T, 256, HALO)

    def body(cat_ref, gl_ref, wa_ref, wb_ref, wc_ref, m_ref):
        acc = jnp.zeros((tm, D), F32)
        for k, w_ref in enumerate((wa_ref, wb_ref, wc_ref)):
            y = jnp.dot(cat_ref[:, k * BR_W:(k + 1) * BR_W], w_ref[...], preferred_element_type=F32)
            acc = acc + _sigmoid(gl_ref[:, k * D:(k + 1) * D].astype(F32)) * y
        m_ref[...] = acc.astype(BF16)

    wspec = pl.BlockSpec((BR_W, D), lambda i: (0, 0))
    return pl.pallas_call(
        body, name=name, grid=(T // tm,),
        in_specs=[pl.BlockSpec((tm, 3 * BR_W), lambda i: (i, 0)),
                  pl.BlockSpec((tm, GL), lambda i: (i, glb)), wspec, wspec, wspec],
        out_specs=pl.BlockSpec((tm, D), lambda i: (i, 0)),
        out_shape=jax.ShapeDtypeStruct((T, D), BF16),
        compiler_params=_cparams(("parallel",)),
    )(cat, z, w_pa, w_pb, w_pc)


def _proj_bwd(dmerged, cat, z, w_pa, w_pb, w_pc, D, name):
    T, IN = z.shape
    GL = 3 * D
    glb = N_SPLIT // GL
    tm = _pick(T, 256, HALO)

    def body(dm_ref, cat_ref, gl_ref, wa_ref, wb_ref, wc_ref, dy_ref, dgl_ref, dcat_ref):
        dm = dm_ref[...].astype(F32)
        for k, w_ref in enumerate((wa_ref, wb_ref, wc_ref)):
            w = w_ref[...]
            y = jnp.dot(cat_ref[:, k * BR_W:(k + 1) * BR_W], w, preferred_element_type=F32)
            sg = _sigmoid(gl_ref[:, k * D:(k + 1) * D].astype(F32))
            dyk = (dm * sg).astype(BF16)
            dy_ref[:, k * D:(k + 1) * D] = dyk
            dgl_ref[:, k * D:(k + 1) * D] = (dm * y * sg * (1.0 - sg)).astype(BF16)
            dcat_ref[:, k * BR_W:(k + 1) * BR_W] = lax.dot_general(
                dyk, w, (((1,), (1,)), ((), ())), preferred_element_type=F32).astype(BF16)

    wspec = pl.BlockSpec((BR_W, D), lambda i: (0, 0))
    return pl.pallas_call(
        body, name=name, grid=(T // tm,),
        in_specs=[pl.BlockSpec((tm, D), lambda i: (i, 0)),
                  pl.BlockSpec((tm, 3 * BR_W), lambda i: (i, 0)),
                  pl.BlockSpec((tm, GL), lambda i: (i, glb)), wspec, wspec, wspec],
        out_specs=(pl.BlockSpec((tm, GL), lambda i: (i, 0)), pl.BlockSpec((tm, GL), lambda i: (i, 0)),
                   pl.BlockSpec((tm, 3 * BR_W), lambda i: (i, 0))),
        out_shape=(jax.ShapeDtypeStruct((T, GL), BF16), jax.ShapeDtypeStruct((T, GL), BF16),
                   jax.ShapeDtypeStruct((T, 3 * BR_W), BF16)),
        compiler_params=_cparams(("parallel",)),
    )(dmerged, cat, z, w_pa, w_pb, w_pc)


def _loss_head(x, target, g, name):
    T, D = x.shape
    tm = _pick(T, 512, SUBLANE)

    def body(x_ref, t_ref, g_ref, dx_ref, part_ref):
        i = pl.program_id(0)
        xv = x_ref[...]
        gv = g_ref[...]
        rstd = lax.rsqrt(jnp.mean(xv * xv, axis=-1, keepdims=True) + EPS)
        xhat = xv * rstd
        err = xhat * gv - t_ref[...]
        dy = err * (1.0 / D)
        dxhat = dy * gv
        dx_ref[...] = rstd * (dxhat - xhat * jnp.mean(dxhat * xhat, axis=-1, keepdims=True))

        @pl.when(i == 0)
        def _():
            part_ref[...] = jnp.zeros_like(part_ref)

        part_ref[0:1, :] += jnp.sum(dy * xhat, axis=0, keepdims=True)
        part_ref[1:2, :] += jnp.zeros((1, D), F32) + (0.5 / D) * jnp.sum(err * err)

    row = pl.BlockSpec((tm, D), lambda i: (i, 0))
    return pl.pallas_call(
        body, name=name, grid=(T // tm,),
        in_specs=[row, row, pl.BlockSpec((1, D), lambda i: (0, 0))],
        out_specs=(row, pl.BlockSpec((SUBLANE, D), lambda i: (0, 0))),
        out_shape=(jax.ShapeDtypeStruct((T, D), F32), jax.ShapeDtypeStruct((SUBLANE, D), F32)),
        compiler_params=_cparams(("arbitrary",)),
    )(x, target, g)


def _mod_fwd(c_all, w_mod, b_cols, name):
    L, D, N4 = w_mod.shape
    Bg = c_all.shape[0]
    tn = _pick(N4, 768)

    def body(c_ref, w_ref, b_ref, o_ref):
        cv = c_ref[...]
        ca = (cv * _sigmoid(cv)).astype(BF16)
        o_ref[...] = jnp.dot(ca, w_ref[...].astype(BF16), preferred_element_type=F32) + b_ref[...]

    return pl.pallas_call(
        body, name=name, grid=(L, N4 // tn),
        in_specs=[pl.BlockSpec((Bg, D), lambda l, j: (0, 0)),
                  pl.BlockSpec((None, D, tn), lambda l, j: (l, 0, j)),
                  pl.BlockSpec((None, 1, tn), lambda l, j: (l, 0, j))],
        out_specs=pl.BlockSpec((None, Bg, tn), lambda l, j: (l, 0, j)),
        out_shape=jax.ShapeDtypeStruct((L, Bg, N4), F32),
        compiler_params=_cparams(("parallel", "parallel")),
    )(c_all, w_mod, b_cols)


def _mod_wgrad(c_all, dmod_cols, name, deps=()):
    L, Bg, N4 = dmod_cols.shape
    D = c_all.shape[1]
    tn = _pick(N4, 768)

    def body(c_ref, d_ref, *rest):
        cv = c_ref[...]
        ca = (cv * _sigmoid(cv)).astype(BF16)
        rest[-1][...] = lax.dot_general(ca, d_ref[...].astype(BF16), (((0,), (0,)), ((), ())),
                                     preferred_element_type=F32)

    return pl.pallas_call(
        body, name=name, grid=(L, N4 // tn),
        in_specs=[pl.BlockSpec((Bg, D), lambda l, j: (0, 0)),
                  pl.BlockSpec((None, Bg, tn), lambda l, j: (l, 0, j))] + [ANY] * len(deps),
        out_specs=pl.BlockSpec((None, D, tn), lambda l, j: (l, 0, j)),
        out_shape=jax.ShapeDtypeStruct((L, D, N4), F32),
        compiler_params=_cparams(("parallel", "parallel")),
    )(c_all, dmod_cols, *deps)


def _rows_tile(R, C):
    return _pick(R, max(SUBLANE, (256 * 1024) // C), SUBLANE)


def _cast_into_full(w, layer, col_sharded, chip, name, deps=()):
    L, R, C = w.shape
    K, N = (R, C * N_CHIP) if col_sharded else (R * N_CHIP, C)
    tr = _pick(R, max(HALO, (512 * 1024) // C), HALO)
    nb = R // tr

    def body(q_ref, w_ref, *rest):
        rest[-1][...] = w_ref[...].astype(BF16)

    out_idx = (lambda i, q: (i, q[0])) if col_sharded else (lambda i, q: (q[0] * nb + i, 0))
    grid_spec = pltpu.PrefetchScalarGridSpec(
        num_scalar_prefetch=1, grid=(nb,),
        in_specs=[pl.BlockSpec((None, tr, C), lambda i, q: (layer, i, 0))] + [ANY] * len(deps),
        out_specs=pl.BlockSpec((tr, C), out_idx))
    return pl.pallas_call(
        body, name=name, grid_spec=grid_spec, out_shape=jax.ShapeDtypeStruct((K, N), BF16),
        compiler_params=_cparams(("arbitrary",)),
    )(chip.reshape(1).astype(jnp.int32), w, *deps)


def _sum_into(stack, grad, slots, layer, col_sharded, chip, name):
    _, R, C = slots.shape
    tr = _rows_tile(R, C)
    nb = R // tr

    def body(q_ref, stack_ref, g_ref, s_ref, o_ref):
        o_ref[...] = ((g_ref[...].astype(F32) + s_ref[0].astype(F32)) + s_ref[1].astype(F32)) + s_ref[2].astype(F32)

    g_idx = (lambda i, q: (i, q[0])) if col_sharded else (lambda i, q: (q[0] * nb + i, 0))
    grid_spec = pltpu.PrefetchScalarGridSpec(
        num_scalar_prefetch=1, grid=(nb,),
        in_specs=[ANY, pl.BlockSpec((tr, C), g_idx), pl.BlockSpec((3, tr, C), lambda i, q: (0, i, 0))],
        out_specs=pl.BlockSpec((tr, C), lambda i, q: (layer * nb + i, 0)))
    return pl.pallas_call(
        body, name=name, grid_spec=grid_spec, out_shape=jax.ShapeDtypeStruct(stack.shape, F32),
        input_output_aliases={1: 0}, compiler_params=_cparams(("arbitrary",)),
    )(chip.reshape(1).astype(jnp.int32), stack, grad, slots)


def _sum_devices(parts, name):
    n, R, C = parts.shape
    tr = _rows_tile(R, C)

    def body(s_ref, o_ref):
        acc = s_ref[0]
        for d in range(1, n):
            acc = acc + s_ref[d]
        o_ref[...] = acc

    return pl.pallas_call(
        body, name=name, grid=(R // tr,),
        in_specs=[pl.BlockSpec((n, tr, C), lambda i: (0, i, 0))],
        out_specs=pl.BlockSpec((tr, C), lambda i: (i, 0)),
        out_shape=jax.ShapeDtypeStruct((R, C), F32), compiler_params=_cparams(("parallel",)),
    )(parts)


def _adamw(w, m, v, grads, name):
    R, C = w.shape
    tr = _rows_tile(R, C)
    bc1 = 1.0 - ADAM_B1 ** ADAM_STEP
    bc2 = 1.0 - ADAM_B2 ** ADAM_STEP
    ng = len(grads)

    def body(*refs):
        w_ref, m_ref, v_ref = refs[:3]
        g_refs = refs[3:3 + ng]
        g_out, d_out, m_out, v_out = refs[3 + ng:]
        g = g_refs[0][...]
        for r in g_refs[1:]:
            g = g + r[...]
        mn = ADAM_B1 * m_ref[...] + (1.0 - ADAM_B1) * g
        vn = ADAM_B2 * v_ref[...] + (1.0 - ADAM_B2) * (g * g)
        m_hat = mn / bc1
        v_hat = vn / bc2
        g_out[...] = g
        d_out[...] = -ADAM_LR * (m_hat / (jnp.sqrt(v_hat) + ADAM_EPS) + ADAM_WD * w_ref[...])
        m_out[...] = mn
        v_out[...] = vn

    spec = pl.BlockSpec((tr, C), lambda i: (i, 0))
    sds = jax.ShapeDtypeStruct((R, C), F32)
    return pl.pallas_call(
        body, name=name, grid=(R // tr,), in_specs=[spec] * (3 + ng), out_specs=(spec,) * 4,
        out_shape=(sds,) * 4, compiler_params=_cparams(("parallel",)),
    )(w, m, v, *grads)


def _place():
    x, y, c = lax.axis_index("x"), lax.axis_index("y"), lax.axis_index("c")
    chips = [(1 - x, y), (x, 1 - y), (1 - x, 1 - y)]
    return x, y, c, chips


ANY = pl.BlockSpec(memory_space=pl.ANY)


def _allgather8(v, name):
    m_per, n = v.shape

    def body(x_ref, out_ref, send_sems, recv_sems, local_sem):
        x, y, c, chips = _place()
        me, sibling = (x, y, c), (x, y, 1 - c)

        def rows(px, py, pc):
            return out_ref.at[pl.ds((4 * px + 2 * py + pc) * m_per, m_per), :]

        def copy(k, block, to, src=None):
            return pltpu.make_async_remote_copy(
                src_ref=rows(*block) if src is None else src, dst_ref=rows(*block),
                send_sem=send_sems.at[k], recv_sem=recv_sems.at[k], device_id=to, device_id_type=MESH)

        mine = pltpu.make_async_copy(x_ref, rows(*me), local_sem)
        mine.start()
        first = [copy(0, me, sibling, src=x_ref)]
        first += [copy(1 + j, me, (*chip, c), src=x_ref) for j, chip in enumerate(chips)]
        for cp in first:
            cp.start()
        passed = [copy(4 + j, (*chip, c), sibling) for j, chip in enumerate(chips)]
        for j, chip in enumerate(chips):
            copy(1 + j, (*chip, c), me).wait_recv()
            passed[j].start()
        copy(0, sibling, me).wait_recv()
        for j, chip in enumerate(chips):
            copy(4 + j, (*chip, 1 - c), me).wait_recv()
        for cp in first + passed:
            cp.wait_send()
        mine.wait()

    return pl.pallas_call(
        body, name=name, out_shape=jax.ShapeDtypeStruct((N_DEV * m_per, n), v.dtype),
        in_specs=[ANY], out_specs=ANY,
        scratch_shapes=[pltpu.SemaphoreType.DMA((7,)), pltpu.SemaphoreType.DMA((7,)), pltpu.SemaphoreType.DMA],
    )(v)


def _window(ref, col_sharded, q, lead):
    full = (slice(None),) * lead
    if col_sharded:
        width = ref.shape[-1] // N_CHIP
        return ref.at[full + (slice(None), pl.ds(pl.multiple_of(q * width, LANE), width))]
    height = ref.shape[-2] // N_CHIP
    return ref.at[full + (pl.ds(pl.multiple_of(q * height, HALO), height), slice(None))]


HBM = pl.BlockSpec(memory_space=pltpu.HBM)
SEM = pl.BlockSpec(memory_space=pltpu.SEMAPHORE)
EFFECT = pltpu.SideEffectType.DATAFLOW_SIDE_EFFECTING


def _in_hbm(v):
    return pltpu.with_memory_space_constraint(v, pltpu.HBM)


def _gather_start(bufs, col_sharded, name):
    n = len(bufs)

    def body(*refs):
        ins = refs[:n]
        send_sems, recv_sems = refs[n], refs[n + 1]
        token = refs[-1]
        x, y, c, chips = _place()
        q = 2 * x + y
        for w in range(n):
            for k, chip in enumerate(chips):
                pltpu.make_async_remote_copy(
                    src_ref=_window(ins[w], col_sharded[w], q, 0), dst_ref=_window(ins[w], col_sharded[w], q, 0),
                    send_sem=send_sems.at[3 * w + k], recv_sem=recv_sems.at[3 * w + k],
                    device_id=(*chip, c), device_id_type=MESH).start()
        token[...] = jnp.zeros_like(token)

    out = pl.pallas_call(
        body, name=name,
        out_shape=(pltpu.SemaphoreType.DMA((3 * n,)), pltpu.SemaphoreType.DMA((3 * n,)),
                   *[pltpu.HBM(b.shape, b.dtype) for b in bufs], jax.ShapeDtypeStruct((SUBLANE, LANE), F32)),
        in_specs=(HBM,) * n, out_specs=(SEM, SEM) + (HBM,) * n + (pl.BlockSpec(memory_space=pltpu.VMEM),),
        input_output_aliases={w: 2 + w for w in range(n)},
        compiler_params=pltpu.CompilerParams(has_side_effects=EFFECT),
    )(*[_in_hbm(b) for b in bufs])
    return out[0], out[1], list(out[2:2 + n]), out[-1]


def _gather_wait(send_sems, recv_sems, bufs, col_sharded, after, name):
    n = len(bufs)

    def body(*refs):
        ins = refs[:n]
        send_sems, recv_sems = refs[n], refs[n + 1]
        x, y, c, chips = _place()
        q = 2 * x + y
        for w in range(n):
            for k, (cx, cy) in enumerate(chips):
                cp = pltpu.make_async_remote_copy(
                    src_ref=_window(ins[w], col_sharded[w], q, 0),
                    dst_ref=_window(ins[w], col_sharded[w], 2 * cx + cy, 0),
                    send_sem=send_sems.at[3 * w + k], recv_sem=recv_sems.at[3 * w + k],
                    device_id=(cx, cy, c), device_id_type=MESH)
                cp.wait_send()
                cp.wait_recv()

    out = pl.pallas_call(
        body, name=name, out_shape=tuple(pltpu.HBM(b.shape, b.dtype) for b in bufs),
        in_specs=(HBM,) * n + (SEM, SEM) + (ANY,) * len(after), out_specs=(HBM,) * n,
        input_output_aliases={w: w for w in range(n)},
        compiler_params=pltpu.CompilerParams(has_side_effects=EFFECT),
    )(*bufs, send_sems, recv_sems, *after)
    return list(out)


def _scatter_start(grads, col_sharded, name):
    n = len(grads)
    lands = []
    for g, cs in zip(grads, col_sharded):
        K, N = g.shape
        lands.append(lax.empty((3, K, N // N_CHIP) if cs else (3, K // N_CHIP, N), BF16))

    def body(*refs):
        ins, slots = refs[:n], refs[n:2 * n]
        send_sems, recv_sems = refs[2 * n], refs[2 * n + 1]
        token = refs[-1]
        x, y, c, chips = _place()
        for w in range(n):
            for k, (cx, cy) in enumerate(chips):
                pltpu.make_async_remote_copy(
                    src_ref=_window(ins[w], col_sharded[w], 2 * cx + cy, 0), dst_ref=slots[w].at[k],
                    send_sem=send_sems.at[3 * w + k], recv_sem=recv_sems.at[3 * w + k],
                    device_id=(cx, cy, c), device_id_type=MESH).start()
        token[...] = jnp.zeros_like(token)

    out = pl.pallas_call(
        body, name=name,
        out_shape=(pltpu.SemaphoreType.DMA((3 * n,)), pltpu.SemaphoreType.DMA((3 * n,)),
                   *[pltpu.HBM(b.shape, b.dtype) for b in grads], *[pltpu.HBM(b.shape, b.dtype) for b in lands],
                   jax.ShapeDtypeStruct((SUBLANE, LANE), F32)),
        in_specs=(HBM,) * (2 * n),
        out_specs=(SEM, SEM) + (HBM,) * (2 * n) + (pl.BlockSpec(memory_space=pltpu.VMEM),),
        input_output_aliases={w: 2 + w for w in range(2 * n)},
        compiler_params=pltpu.CompilerParams(has_side_effects=EFFECT),
    )(*[_in_hbm(b) for b in grads], *[_in_hbm(b) for b in lands])
    return out[0], out[1], list(out[2:2 + n]), list(out[2 + n:2 + 2 * n]), out[-1]


def _scatter_wait(send_sems, recv_sems, grads, lands, col_sharded, after, name):
    n = len(grads)

    def body(*refs):
        ins, slots = refs[:n], refs[n:2 * n]
        send_sems, recv_sems = refs[2 * n], refs[2 * n + 1]
        x, y, c, chips = _place()
        for w in range(n):
            for k, (cx, cy) in enumerate(chips):
                cp = pltpu.make_async_remote_copy(
                    src_ref=_window(ins[w], col_sharded[w], 2 * cx + cy, 0), dst_ref=slots[w].at[k],
                    send_sem=send_sems.at[3 * w + k], recv_sem=recv_sems.at[3 * w + k],
                    device_id=(cx, cy, c), device_id_type=MESH)
                cp.wait_send()
                cp.wait_recv()

    out = pl.pallas_call(
        body, name=name, out_shape=tuple(pltpu.HBM(b.shape, b.dtype) for b in list(grads) + list(lands)),
        in_specs=(HBM,) * (2 * n) + (SEM, SEM) + (ANY,) * len(after), out_specs=(HBM,) * (2 * n),
        input_output_aliases={w: w for w in range(2 * n)},
        compiler_params=pltpu.CompilerParams(has_side_effects=EFFECT),
    )(*grads, *lands, send_sems, recv_sems, *after)
    return list(out[:n]), list(out[n:])


def _swap_start(arrays, name):
    n = len(arrays)
    lands = [lax.empty(a.shape, a.dtype) for a in arrays]

    def body(*refs):
        ins, lnd = refs[:n], refs[n:2 * n]
        send_sems, recv_sems = refs[2 * n], refs[2 * n + 1]
        token = refs[-1]
        x, y, c, _ = _place()
        for w in range(n):
            pltpu.make_async_remote_copy(
                src_ref=ins[w], dst_ref=lnd[w], send_sem=send_sems.at[w], recv_sem=recv_sems.at[w],
                device_id=(x, y, 1 - c), device_id_type=MESH).start()
        token[...] = jnp.zeros_like(token)

    out = pl.pallas_call(
        body, name=name,
        out_shape=(pltpu.SemaphoreType.DMA((n,)), pltpu.SemaphoreType.DMA((n,)),
                   *[pltpu.HBM(b.shape, b.dtype) for b in arrays], *[pltpu.HBM(b.shape, b.dtype) for b in lands],
                   jax.ShapeDtypeStruct((SUBLANE, LANE), F32)),
        in_specs=(HBM,) * (2 * n),
        out_specs=(SEM, SEM) + (HBM,) * (2 * n) + (pl.BlockSpec(memory_space=pltpu.VMEM),),
        input_output_aliases={w: 2 + w for w in range(2 * n)},
        compiler_params=pltpu.CompilerParams(has_side_effects=EFFECT),
    )(*[_in_hbm(b) for b in arrays], *[_in_hbm(b) for b in lands])
    return out[0], out[1], list(out[2:2 + n]), list(out[2 + n:2 + 2 * n]), out[-1]


def _swap_wait(send_sems, recv_sems, arrays, lands, after, name):
    n = len(arrays)

    def body(*refs):
        ins, lnd = refs[:n], refs[n:2 * n]
        send_sems, recv_sems = refs[2 * n], refs[2 * n + 1]
        x, y, c, _ = _place()
        for w in range(n):
            cp = pltpu.make_async_remote_copy(
                src_ref=ins[w], dst_ref=lnd[w], send_sem=send_sems.at[w], recv_sem=recv_sems.at[w],
                device_id=(x, y, 1 - c), device_id_type=MESH)
            cp.wait_send()
            cp.wait_recv()

    out = pl.pallas_call(
        body, name=name, out_shape=tuple(pltpu.HBM(b.shape, b.dtype) for b in list(arrays) + list(lands)),
        in_specs=(HBM,) * (2 * n) + (SEM, SEM) + (ANY,) * len(after), out_specs=(HBM,) * (2 * n),
        input_output_aliases={w: w for w in range(2 * n)},
        compiler_params=pltpu.CompilerParams(has_side_effects=EFFECT),
    )(*arrays, *lands, send_sems, recv_sems, *after)
    return list(out[:n]), list(out[n:])


BIG = ("w_in", "w_pa", "w_pb", "w_pc", "w_o", "w_13", "w_2")
BIG_COL_SHARDED = (True, True, True, True, False, True, False)
SMALL = ("b_mod", "g_mix", "gm_ln_g", "gm_ln_b", "gm_w_s", "gm_b_s", "pool_w", "pool_scale", "conv_w",
         "g_ffn", "g_final")
WEIGHTS = ("w_mod", "b_mod", "g_mix", "w_in", "gm_ln_g", "gm_ln_b", "gm_w_s", "gm_b_s", "w_pa", "pool_w",
           "pool_scale", "w_pb", "conv_w", "w_pc", "w_o", "g_ffn", "w_13", "w_2", "g_final")


def _pack(arrays, width):
    flat = jnp.concatenate([a.reshape(-1) for a in arrays])
    rows = -(-flat.shape[0] // width)
    rows = -(-rows // SUBLANE) * SUBLANE
    flat = jnp.pad(flat, (0, rows * width - flat.shape[0]))
    return flat.reshape(rows, width)


def _unpack(packed, shapes):
    flat = packed.reshape(-1)
    out, off = [], 0
    for s in shapes:
        size = 1
        for d in s:
            size *= d
        out.append(flat[off:off + size].reshape(s))
        off += size
    return out


def kernel(x, c, w_mod, b_mod, g_mix, w_in, gm_ln_g, gm_ln_b, gm_w_s, gm_b_s, w_pa, pool_w, pool_scale, w_pb, conv_w, w_pc, w_o, g_ffn, w_13, w_2, g_final, loss_target, m_w_mod, m_b_mod, m_g_mix, m_w_in, m_gm_ln_g, m_gm_ln_b, m_gm_w_s, m_gm_b_s, m_w_pa, m_pool_w, m_pool_scale, m_w_pb, m_conv_w, m_w_pc, m_w_o, m_g_ffn, m_w_13, m_w_2, m_g_final, v_w_mod, v_b_mod, v_g_mix, v_w_in, v_gm_ln_g, v_gm_ln_b, v_gm_w_s, v_gm_b_s, v_w_pa, v_pool_w, v_pool_scale, v_w_pb, v_conv_w, v_w_pc, v_w_o, v_g_ffn, v_w_13, v_w_2, v_g_final):
    W = dict(w_mod=w_mod, b_mod=b_mod, g_mix=g_mix, w_in=w_in, gm_ln_g=gm_ln_g, gm_ln_b=gm_ln_b, gm_w_s=gm_w_s,
             gm_b_s=gm_b_s, w_pa=w_pa, pool_w=pool_w, pool_scale=pool_scale, w_pb=w_pb, conv_w=conv_w, w_pc=w_pc,
             w_o=w_o, g_ffn=g_ffn, w_13=w_13, w_2=w_2, g_final=g_final)
    Mo = dict(w_mod=m_w_mod, b_mod=m_b_mod, g_mix=m_g_mix, w_in=m_w_in, gm_ln_g=m_gm_ln_g, gm_ln_b=m_gm_ln_b,
              gm_w_s=m_gm_w_s, gm_b_s=m_gm_b_s, w_pa=m_w_pa, pool_w=m_pool_w, pool_scale=m_pool_scale, w_pb=m_w_pb,
              conv_w=m_conv_w, w_pc=m_w_pc, w_o=m_w_o, g_ffn=m_g_ffn, w_13=m_w_13, w_2=m_w_2, g_final=m_g_final)
    Vo = dict(w_mod=v_w_mod, b_mod=v_b_mod, g_mix=v_g_mix, w_in=v_w_in, gm_ln_g=v_gm_ln_g, gm_ln_b=v_gm_ln_b,
              gm_w_s=v_gm_w_s, gm_b_s=v_gm_b_s, w_pa=v_w_pa, pool_w=v_pool_w, pool_scale=v_pool_scale, w_pb=v_w_pb,
              conv_w=v_conv_w, w_pc=v_w_pc, w_o=v_w_o, g_ffn=v_g_ffn, w_13=v_w_13, w_2=v_w_2, g_final=v_g_final)

    B, S, D = x.shape
    T = B * S
    L = w_in.shape[0]
    Bg = B * N_DEV
    N4 = w_mod.shape[2]
    CW = conv_w.shape[2]
    xi, yi, ci = lax.axis_index("x"), lax.axis_index("y"), lax.axis_index("c")
    chip = 2 * xi + yi
    dev = 2 * chip + ci

    head = _pack([c, conv_w], D)
    hrows = head.shape[0]
    got = _allgather8(head, "gather_c_conv").reshape(N_DEV, hrows * D)
    c_all = got[:, :B * D].reshape(Bg, D)
    conv_parts = got[:, B * D:B * D + L * 3 * CW].reshape(N_CHIP, 2, L, 3, CW)[:, 0]
    conv_full = jnp.transpose(conv_parts, (1, 2, 0, 3)).reshape(L, 3, N_CHIP * CW)

    b_cols = lax.dynamic_slice_in_dim(b_mod, chip * N4, N4, axis=1).reshape(L, 1, N4)
    mod_part = _mod_fwd(c_all, w_mod, b_cols, "mod_fwd")
    half = Bg // 2
    mine = lax.dynamic_slice_in_dim(mod_part, ci * half, half, axis=1)
    mod_got = _allgather8(mine.reshape(L * half, N4), "gather_mod").reshape(N_CHIP, 2, L, half, N4)
    mod_full = jnp.transpose(mod_got, (2, 1, 3, 0, 4)).reshape(L, Bg, 6, D)
    mod_mine = lax.dynamic_slice_in_dim(mod_full, dev * B, B, axis=1)
    mod = jnp.pad(mod_mine, ((0, 0), (0, 0), (0, SUBLANE - 6), (0, 0)))

    groups = [(("w_in",), 0), (BIG[1:], 0)] + [(BIG, l) for l in range(1, L)]
    gathers = []
    order = (mod,)
    for gi, (names, l) in enumerate(groups):
        cs = [BIG_COL_SHARDED[BIG.index(n)] for n in names]
        bufs = [_cast_into_full(W[n], l, c_, chip, "cast_" + n, deps=order) for n, c_ in zip(names, cs)]
        ss, rs, bufs, tok = _gather_start(bufs, cs, f"gather_start_{gi}")
        gathers.append((ss, rs, bufs, names, cs))
        order = (tok,)

    def gathered(gi, after):
        ss, rs, bufs, names, cs = gathers[gi]
        return dict(zip(names, _gather_wait(ss, rs, bufs, cs, after, f"gather_wait_{gi}")))

    def mixer_params(l):
        vec = jnp.zeros((SUBLANE, BR_W), F32)
        vec = vec.at[0].set(gm_ln_g[l]).at[1].set(gm_ln_b[l]).at[2].set(pool_scale[l])
        conv = jnp.zeros((SUBLANE, BR_W), F32).at[0:3].set(conv_full[l])
        b_s = jnp.zeros((CHUNK, LANE), F32).at[:, 0:HEADS].set(jnp.transpose(gm_b_s[l]))
        return dict(vec=vec, conv=conv, w_s=gm_w_s[l], b_s=b_s, pool_w=pool_w[l])

    xs = x.reshape(T, D)
    saved = []
    for l in range(L):
        prm = mixer_params(l)
        full = gathered(0, list(order)) if l == 0 else gathered(l + 1, [xs])
        z = _normed_matmul(xs, g_mix[l].reshape(1, D), mod[l], 0, 1, full["w_in"], S, "mm_in",
                           deps=order if l == 0 else ())
        cat = _mixer_fwd(z, prm, S, "mixer_fwd")
        if l == 0:
            full.update(gathered(1, [cat]))
        merged = _proj_fwd(cat, z, full["w_pa"], full["w_pb"], full["w_pc"], D, "proj_fwd")
        x1, mo = _matmul(merged, full["w_o"], mode="nn", name="mm_o", resid=(xs, mod[l], 2, S))
        ga, gb, act = _normed_matmul(x1, g_ffn[l].reshape(1, D), mod[l], 3, 4, full["w_13"], S, "mm_13", swiglu=True)
        x2, ffo = _matmul(act, full["w_2"], mode="nn", name="mm_2", resid=(x1, mod[l], 5, S))
        saved.append(dict(prm=prm, full=full, x0=xs, z=z, cat=cat, merged=merged, mo=mo, x1=x1, ga=ga, gb=gb, act=act,
                          ffo=ffo))
        xs = x2

    dx, head_part = _loss_head(xs, loss_target.reshape(T, D), g_final.reshape(1, D), "loss_head")
    loss = lax.psum(head_part[1, 0], ("x", "y", "c"))

    big_grads = {n: [None] * L for n in BIG}
    small_part = {n: [None] * L for n in SMALL if n not in ("b_mod", "g_final")}
    dmod = [None] * L
    scatters = []
    FFN, MIX = ("w_13", "w_2"), ("w_in", "w_pa", "w_pb", "w_pc", "w_o")

    def scatter(names, l, tag):
        cs = [BIG_COL_SHARDED[BIG.index(n)] for n in names]
        ss, rs, gthru, lands, tok = _scatter_start([big_grads[n][l] for n in names], cs, f"scatter_start_{l}{tag}")
        scatters.append((ss, rs, gthru, lands, names, cs, l))
        return (tok,)

    def exchange_small():
        dmod_l = jnp.stack(dmod, axis=0).reshape(L * B, 6 * D)
        rows = -(-(L * B) // SUBLANE) * SUBLANE
        dmod_got = _allgather8(jnp.pad(dmod_l, ((0, rows - L * B), (0, 0))), "gather_dmod")
        local = dict(b_mod=jnp.sum(jnp.stack(dmod, axis=0), axis=1).reshape(L, 6 * D), g_final=head_part[0])
        for n in small_part:
            local[n] = jnp.stack(small_part[n], axis=0)
        packed = _pack([local[n] for n in SMALL], LANE)
        return dmod_got, rows, _allgather8(packed, "gather_small"), [local[n].shape for n in SMALL]

    sent = ()
    for l in reversed(range(L)):
        sv = saved[l]
        full = sv["full"]
        dffo, pg2, da, db = _gated_dgrad(dx, sv["ffo"], mod[l], 5, full["w_2"], S, "mm_2_dgrad",
                                         gab=(sv["ga"], sv["gb"]), deps=sent)
        dh2 = _dgrad_pair(da, db, full["w_13"], "mm_13_dgrad")
        dx1, h2, pb2, pgf = _norm_bwd(sv["x1"], dh2, dx, g_ffn[l].reshape(1, D), mod[l], 3, 4, S, "norm_ffn_bwd")
        big_grads["w_2"][l] = _matmul(sv["act"], dffo, mode="tn", name="mm_2_wgrad")
        big_grads["w_13"][l] = _wgrad_pair(h2, da, db, "mm_13_wgrad")
        sent = scatter(FFN, l, "a")

        dmo, pg1, dmerged = _gated_dgrad(dx1, sv["mo"], mod[l], 2, full["w_o"], S, "mm_o_dgrad", deps=sent)
        big_grads["w_o"][l] = _matmul(sv["merged"], dmo, mode="tn", name="mm_o_wgrad")
        dy, dgl, dcat = _proj_bwd(dmerged, sv["cat"], sv["z"], full["w_pa"], full["w_pb"], full["w_pc"], D,
                                  "proj_bwd")
        for k, n in enumerate(("w_pa", "w_pb", "w_pc")):
            big_grads[n][l] = _matmul(sv["cat"], dy, mode="tn", name="mm_proj_wgrad",
                                      a_cols=(k * BR_W, BR_W), b_cols=(k * D, D))
        dz, pv, dws, dbs, dpw = _mixer_bwd(sv["z"], dcat, dgl, sv["prm"], S, "mixer_bwd")
        dh = _matmul(dz, full["w_in"], mode="nt", name="mm_in_dgrad")
        dx0, h, pb1, pgm = _norm_bwd(sv["x0"], dh, dx1, g_mix[l].reshape(1, D), mod[l], 0, 1, S, "norm_mix_bwd")
        dx = dx0

        dmod[l] = jnp.stack([pb1[:, 0], pb1[:, 1], pg1[:, 0], pb2[:, 0], pb2[:, 1], pg2[:, 0]], axis=1)
        small_part["g_mix"][l] = pgm[0]
        small_part["g_ffn"][l] = pgf[0]
        small_part["gm_ln_g"][l] = pv[0]
        small_part["gm_ln_b"][l] = pv[1]
        small_part["pool_scale"][l] = pv[2]
        small_part["conv_w"][l] = pv[3:6]
        small_part["gm_w_s"][l] = dws
        small_part["gm_b_s"][l] = jnp.transpose(dbs[:, 0:HEADS])
        small_part["pool_w"][l] = dpw

        if l == 0:
            dmod_got, dmod_rows, small_got, pshapes = exchange_small()
            last = (dmod_got, small_got)
        else:
            last = ()
        big_grads["w_in"][l] = _matmul(h, dz, mode="tn", name="mm_in_wgrad", deps=last)
        sent = scatter(MIX, l, "b")
    grad_x = dx.reshape(B, S, D)

    results = {}

    dmod_all = dmod_got.reshape(N_DEV, dmod_rows, 6 * D)[:, :L * B].reshape(N_DEV, L, B, 6 * D)
    dmod_all = jnp.transpose(dmod_all, (1, 0, 2, 3)).reshape(L, Bg, 6 * D)
    dmod_cols = lax.dynamic_slice_in_dim(dmod_all, chip * N4, N4, axis=2)
    g_wmod = _mod_wgrad(c_all, dmod_cols, "mod_wgrad", deps=sent)
    res = _adamw(w_mod.reshape(L * D, N4), m_w_mod.reshape(L * D, N4), v_w_mod.reshape(L * D, N4),
                 [g_wmod.reshape(L * D, N4)], "adamw_w_mod")
    results["w_mod"] = [r.reshape(L, D, N4) for r in res]

    gathered_small = small_got.reshape(N_DEV, small_got.shape[0] // N_DEV, LANE)
    g_small = dict(zip(SMALL, _unpack(_sum_devices(gathered_small, "sum_small"), pshapes)))
    g_small["conv_w"] = lax.dynamic_slice_in_dim(g_small["conv_w"], chip * CW, CW, axis=2)
    wshapes = [W[n].shape for n in SMALL]
    res = _adamw(_pack([W[n] for n in SMALL], LANE), _pack([Mo[n] for n in SMALL], LANE),
                 _pack([Vo[n] for n in SMALL], LANE), [_pack([g_small[n] for n in SMALL], LANE)], "adamw_small")
    small_res = [_unpack(r, wshapes) for r in res]
    for i, n in enumerate(SMALL):
        results[n] = [small_res[j][i] for j in range(4)]

    stacks = {n: lax.empty((W[n].shape[0] * W[n].shape[1], W[n].shape[2]), F32) for n in BIG}
    after = [res[0], results["w_mod"][0]]
    swaps = []
    for part in (FFN, MIX):
        for ss, rs, gthru, lands, names, cs, l in scatters:
            if names != part:
                continue
            gthru, lands = _scatter_wait(ss, rs, gthru, lands, cs, after, f"scatter_wait_{l}_{names[0]}")
            for n, g, ld, c_ in zip(names, gthru, lands, cs):
                stacks[n] = _sum_into(stacks[n], g, ld, l, c_, chip, "sum_" + n)
            after = [stacks[names[-1]]]
        ss, rs, mine, theirs, tok = _swap_start([stacks[n] for n in part], f"swap_start_{part[0]}")
        swaps.append((ss, rs, mine, theirs, part))
        after = after + [tok]
    for ss, rs, mine, theirs, part in swaps:
        mine, theirs = _swap_wait(ss, rs, mine, theirs, after, f"swap_wait_{part[0]}")
        for n, own, other in zip(part, mine, theirs):
            _, R, C = W[n].shape
            res = _adamw(W[n].reshape(L * R, C), Mo[n].reshape(L * R, C), Vo[n].reshape(L * R, C), [own, other],
                         "adamw_" + n)
            results[n] = [r.reshape(L, R, C) for r in res]
        after = [res[0]]

    return (loss, grad_x, *[results[n][0] for n in WEIGHTS], *[results[n][1] for n in WEIGHTS],
            *[results[n][2] for n in WEIGHTS], *[results[n][3] for n in WEIGHTS])
```

```python
import jax
import jax.numpy as jnp
from jax import lax
from jax.experimental import pallas as pl
from jax.experimental.pallas import tpu as pltpu

F32 = jnp.float32
BF16 = jnp.bfloat16
MESH = pl.DeviceIdType.MESH

EPS = 1e-6
CHUNK = 128
HEADS = 4
HEAD_DIM = 128
BR_W = 512
N_GROUP = 4
GROUP_DIM = 128
HALO = 16
N_SPLIT = 6 * BR_W
N_CHIP = 4
N_DEV = 8

ADAM_LR = 0.001
ADAM_B1 = 0.9
ADAM_B2 = 0.999
ADAM_EPS = 1e-08
ADAM_WD = 0.01
ADAM_STEP = 10

V7X_VMEM_LIMIT = 56 * 1024 * 1024
LANE = 128
SUBLANE = 8

GELU_K = 0.7978845608028654
GELU_C = 0.044715


def _cparams(sem):
    return pltpu.CompilerParams(dimension_semantics=sem, vmem_limit_bytes=V7X_VMEM_LIMIT)


def _pick(n, cap, q=LANE):
    best = None
    d = q
    while d <= min(n, cap):
        if n % d == 0:
            best = d
        d += q
    return n if best is None else best


def _sigmoid(x):
    return 0.5 * jnp.tanh(0.5 * x) + 0.5


def _gelu(x):
    t = jnp.tanh(GELU_K * (x + GELU_C * x * x * x))
    return 0.5 * x * (1.0 + t), t


def _gelu_grad(x, t):
    return 0.5 * (1.0 + t) + 0.5 * x * (1.0 - t * t) * GELU_K * (1.0 + 3.0 * GELU_C * x * x)


def _matmul(a, b, *, mode, name, out_dtype=None, layer=None, a_cols=None, b_cols=None,
            resid=None, deps=(), tm_cap=None, tn_cap=1536, tk_cap=1536):
    out_dtype = BF16 if out_dtype is None else out_dtype
    b2 = b.shape[-2:]
    if tm_cap is None:
        k_len = a.shape[0] if mode == "tn" else a.shape[1]
        if mode == "tn":
            tm_cap, tk_cap = 1536, 2048
        else:
            tm_cap, tk_cap = (512, 8192) if k_len > 1536 else (1024, 1536)
    if mode == "nn":
        M, K = a.shape
        N = b2[1]
    elif mode == "nt":
        M, K = a.shape
        N = b2[0]
    else:
        K = a.shape[0]
        M = a.shape[1] if a_cols is None else a_cols[1]
        N = b2[1] if b_cols is None else b_cols[1]
    tm = _pick(M if resid is None else resid[3], tm_cap)
    tn = _pick(N, tn_cap)
    tk = _pick(K, tk_cap)
    nk = K // tk
    a_off = 0 if a_cols is None else a_cols[0] // tm
    b_off = 0 if b_cols is None else b_cols[0] // tn
    if a_cols is not None:
        assert a_cols[0] % tm == 0
    if b_cols is not None:
        assert b_cols[0] % tn == 0

    if mode == "nn":
        a_spec = pl.BlockSpec((tm, tk), lambda i, j, k: (i, k))
        b_blk, b_idx = (tk, tn), (lambda i, j, k: (k, j))
        dims = (((1,), (0,)), ((), ()))
    elif mode == "nt":
        a_spec = pl.BlockSpec((tm, tk), lambda i, j, k: (i, k))
        b_blk, b_idx = (tn, tk), (lambda i, j, k: (j, k))
        dims = (((1,), (1,)), ((), ()))
    else:
        a_spec = pl.BlockSpec((tk, tm), lambda i, j, k: (k, i + a_off))
        b_blk, b_idx = (tk, tn), (lambda i, j, k: (k, j + b_off))
        dims = (((0,), (0,)), ((), ()))
    if layer is None:
        b_spec = pl.BlockSpec(b_blk, b_idx)
    else:
        b_spec = pl.BlockSpec((None,) + b_blk, lambda i, j, k: (layer,) + b_idx(i, j, k))

    in_specs = [a_spec, b_spec]
    operands = [a, b]
    o_spec = pl.BlockSpec((tm, tn), lambda i, j, k: (i, j))
    if resid is not None:
        x, mod, row, seq = resid
        D = mod.shape[-1]
        in_specs += [o_spec, pl.BlockSpec((1, SUBLANE, tn), lambda i, j, k: ((i * tm) // seq, 0, j))]
        operands += [x, mod]
        out_shape = (jax.ShapeDtypeStruct((M, N), F32), jax.ShapeDtypeStruct((M, N), BF16))
        out_specs = (o_spec, o_spec)
        assert seq % tm == 0 and D == N
    else:
        out_shape = jax.ShapeDtypeStruct((M, N), out_dtype)
        out_specs = o_spec

    def finish(acc, refs):
        if resid is not None:
            x_ref, mod_ref, o_ref, p_ref = refs
            o_ref[...] = x_ref[...] + mod_ref[0, row:row + 1, :] * acc
            p_ref[...] = acc.astype(BF16)
        else:
            (o_ref,) = refs
            o_ref[...] = acc.astype(out_dtype)

    n_in = len(operands) - 2
    in_specs += [ANY] * len(deps)
    operands += list(deps)

    def body(a_ref, b_ref, *refs):
        refs = refs[:n_in] + refs[n_in + len(deps):]
        part = lax.dot_general(a_ref[...], b_ref[...], dims, preferred_element_type=F32)
        if nk == 1:
            finish(part, refs)
            return
        acc_ref = refs[-1]
        k = pl.program_id(2)

        @pl.when(k == 0)
        def _():
            acc_ref[...] = part

        @pl.when(k > 0)
        def _():
            acc_ref[...] += part

        @pl.when(k == nk - 1)
        def _():
            finish(acc_ref[...], refs[:-1])

    scratch = [] if nk == 1 else [pltpu.VMEM((tm, tn), F32)]
    return pl.pallas_call(
        body, name=name, grid=(M // tm, N // tn, nk), in_specs=in_specs, out_specs=out_specs,
        out_shape=out_shape, scratch_shapes=scratch,
        compiler_params=_cparams(("parallel", "parallel", "arbitrary")),
    )(*operands)


def _row_tile(seq, cap):
    return _pick(seq, cap, HALO)


def _mod_spec(tm, seq, D):
    return pl.BlockSpec((1, SUBLANE, D), lambda i: ((i * tm) // seq, 0, 0))


def _normed_matmul(x, g, mod, shift_row, scale_row, w, seq, name, swiglu=False, deps=()):
    T, D = x.shape
    N = w.shape[1] // 2 if swiglu else w.shape[1]
    tm = _pick(seq, 256 if swiglu else 1024)
    tn = _pick(N, 4096 if swiglu else 1536)
    nj = N // tn
    n_w = 2 if swiglu else 1

    def body(x_ref, g_ref, mod_ref, *rest):
        w_refs = rest[:n_w]
        outs = rest[n_w + len(deps):-1]
        h_ref = rest[-1]

        @pl.when(pl.program_id(1) == 0)
        def _():
            xv = x_ref[...]
            rstd = lax.rsqrt(jnp.mean(xv * xv, axis=-1, keepdims=True) + EPS)
            n = xv * rstd * g_ref[...]
            h = n * (1.0 + mod_ref[0, scale_row:scale_row + 1, :]) + mod_ref[0, shift_row:shift_row + 1, :]
            h_ref[...] = h.astype(BF16)

        h = h_ref[...]
        if not swiglu:
            outs[0][...] = jnp.dot(h, w_refs[0][...], preferred_element_type=F32).astype(BF16)
            return
        a16 = jnp.dot(h, w_refs[0][...], preferred_element_type=F32).astype(BF16)
        b16 = jnp.dot(h, w_refs[1][...], preferred_element_type=F32).astype(BF16)
        outs[0][...] = a16
        outs[1][...] = b16
        outs[2][...] = a16 * _sigmoid(a16) * b16

    w_specs = [pl.BlockSpec((D, tn), lambda i, j: (0, j))]
    if swiglu:
        w_specs.append(pl.BlockSpec((D, tn), lambda i, j: (0, j + nj)))
    o_spec = pl.BlockSpec((tm, tn), lambda i, j: (i, j))
    n_out = 3 if swiglu else 1
    out = pl.pallas_call(
        body, name=name, grid=(T // tm, nj),
        in_specs=[pl.BlockSpec((tm, D), lambda i, j: (i, 0)), pl.BlockSpec((1, D), lambda i, j: (0, 0)),
                  pl.BlockSpec((1, SUBLANE, D), lambda i, j: ((i * tm) // seq, 0, 0))] + w_specs + [ANY] * len(deps),
        out_specs=(o_spec,) * n_out, out_shape=(jax.ShapeDtypeStruct((T, N), BF16),) * n_out,
        scratch_shapes=[pltpu.VMEM((tm, D), BF16)],
        compiler_params=_cparams(("parallel", "arbitrary")),
    )(x, g, mod, *([w] * n_w), *deps)
    return out if swiglu else out[0]


def _dgrad_norm_bwd(parts, w, x, dres, g, mod, shift_row, scale_row, seq, name):
    T, D = x.shape
    B = mod.shape[0]
    n = len(parts)
    Kp = parts[0].shape[1]
    tm = _pick(seq, 256)
    per_seq = seq // tm

    def body(*refs):
        p_refs, w_refs = refs[:len(parts)], refs[len(parts):2 * len(parts)]
        x_ref, dres_ref, g_ref, mod_ref, dx_ref, h_ref, pb_ref, pg_ref = refs[2 * len(parts):]
        i = pl.program_id(0)
        dims = (((1,), (1,)), ((), ()))
        dhv = lax.dot_general(p_refs[0][...], w_refs[0][...], dims, preferred_element_type=F32)
        for p_ref, w_ref in zip(p_refs[1:], w_refs[1:]):
            dhv = dhv + lax.dot_general(p_ref[...], w_ref[...], dims, preferred_element_type=F32)
        xv = x_ref[...]
        gv = g_ref[...]
        scale1 = 1.0 + mod_ref[0, scale_row:scale_row + 1, :]
        rstd = lax.rsqrt(jnp.mean(xv * xv, axis=-1, keepdims=True) + EPS)
        xhat = xv * rstd
        n = xhat * gv
        dn = dhv * scale1
        dxhat = dn * gv
        dx = rstd * (dxhat - xhat * jnp.mean(dxhat * xhat, axis=-1, keepdims=True))
        dx_ref[...] = dres_ref[...] + dx
        h_ref[...] = (n * scale1 + mod_ref[0, shift_row:shift_row + 1, :]).astype(BF16)

        @pl.when(i % per_seq == 0)
        def _():
            pb_ref[...] = jnp.zeros_like(pb_ref)

        @pl.when(i == 0)
        def _():
            pg_ref[...] = jnp.zeros_like(pg_ref)

        pb_ref[0, 0:1, :] += jnp.sum(dhv, axis=0, keepdims=True)
        pb_ref[0, 1:2, :] += jnp.sum(dhv * n, axis=0, keepdims=True)
        pg_ref[0:1, :] += jnp.sum(dn * xhat, axis=0, keepdims=True)

    row = pl.BlockSpec((tm, D), lambda i: (i, 0))
    p_specs = [pl.BlockSpec((tm, Kp), lambda i: (i, 0))] * n
    w_specs = [pl.BlockSpec((D, Kp), lambda i, p=p: (0, p)) for p in range(n)]
    return pl.pallas_call(
        body, name=name, grid=(T // tm,),
        in_specs=p_specs + w_specs + [row, row, pl.BlockSpec((1, D), lambda i: (0, 0)), _mod_spec(tm, seq, D)],
        out_specs=(row, row, _mod_spec(tm, seq, D), pl.BlockSpec((SUBLANE, D), lambda i: (0, 0))),
        out_shape=(jax.ShapeDtypeStruct((T, D), F32), jax.ShapeDtypeStruct((T, D), BF16),
                   jax.ShapeDtypeStruct((B, SUBLANE, D), F32), jax.ShapeDtypeStruct((SUBLANE, D), F32)),
        compiler_params=_cparams(("arbitrary",)),
    )(*parts, *([w] * n), x, dres, g, mod)


def _gated_dgrad(dx, prod, mod, gate_row, w, seq, name, gab=None, deps=()):
    T, D = dx.shape
    B = mod.shape[0]
    N = w.shape[0]
    tm = _pick(seq, 512 if gab is None else 256)
    tn = _pick(N, 4096)
    per_seq = seq // tm
    n_gab = 0 if gab is None else 2

    def body(dx_ref, p_ref, mod_ref, w_ref, *rest):
        gab_refs = rest[:n_gab]
        dp_ref, pb_ref = rest[n_gab + len(deps):n_gab + len(deps) + 2]
        outs = rest[n_gab + len(deps) + 2:-1]
        a_ref = rest[-1]
        i = pl.program_id(0)

        @pl.when(pl.program_id(1) == 0)
        def _():
            dxv = dx_ref[...]
            dp = (dxv * mod_ref[0, gate_row:gate_row + 1, :]).astype(BF16)
            a_ref[...] = dp
            dp_ref[...] = dp

            @pl.when(i % per_seq == 0)
            def _():
                pb_ref[...] = jnp.zeros_like(pb_ref)

            pb_ref[0, 0:1, :] += jnp.sum(dxv * p_ref[...].astype(F32), axis=0, keepdims=True)

        du = lax.dot_general(a_ref[...], w_ref[...], (((1,), (1,)), ((), ())), preferred_element_type=F32)
        if gab is None:
            outs[0][...] = du.astype(BF16)
            return
        du = du.astype(BF16)
        a = gab_refs[0][...]
        b = gab_refs[1][...]
        sg = _sigmoid(a)
        dsg = du * sg
        outs[0][...] = dsg * b * (1.0 + a * (1.0 - sg))
        outs[1][...] = dsg * a

    row = pl.BlockSpec((tm, D), lambda i, j: (i, 0))
    tile = pl.BlockSpec((tm, tn), lambda i, j: (i, j))
    mod_spec = pl.BlockSpec((1, SUBLANE, D), lambda i, j: ((i * tm) // seq, 0, 0))
    n_out = 1 if gab is None else 2
    out = pl.pallas_call(
        body, name=name, grid=(T // tm, N // tn),
        in_specs=[row, row, mod_spec, pl.BlockSpec((tn, D), lambda i, j: (j, 0))] + [tile] * n_gab
        + [ANY] * len(deps),
        out_specs=(row, mod_spec) + (tile,) * n_out,
        out_shape=(jax.ShapeDtypeStruct((T, D), BF16), jax.ShapeDtypeStruct((B, SUBLANE, D), F32))
        + (jax.ShapeDtypeStruct((T, N), BF16),) * n_out,
        scratch_shapes=[pltpu.VMEM((tm, D), BF16)],
        compiler_params=_cparams(("arbitrary", "arbitrary")),
    )(dx, prod, mod, w, *(gab or ()), *deps)
    return out


def _wgrad_pair(h, da, db, name):
    T, D = h.shape
    Fh = da.shape[1]
    tn = _pick(Fh, 1536)
    tk = _pick(T, 2048)
    half = Fh // tn
    nk = T // tk

    def body(h_ref, da_ref, db_ref, o_ref, acc_ref):
        j, k = pl.program_id(0), pl.program_id(1)
        dims = (((0,), (0,)), ((), ()))

        def accumulate(g_ref):
            part = lax.dot_general(h_ref[...], g_ref[...], dims, preferred_element_type=F32)

            @pl.when(k == 0)
            def _():
                acc_ref[...] = part

            @pl.when(k > 0)
            def _():
                acc_ref[...] += part

        @pl.when(j < half)
        def _():
            accumulate(da_ref)

        @pl.when(j >= half)
        def _():
            accumulate(db_ref)

        @pl.when(k == nk - 1)
        def _():
            o_ref[...] = acc_ref[...].astype(BF16)

    a_idx = lambda j, k: (jnp.where(j < half, k, 0), jnp.minimum(j, half - 1))
    b_idx = lambda j, k: (jnp.where(j >= half, k, 0), jnp.maximum(j - half, 0))
    return pl.pallas_call(
        body, name=name, grid=(2 * half, nk),
        in_specs=[pl.BlockSpec((tk, D), lambda j, k: (k, 0)), pl.BlockSpec((tk, tn), a_idx),
                  pl.BlockSpec((tk, tn), b_idx)],
        out_specs=pl.BlockSpec((D, tn), lambda j, k: (0, j)),
        out_shape=jax.ShapeDtypeStruct((D, 2 * Fh), BF16), scratch_shapes=[pltpu.VMEM((D, tn), F32)],
        compiler_params=_cparams(("parallel", "arbitrary")),
    )(h, da, db)


def _shift_down(v, d):
    return pltpu.roll(v, d, 0)


def _shift_up(v, d):
    return pltpu.roll(v, v.shape[0] - d, 0)


def _tril_mask():
    r = lax.broadcasted_iota(jnp.int32, (CHUNK, CHUNK), 0)
    c = lax.broadcasted_iota(jnp.int32, (CHUNK, CHUNK), 1)
    return c <= r


def _pool_counts(i, tm, seq, rows, first_row):
    r = lax.broadcasted_iota(jnp.int32, (rows, 1), 0) + (i * tm + first_row)
    pos1 = (r % seq + 1).astype(F32)
    return [jnp.minimum(pos1, float(2 << g)) for g in range(N_GROUP)]


def _mixer_forward_values(zt, hxb, hch, i, tm, seq, ln_g, ln_b, ws_ref, bs_ref, pw_ref, pscale, cw_ref):
    u = zt[:, 0 * BR_W:1 * BR_W]
    v = zt[:, 1 * BR_W:2 * BR_W]
    xb = zt[:, 2 * BR_W:3 * BR_W]
    bg = zt[:, 3 * BR_W:4 * BR_W]
    cg = zt[:, 4 * BR_W:5 * BR_W]
    hc = zt[:, 5 * BR_W:6 * BR_W]
    out = {}

    ug, tu = _gelu(u)
    vg, tv = _gelu(v)
    mu = jnp.mean(vg, axis=-1, keepdims=True)
    vc = vg - mu
    rstd = lax.rsqrt(jnp.mean(vc * vc, axis=-1, keepdims=True) + EPS)
    vhat = vc * rstd
    vn = (vhat * ln_g + ln_b).astype(BF16)
    mask = _tril_mask()
    wt = [jnp.where(mask, ws_ref[h], 0.0).astype(BF16) for h in range(HEADS)]
    rows = []
    for n in range(tm // CHUNK):
        blocks = []
        for h in range(HEADS):
            blk = vn[n * CHUNK:(n + 1) * CHUNK, h * HEAD_DIM:(h + 1) * HEAD_DIM]
            sb = jnp.dot(wt[h], blk, preferred_element_type=F32) + bs_ref[:, h:h + 1]
            blocks.append(sb)
        rows.append(jnp.concatenate(blocks, axis=1))
    s = jnp.concatenate(rows, axis=0) if len(rows) > 1 else rows[0]
    out.update(u=u, v=v, ug=ug, tu=tu, tv=tv, rstd=rstd, vhat=vhat, vn=vn, wt=wt, s=s, a_out=ug * s)

    ext = jnp.concatenate([hxb, xb], axis=0)
    cnt = _pool_counts(i, tm, seq, tm, 0)
    p, qs = [], []
    for g in range(N_GROUP):
        e = ext[:, g * GROUP_DIM:(g + 1) * GROUP_DIM]
        acc = e
        for d in (1, 2, 4, 8)[:g + 1]:
            acc = acc + _shift_down(acc, d)
        pg = acc[HALO:, :] / cnt[g] - xb[:, g * GROUP_DIM:(g + 1) * GROUP_DIM]
        p.append(pg.astype(BF16))
        qs.append(jnp.dot(p[g], pw_ref[g].astype(BF16), preferred_element_type=F32))
    q = jnp.concatenate(qs, axis=1)
    out.update(p=p, q=q, b_out=q * pscale)

    zc = cg * hc
    zce = jnp.concatenate([hch[:, :BR_W] * hch[:, BR_W:], zc], axis=0)
    z1 = _shift_down(zce, 1)[HALO:, :]
    z2 = _shift_down(zce, 2)[HALO:, :]
    y = cw_ref[0:1, :] * z2 + cw_ref[1:2, :] * z1 + cw_ref[2:3, :] * zc
    out.update(bg=bg, cg=cg, hc=hc, zc=zc, z1=z1, z2=z2, y=y, c_out=bg * y)
    return out


def _mixer_specs(tm, T):
    nb = T // HALO
    per = tm // HALO
    prev = lambda i: jnp.maximum(i * per - 1, 0)
    nxt = lambda i: jnp.minimum((i + 1) * per, nb - 1)
    return prev, nxt


def _mixer_fwd(z, prm, seq, name):
    T = z.shape[0]
    tm = _row_tile(seq, 256)
    per_seq = seq // tm
    prev, _ = _mixer_specs(tm, T)

    def body(z_ref, hxb_ref, hch_ref, vec_ref, cw_ref, ws_ref, bs_ref, pw_ref, cat_ref):
        i = pl.program_id(0)
        keep = jnp.where(i % per_seq == 0, 0.0, 1.0)
        zt = z_ref[...].astype(F32)
        hxb = hxb_ref[...].astype(F32) * keep
        hch = hch_ref[...].astype(F32) * keep
        o = _mixer_forward_values(zt, hxb, hch, i, tm, seq, vec_ref[0:1, :], vec_ref[1:2, :],
                                  ws_ref, bs_ref, pw_ref, vec_ref[2:3, :], cw_ref)
        cat_ref[:, 0 * BR_W:1 * BR_W] = o["a_out"].astype(BF16)
        cat_ref[:, 1 * BR_W:2 * BR_W] = o["b_out"].astype(BF16)
        cat_ref[:, 2 * BR_W:3 * BR_W] = o["c_out"].astype(BF16)

    full = lambda shape: pl.BlockSpec(shape, lambda i: (0,) * len(shape))
    return pl.pallas_call(
        body, name=name, grid=(T // tm,),
        in_specs=[pl.BlockSpec((tm, N_SPLIT), lambda i: (i, 0)),
                  pl.BlockSpec((HALO, BR_W), lambda i: (prev(i), 2)),
                  pl.BlockSpec((HALO, 2 * BR_W), lambda i: (prev(i), 2)),
                  full((SUBLANE, BR_W)), full((SUBLANE, BR_W)), full((HEADS, CHUNK, CHUNK)),
                  full((CHUNK, LANE)), full((N_GROUP, GROUP_DIM, GROUP_DIM))],
        out_specs=pl.BlockSpec((tm, 3 * BR_W), lambda i: (i, 0)),
        out_shape=jax.ShapeDtypeStruct((T, 3 * BR_W), BF16),
        compiler_params=_cparams(("parallel",)),
    )(z, z, z, prm["vec"], prm["conv"], prm["w_s"], prm["b_s"], prm["pool_w"])


def _mixer_bwd(z, dcat, dgl, prm, seq, name):
    T, IN = z.shape
    GL = IN - N_SPLIT
    tm = _row_tile(seq, 256)
    per_seq = seq // tm
    prev, nxt = _mixer_specs(tm, T)
    nrow = tm + HALO

    def body(z_ref, hxb_ref, hch_ref, nbg_ref, dcat_ref, ndb_ref, ndc_ref, dgl_ref,
             vec_ref, cw_ref, ws_ref, bs_ref, pw_ref,
             dz_ref, pv_ref, dws_ref, dbs_ref, dpw_ref):
        i = pl.program_id(0)
        keep_prev = jnp.where(i % per_seq == 0, 0.0, 1.0)
        keep_next = jnp.where(i % per_seq == per_seq - 1, 0.0, 1.0)
        zt = z_ref[...].astype(F32)
        hxb = hxb_ref[...].astype(F32) * keep_prev
        hch = hch_ref[...].astype(F32) * keep_prev
        ln_g = vec_ref[0:1, :]
        pscale = vec_ref[2:3, :]
        o = _mixer_forward_values(zt, hxb, hch, i, tm, seq, ln_g, vec_ref[1:2, :],
                                  ws_ref, bs_ref, pw_ref, pscale, cw_ref)
        dcv = dcat_ref[...].astype(F32)
        da = dcv[:, 0 * BR_W:1 * BR_W]
        db = dcv[:, 1 * BR_W:2 * BR_W]
        dc = dcv[:, 2 * BR_W:3 * BR_W]
        mask = _tril_mask()

        @pl.when(i == 0)
        def _():
            pv_ref[...] = jnp.zeros_like(pv_ref)
            dws_ref[...] = jnp.zeros_like(dws_ref)
            dbs_ref[...] = jnp.zeros_like(dbs_ref)
            dpw_ref[...] = jnp.zeros_like(dpw_ref)

        d_ug = da * o["s"]
        ds = da * o["ug"]
        ds_b = ds.astype(BF16)
        vn = o["vn"]
        dvn_rows = []
        dws = [jnp.zeros((CHUNK, CHUNK), F32) for _ in range(HEADS)]
        dsum = jnp.zeros((CHUNK, BR_W), F32)
        for n in range(tm // CHUNK):
            blocks = []
            rs = slice(n * CHUNK, (n + 1) * CHUNK)
            dsum = dsum + ds[rs, :]
            for h in range(HEADS):
                cs = slice(h * HEAD_DIM, (h + 1) * HEAD_DIM)
                dsb = ds_b[rs, cs]
                blocks.append(lax.dot_general(o["wt"][h], dsb, (((0,), (0,)), ((), ())),
                                              preferred_element_type=F32))
                dws[h] = dws[h] + lax.dot_general(dsb, vn[rs, cs], (((1,), (1,)), ((), ())),
                                                  preferred_element_type=F32)
            dvn_rows.append(jnp.concatenate(blocks, axis=1))
        dvn = jnp.concatenate(dvn_rows, axis=0) if len(dvn_rows) > 1 else dvn_rows[0]
        lane = lax.broadcasted_iota(jnp.int32, (CHUNK, LANE), 1)
        dbs_t = jnp.zeros((CHUNK, LANE), F32)
        for h in range(HEADS):
            dws_ref[h] += jnp.where(mask, dws[h], 0.0)
            rsum = jnp.sum(dsum[:, h * HEAD_DIM:(h + 1) * HEAD_DIM], axis=1, keepdims=True)
            dbs_t = dbs_t + jnp.where(lane == h, rsum, 0.0)
        dbs_ref[...] += dbs_t
        vhat = o["vhat"]
        pv_ref[0:1, :] += jnp.sum(dvn * vhat, axis=0, keepdims=True)
        pv_ref[1:2, :] += jnp.sum(dvn, axis=0, keepdims=True)
        dvhat = dvn * ln_g
        dvg = o["rstd"] * (dvhat - jnp.mean(dvhat, axis=-1, keepdims=True)
                           - vhat * jnp.mean(dvhat * vhat, axis=-1, keepdims=True))
        du = d_ug * _gelu_grad(o["u"], o["tu"])
        dv = dvg * _gelu_grad(o["v"], o["tv"])

        pv_ref[2:3, :] += jnp.sum(db * o["q"], axis=0, keepdims=True)
        dq = (db * pscale).astype(BF16)
        dqn = (ndb_ref[...].astype(F32) * pscale * keep_next).astype(BF16)
        cnt = _pool_counts(i, tm, seq, nrow, 0)
        dxb = []
        for g in range(N_GROUP):
            cs = slice(g * GROUP_DIM, (g + 1) * GROUP_DIM)
            pwg = pw_ref[g].astype(BF16)
            dpw_ref[g] += lax.dot_general(o["p"][g], dq[:, cs], (((0,), (0,)), ((), ())),
                                          preferred_element_type=F32)
            dp = lax.dot_general(dq[:, cs], pwg, (((1,), (1,)), ((), ())), preferred_element_type=F32)
            dpn = lax.dot_general(dqn[:, cs], pwg, (((1,), (1,)), ((), ())), preferred_element_type=F32)
            acc = jnp.concatenate([dp, dpn], axis=0) / cnt[g]
            for d in (1, 2, 4, 8)[:g + 1]:
                acc = acc + _shift_up(acc, d)
            dxb.append(acc[:tm, :] - dp)
        dxb = jnp.concatenate(dxb, axis=1)

        dbg = dc * o["y"]
        dy = dc * o["bg"]
        pv_ref[3:4, :] += jnp.sum(dy * o["z2"], axis=0, keepdims=True)
        pv_ref[4:5, :] += jnp.sum(dy * o["z1"], axis=0, keepdims=True)
        pv_ref[5:6, :] += jnp.sum(dy * o["zc"], axis=0, keepdims=True)
        dyn = ndc_ref[...].astype(F32) * nbg_ref[...].astype(F32) * keep_next
        dye = jnp.concatenate([dy, dyn], axis=0)
        dzc = (cw_ref[2:3, :] * dy + cw_ref[1:2, :] * _shift_up(dye, 1)[:tm, :]
               + cw_ref[0:1, :] * _shift_up(dye, 2)[:tm, :])
        dcg = dzc * o["hc"]
        dhc = dzc * o["cg"]

        dz_ref[:, 0 * BR_W:1 * BR_W] = du.astype(BF16)
        dz_ref[:, 1 * BR_W:2 * BR_W] = dv.astype(BF16)
        dz_ref[:, 2 * BR_W:3 * BR_W] = dxb.astype(BF16)
        dz_ref[:, 3 * BR_W:4 * BR_W] = dbg.astype(BF16)
        dz_ref[:, 4 * BR_W:5 * BR_W] = dcg.astype(BF16)
        dz_ref[:, 5 * BR_W:6 * BR_W] = dhc.astype(BF16)
        dz_ref[:, N_SPLIT:] = dgl_ref[...]

    full = lambda shape: pl.BlockSpec(shape, lambda i: (0,) * len(shape))
    return pl.pallas_call(
        body, name=name, grid=(T // tm,),
        in_specs=[pl.BlockSpec((tm, N_SPLIT), lambda i: (i, 0)),
                  pl.BlockSpec((HALO, BR_W), lambda i: (prev(i), 2)),
                  pl.BlockSpec((HALO, 2 * BR_W), lambda i: (prev(i), 2)),
                  pl.BlockSpec((HALO, BR_W), lambda i: (nxt(i), 3)),
                  pl.BlockSpec((tm, 3 * BR_W), lambda i: (i, 0)),
                  pl.BlockSpec((HALO, BR_W), lambda i: (nxt(i), 1)),
                  pl.BlockSpec((HALO, BR_W), lambda i: (nxt(i), 2)),
                  pl.BlockSpec((tm, GL), lambda i: (i, 0)),
                  full((SUBLANE, BR_W)), full((SUBLANE, BR_W)), full((HEADS, CHUNK, CHUNK)),
                  full((CHUNK, LANE)), full((N_GROUP, GROUP_DIM, GROUP_DIM))],
        out_specs=(pl.BlockSpec((tm, IN), lambda i: (i, 0)),
                   full((SUBLANE, BR_W)), full((HEADS, CHUNK, CHUNK)), full((CHUNK, LANE)),
                   full((N_GROUP, GROUP_DIM, GROUP_DIM))),
        out_shape=(jax.ShapeDtypeStruct((T, IN), BF16),
                   jax.ShapeDtypeStruct((SUBLANE, BR_W), F32),
                   jax.ShapeDtypeStruct((HEADS, CHUNK, CHUNK), F32),
                   jax.ShapeDtypeStruct((CHUNK, LANE), F32),
                   jax.ShapeDtypeStruct((N_GROUP, GROUP_DIM, GROUP_DIM), F32)),
        compiler_params=_cparams(("arbitrary",)),
    )(z, z, z, z, dcat, dcat, dcat, dgl, prm["vec"], prm["conv"], prm["w_s"], prm["b_s"], prm["pool_w"])


def _proj_fwd(cat, z, w_pa, w_pb, w_pc, D, name):
    T, IN = z.shape
    GL = 3 * D
    assert N_SPLIT % GL == 0
    glb = N_SPLIT // GL
    tm = _pick(T, 256, HALO)

    def body(cat_ref, gl_ref, wa_ref, wb_ref, wc_ref, m_ref):
        acc = jnp.zeros((tm, D), F32)
        for k, w_ref in enumerate((wa_ref, wb_ref, wc_ref)):
            y = jnp.dot(cat_ref[:, k * BR_W:(k + 1) * BR_W], w_ref[...], preferred_element_type=F32)
            acc = acc + _sigmoid(gl_ref[:, k * D:(k + 1) * D]).astype(F32) * y
        m_ref[...] = acc.astype(BF16)

    wspec = pl.BlockSpec((BR_W, D), lambda i: (0, 0))
    return pl.pallas_call(
        body, name=name, grid=(T // tm,),
        in_specs=[pl.BlockSpec((tm, 3 * BR_W), lambda i: (i, 0)),
                  pl.BlockSpec((tm, GL), lambda i: (i, glb)), wspec, wspec, wspec],
        out_specs=pl.BlockSpec((tm, D), lambda i: (i, 0)),
        out_shape=jax.ShapeDtypeStruct((T, D), BF16),
        compiler_params=_cparams(("parallel",)),
    )(cat, z, w_pa, w_pb, w_pc)


def _proj_bwd(dmerged, cat, z, w_pa, w_pb, w_pc, D, name):
    T, IN = z.shape
    GL = 3 * D
    glb = N_SPLIT // GL
    tm = _pick(T, 256, HALO)

    def body(dm_ref, cat_ref, gl_ref, wa_ref, wb_ref, wc_ref, dy_ref, dgl_ref, dcat_ref):
        dm = dm_ref[...]
        for k, w_ref in enumerate((wa_ref, wb_ref, wc_ref)):
            w = w_ref[...]
            y = jnp.dot(cat_ref[:, k * BR_W:(k + 1) * BR_W], w, preferred_element_type=F32).astype(BF16)
            sg = _sigmoid(gl_ref[:, k * D:(k + 1) * D])
            dyk = dm * sg
            dy_ref[:, k * D:(k + 1) * D] = dyk
            dgl_ref[:, k * D:(k + 1) * D] = dyk * y * (1.0 - sg)
            dcat_ref[:, k * BR_W:(k + 1) * BR_W] = lax.dot_general(
                dyk, w, (((1,), (1,)), ((), ())), preferred_element_type=F32).astype(BF16)

    wspec = pl.BlockSpec((BR_W, D), lambda i: (0, 0))
    return pl.pallas_call(
        body, name=name, grid=(T // tm,),
        in_specs=[pl.BlockSpec((tm, D), lambda i: (i, 0)),
                  pl.BlockSpec((tm, 3 * BR_W), lambda i: (i, 0)),
                  pl.BlockSpec((tm, GL), lambda i: (i, glb)), wspec, wspec, wspec],
        out_specs=(pl.BlockSpec((tm, GL), lambda i: (i, 0)), pl.BlockSpec((tm, GL), lambda i: (i, 0)),
                   pl.BlockSpec((tm, 3 * BR_W), lambda i: (i, 0))),
        out_shape=(jax.ShapeDtypeStruct((T, GL), BF16), jax.ShapeDtypeStruct((T, GL), BF16),
                   jax.ShapeDtypeStruct((T, 3 * BR_W), BF16)),
        compiler_params=_cparams(("parallel",)),
    )(dmerged, cat, z, w_pa, w_pb, w_pc)


def _loss_head(x, target, g, name):
    T, D = x.shape
    tm = _pick(T, 512, SUBLANE)

    def body(x_ref, t_ref, g_ref, dx_ref, part_ref):
        i = pl.program_id(0)
        xv = x_ref[...]
        gv = g_ref[...]
        rstd = lax.rsqrt(jnp.mean(xv * xv, axis=-1, keepdims=True) + EPS)
        xhat = xv * rstd
        err = xhat * gv - t_ref[...]
        dy = err * (1.0 / D)
        dxhat = dy * gv
        dx_ref[...] = rstd * (dxhat - xhat * jnp.mean(dxhat * xhat, axis=-1, keepdims=True))

        @pl.when(i == 0)
        def _():
            part_ref[...] = jnp.zeros_like(part_ref)

        part_ref[0:1, :] += jnp.sum(dy * xhat, axis=0, keepdims=True)
        part_ref[1:2, :] += jnp.zeros((1, D), F32) + (0.5 / D) * jnp.sum(err * err)

    row = pl.BlockSpec((tm, D), lambda i: (i, 0))
    return pl.pallas_call(
        body, name=name, grid=(T // tm,),
        in_specs=[row, row, pl.BlockSpec((1, D), lambda i: (0, 0))],
        out_specs=(row, pl.BlockSpec((SUBLANE, D), lambda i: (0, 0))),
        out_shape=(jax.ShapeDtypeStruct((T, D), F32), jax.ShapeDtypeStruct((SUBLANE, D), F32)),
        compiler_params=_cparams(("arbitrary",)),
    )(x, target, g)


def _mod_fwd(c_all, w_mod, b_cols, name):
    L, D, N4 = w_mod.shape
    Bg = c_all.shape[0]
    tn = _pick(N4, 768)

    def body(c_ref, w_ref, b_ref, o_ref):
        cv = c_ref[...]
        ca = (cv * _sigmoid(cv)).astype(BF16)
        o_ref[...] = jnp.dot(ca, w_ref[...].astype(BF16), preferred_element_type=F32) + b_ref[...]

    return pl.pallas_call(
        body, name=name, grid=(L, N4 // tn),
        in_specs=[pl.BlockSpec((Bg, D), lambda l, j: (0, 0)),
                  pl.BlockSpec((None, D, tn), lambda l, j: (l, 0, j)),
                  pl.BlockSpec((None, 1, tn), lambda l, j: (l, 0, j))],
        out_specs=pl.BlockSpec((None, Bg, tn), lambda l, j: (l, 0, j)),
        out_shape=jax.ShapeDtypeStruct((L, Bg, N4), F32),
        compiler_params=_cparams(("parallel", "parallel")),
    )(c_all, w_mod, b_cols)


def _mod_wgrad(c_all, dmod_cols, name, deps=()):
    L, Bg, N4 = dmod_cols.shape
    D = c_all.shape[1]
    tn = _pick(N4, 768)

    def body(c_ref, d_ref, *rest):
        cv = c_ref[...]
        ca = (cv * _sigmoid(cv)).astype(BF16)
        rest[-1][...] = lax.dot_general(ca, d_ref[...].astype(BF16), (((0,), (0,)), ((), ())),
                                     preferred_element_type=F32)

    return pl.pallas_call(
        body, name=name, grid=(L, N4 // tn),
        in_specs=[pl.BlockSpec((Bg, D), lambda l, j: (0, 0)),
                  pl.BlockSpec((None, Bg, tn), lambda l, j: (l, 0, j))] + [ANY] * len(deps),
        out_specs=pl.BlockSpec((None, D, tn), lambda l, j: (l, 0, j)),
        out_shape=jax.ShapeDtypeStruct((L, D, N4), F32),
        compiler_params=_cparams(("parallel", "parallel")),
    )(c_all, dmod_cols, *deps)


def _rows_tile(R, C):
    return _pick(R, max(SUBLANE, (256 * 1024) // C), SUBLANE)


def _cast_into_full(w, layer, col_sharded, chip, name, deps=()):
    L, R, C = w.shape
    K, N = (R, C * N_CHIP) if col_sharded else (R * N_CHIP, C)
    tr = _pick(R, max(HALO, (512 * 1024) // C), HALO)
    nb = R // tr

    def body(q_ref, w_ref, *rest):
        rest[-1][...] = w_ref[...].astype(BF16)

    out_idx = (lambda i, q: (i, q[0])) if col_sharded else (lambda i, q: (q[0] * nb + i, 0))
    grid_spec = pltpu.PrefetchScalarGridSpec(
        num_scalar_prefetch=1, grid=(nb,),
        in_specs=[pl.BlockSpec((None, tr, C), lambda i, q: (layer, i, 0))] + [ANY] * len(deps),
        out_specs=pl.BlockSpec((tr, C), out_idx))
    return pl.pallas_call(
        body, name=name, grid_spec=grid_spec, out_shape=jax.ShapeDtypeStruct((K, N), BF16),
        compiler_params=_cparams(("arbitrary",)),
    )(chip.reshape(1).astype(jnp.int32), w, *deps)


def _sum_into(stack, grad, slots, layer, col_sharded, chip, name):
    _, R, C = slots.shape
    tr = _rows_tile(R, C)
    nb = R // tr

    def body(q_ref, stack_ref, g_ref, s_ref, o_ref):
        o_ref[...] = ((g_ref[...].astype(F32) + s_ref[0].astype(F32)) + s_ref[1].astype(F32)) + s_ref[2].astype(F32)

    g_idx = (lambda i, q: (i, q[0])) if col_sharded else (lambda i, q: (q[0] * nb + i, 0))
    grid_spec = pltpu.PrefetchScalarGridSpec(
        num_scalar_prefetch=1, grid=(nb,),
        in_specs=[ANY, pl.BlockSpec((tr, C), g_idx), pl.BlockSpec((3, tr, C), lambda i, q: (0, i, 0))],
        out_specs=pl.BlockSpec((tr, C), lambda i, q: (layer * nb + i, 0)))
    return pl.pallas_call(
        body, name=name, grid_spec=grid_spec, out_shape=jax.ShapeDtypeStruct(stack.shape, F32),
        input_output_aliases={1: 0}, compiler_params=_cparams(("arbitrary",)),
    )(chip.reshape(1).astype(jnp.int32), stack, grad, slots)


def _sum_devices(parts, name):
    n, R, C = parts.shape
    tr = _rows_tile(R, C)

    def body(s_ref, o_ref):
        acc = s_ref[0]
        for d in range(1, n):
            acc = acc + s_ref[d]
        o_ref[...] = acc

    return pl.pallas_call(
        body, name=name, grid=(R // tr,),
        in_specs=[pl.BlockSpec((n, tr, C), lambda i: (0, i, 0))],
        out_specs=pl.BlockSpec((tr, C), lambda i: (i, 0)),
        out_shape=jax.ShapeDtypeStruct((R, C), F32), compiler_params=_cparams(("parallel",)),
    )(parts)


def _adamw(w, m, v, grads, name):
    R, C = w.shape
    tr = _rows_tile(R, C)
    bc1 = 1.0 - ADAM_B1 ** ADAM_STEP
    bc2 = 1.0 - ADAM_B2 ** ADAM_STEP
    ng = len(grads)

    def body(*refs):
        w_ref, m_ref, v_ref = refs[:3]
        g_refs = refs[3:3 + ng]
        g_out, d_out, m_out, v_out = refs[3 + ng:]
        g = g_refs[0][...]
        for r in g_refs[1:]:
            g = g + r[...]
        mn = ADAM_B1 * m_ref[...] + (1.0 - ADAM_B1) * g
        vn = ADAM_B2 * v_ref[...] + (1.0 - ADAM_B2) * (g * g)
        m_hat = mn / bc1
        v_hat = vn / bc2
        g_out[...] = g
        d_out[...] = -ADAM_LR * (m_hat / (jnp.sqrt(v_hat) + ADAM_EPS) + ADAM_WD * w_ref[...])
        m_out[...] = mn
        v_out[...] = vn

    spec = pl.BlockSpec((tr, C), lambda i: (i, 0))
    sds = jax.ShapeDtypeStruct((R, C), F32)
    return pl.pallas_call(
        body, name=name, grid=(R // tr,), in_specs=[spec] * (3 + ng), out_specs=(spec,) * 4,
        out_shape=(sds,) * 4, compiler_params=_cparams(("parallel",)),
    )(w, m, v, *grads)


def _place():
    x, y, c = lax.axis_index("x"), lax.axis_index("y"), lax.axis_index("c")
    chips = [(1 - x, y), (x, 1 - y), (1 - x, 1 - y)]
    return x, y, c, chips


ANY = pl.BlockSpec(memory_space=pl.ANY)


def _allgather8(v, name):
    m_per, n = v.shape

    def body(x_ref, out_ref, send_sems, recv_sems, local_sem):
        x, y, c, chips = _place()
        me, sibling = (x, y, c), (x, y, 1 - c)

        def rows(px, py, pc):
            return out_ref.at[pl.ds((4 * px + 2 * py + pc) * m_per, m_per), :]

        def copy(k, block, to, src=None):
            return pltpu.make_async_remote_copy(
                src_ref=rows(*block) if src is None else src, dst_ref=rows(*block),
                send_sem=send_sems.at[k], recv_sem=recv_sems.at[k], device_id=to, device_id_type=MESH)

        mine = pltpu.make_async_copy(x_ref, rows(*me), local_sem)
        mine.start()
        first = [copy(0, me, sibling, src=x_ref)]
        first += [copy(1 + j, me, (*chip, c), src=x_ref) for j, chip in enumerate(chips)]
        for cp in first:
            cp.start()
        passed = [copy(4 + j, (*chip, c), sibling) for j, chip in enumerate(chips)]
        for j, chip in enumerate(chips):
            copy(1 + j, (*chip, c), me).wait_recv()
            passed[j].start()
        copy(0, sibling, me).wait_recv()
        for j, chip in enumerate(chips):
            copy(4 + j, (*chip, 1 - c), me).wait_recv()
        for cp in first + passed:
            cp.wait_send()
        mine.wait()

    return pl.pallas_call(
        body, name=name, out_shape=jax.ShapeDtypeStruct((N_DEV * m_per, n), v.dtype),
        in_specs=[ANY], out_specs=ANY,
        scratch_shapes=[pltpu.SemaphoreType.DMA((7,)), pltpu.SemaphoreType.DMA((7,)), pltpu.SemaphoreType.DMA],
    )(v)


def _window(ref, col_sharded, q, lead):
    full = (slice(None),) * lead
    if col_sharded:
        width = ref.shape[-1] // N_CHIP
        return ref.at[full + (slice(None), pl.ds(pl.multiple_of(q * width, LANE), width))]
    height = ref.shape[-2] // N_CHIP
    return ref.at[full + (pl.ds(pl.multiple_of(q * height, HALO), height), slice(None))]


HBM = pl.BlockSpec(memory_space=pltpu.HBM)
SEM = pl.BlockSpec(memory_space=pltpu.SEMAPHORE)
EFFECT = pltpu.SideEffectType.DATAFLOW_SIDE_EFFECTING


def _in_hbm(v):
    return pltpu.with_memory_space_constraint(v, pltpu.HBM)


def _gather_start(bufs, col_sharded, name):
    n = len(bufs)

    def body(*refs):
        ins = refs[:n]
        send_sems, recv_sems = refs[n], refs[n + 1]
        token = refs[-1]
        x, y, c, chips = _place()
        q = 2 * x + y
        for w in range(n):
            for k, chip in enumerate(chips):
                pltpu.make_async_remote_copy(
                    src_ref=_window(ins[w], col_sharded[w], q, 0), dst_ref=_window(ins[w], col_sharded[w], q, 0),
                    send_sem=send_sems.at[3 * w + k], recv_sem=recv_sems.at[3 * w + k],
                    device_id=(*chip, c), device_id_type=MESH).start()
        token[...] = jnp.zeros_like(token)

    out = pl.pallas_call(
        body, name=name,
        out_shape=(pltpu.SemaphoreType.DMA((3 * n,)), pltpu.SemaphoreType.DMA((3 * n,)),
                   *[pltpu.HBM(b.shape, b.dtype) for b in bufs], jax.ShapeDtypeStruct((SUBLANE, LANE), F32)),
        in_specs=(HBM,) * n, out_specs=(SEM, SEM) + (HBM,) * n + (pl.BlockSpec(memory_space=pltpu.VMEM),),
        input_output_aliases={w: 2 + w for w in range(n)},
        compiler_params=pltpu.CompilerParams(has_side_effects=EFFECT),
    )(*[_in_hbm(b) for b in bufs])
    return out[0], out[1], list(out[2:2 + n]), out[-1]


def _gather_wait(send_sems, recv_sems, bufs, col_sharded, after, name):
    n = len(bufs)

    def body(*refs):
        ins = refs[:n]
        send_sems, recv_sems = refs[n], refs[n + 1]
        x, y, c, chips = _place()
        q = 2 * x + y
        for w in range(n):
            for k, (cx, cy) in enumerate(chips):
                cp = pltpu.make_async_remote_copy(
                    src_ref=_window(ins[w], col_sharded[w], q, 0),
                    dst_ref=_window(ins[w], col_sharded[w], 2 * cx + cy, 0),
                    send_sem=send_sems.at[3 * w + k], recv_sem=recv_sems.at[3 * w + k],
                    device_id=(cx, cy, c), device_id_type=MESH)
                cp.wait_send()
                cp.wait_recv()

    out = pl.pallas_call(
        body, name=name, out_shape=tuple(pltpu.HBM(b.shape, b.dtype) for b in bufs),
        in_specs=(HBM,) * n + (SEM, SEM) + (ANY,) * len(after), out_specs=(HBM,) * n,
        input_output_aliases={w: w for w in range(n)},
        compiler_params=pltpu.CompilerParams(has_side_effects=EFFECT),
    )(*bufs, send_sems, recv_sems, *after)
    return list(out)


def _scatter_start(grads, col_sharded, name):
    n = len(grads)
    lands = []
    for g, cs in zip(grads, col_sharded):
        K, N = g.shape
        lands.append(lax.empty((3, K, N // N_CHIP) if cs else (3, K // N_CHIP, N), BF16))

    def body(*refs):
        ins, slots = refs[:n], refs[n:2 * n]
        send_sems, recv_sems = refs[2 * n], refs[2 * n + 1]
        token = refs[-1]
        x, y, c, chips = _place()
        for w in range(n):
            for k, (cx, cy) in enumerate(chips):
                pltpu.make_async_remote_copy(
                    src_ref=_window(ins[w], col_sharded[w], 2 * cx + cy, 0), dst_ref=slots[w].at[k],
                    send_sem=send_sems.at[3 * w + k], recv_sem=recv_sems.at[3 * w + k],
                    device_id=(cx, cy, c), device_id_type=MESH).start()
        token[...] = jnp.zeros_like(token)

    out = pl.pallas_call(
        body, name=name,
        out_shape=(pltpu.SemaphoreType.DMA((3 * n,)), pltpu.SemaphoreType.DMA((3 * n,)),
                   *[pltpu.HBM(b.shape, b.dtype) for b in grads], *[pltpu.HBM(b.shape, b.dtype) for b in lands],
                   jax.ShapeDtypeStruct((SUBLANE, LANE), F32)),
        in_specs=(HBM,) * (2 * n),
        out_specs=(SEM, SEM) + (HBM,) * (2 * n) + (pl.BlockSpec(memory_space=pltpu.VMEM),),
        input_output_aliases={w: 2 + w for w in range(2 * n)},
        compiler_params=pltpu.CompilerParams(has_side_effects=EFFECT),
    )(*[_in_hbm(b) for b in grads], *[_in_hbm(b) for b in lands])
    return out[0], out[1], list(out[2:2 + n]), list(out[2 + n:2 + 2 * n]), out[-1]


def _scatter_wait(send_sems, recv_sems, grads, lands, col_sharded, after, name):
    n = len(grads)

    def body(*refs):
        ins, slots = refs[:n], refs[n:2 * n]
        send_sems, recv_sems = refs[2 * n], refs[2 * n + 1]
        x, y, c, chips = _place()
        for w in range(n):
            for k, (cx, cy) in enumerate(chips):
                cp = pltpu.make_async_remote_copy(
                    src_ref=_window(ins[w], col_sharded[w], 2 * cx + cy, 0), dst_ref=slots[w].at[k],
                    send_sem=send_sems.at[3 * w + k], recv_sem=recv_sems.at[3 * w + k],
                    device_id=(cx, cy, c), device_id_type=MESH)
                cp.wait_send()
                cp.wait_recv()

    out = pl.pallas_call(
        body, name=name, out_shape=tuple(pltpu.HBM(b.shape, b.dtype) for b in list(grads) + list(lands)),
        in_specs=(HBM,) * (2 * n) + (SEM, SEM) + (ANY,) * len(after), out_specs=(HBM,) * (2 * n),
        input_output_aliases={w: w for w in range(2 * n)},
        compiler_params=pltpu.CompilerParams(has_side_effects=EFFECT),
    )(*grads, *lands, send_sems, recv_sems, *after)
    return list(out[:n]), list(out[n:])


def _swap_start(arrays, name):
    n = len(arrays)
    lands = [lax.empty(a.shape, a.dtype) for a in arrays]

    def body(*refs):
        ins, lnd = refs[:n], refs[n:2 * n]
        send_sems, recv_sems = refs[2 * n], refs[2 * n + 1]
        token = refs[-1]
        x, y, c, _ = _place()
        for w in range(n):
            pltpu.make_async_remote_copy(
                src_ref=ins[w], dst_ref=lnd[w], send_sem=send_sems.at[w], recv_sem=recv_sems.at[w],
                device_id=(x, y, 1 - c), device_id_type=MESH).start()
        token[...] = jnp.zeros_like(token)

    out = pl.pallas_call(
        body, name=name,
        out_shape=(pltpu.SemaphoreType.DMA((n,)), pltpu.SemaphoreType.DMA((n,)),
                   *[pltpu.HBM(b.shape, b.dtype) for b in arrays], *[pltpu.HBM(b.shape, b.dtype) for b in lands],
                   jax.ShapeDtypeStruct((SUBLANE, LANE), F32)),
        in_specs=(HBM,) * (2 * n),
        out_specs=(SEM, SEM) + (HBM,) * (2 * n) + (pl.BlockSpec(memory_space=pltpu.VMEM),),
        input_output_aliases={w: 2 + w for w in range(2 * n)},
        compiler_params=pltpu.CompilerParams(has_side_effects=EFFECT),
    )(*[_in_hbm(b) for b in arrays], *[_in_hbm(b) for b in lands])
    return out[0], out[1], list(out[2:2 + n]), list(out[2 + n:2 + 2 * n]), out[-1]


def _swap_wait(send_sems, recv_sems, arrays, lands, after, name):
    n = len(arrays)

    def body(*refs):
        ins, lnd = refs[:n], refs[n:2 * n]
        send_sems, recv_sems = refs[2 * n], refs[2 * n + 1]
        x, y, c, _ = _place()
        for w in range(n):
            cp = pltpu.make_async_remote_copy(
                src_ref=ins[w], dst_ref=lnd[w], send_sem=send_sems.at[w], recv_sem=recv_sems.at[w],
                device_id=(x, y, 1 - c), device_id_type=MESH)
            cp.wait_send()
            cp.wait_recv()

    out = pl.pallas_call(
        body, name=name, out_shape=tuple(pltpu.HBM(b.shape, b.dtype) for b in list(arrays) + list(lands)),
        in_specs=(HBM,) * (2 * n) + (SEM, SEM) + (ANY,) * len(after), out_specs=(HBM,) * (2 * n),
        input_output_aliases={w: w for w in range(2 * n)},
        compiler_params=pltpu.CompilerParams(has_side_effects=EFFECT),
    )(*arrays, *lands, send_sems, recv_sems, *after)
    return list(out[:n]), list(out[n:])


BIG = ("w_in", "w_pa", "w_pb", "w_pc", "w_o", "w_13", "w_2")
BIG_COL_SHARDED = (True, True, True, True, False, True, False)
SMALL = ("b_mod", "g_mix", "gm_ln_g", "gm_ln_b", "gm_w_s", "gm_b_s", "pool_w", "pool_scale", "conv_w",
         "g_ffn", "g_final")
WEIGHTS = ("w_mod", "b_mod", "g_mix", "w_in", "gm_ln_g", "gm_ln_b", "gm_w_s", "gm_b_s", "w_pa", "pool_w",
           "pool_scale", "w_pb", "conv_w", "w_pc", "w_o", "g_ffn", "w_13", "w_2", "g_final")


def _pack(arrays, width):
    flat = jnp.concatenate([a.reshape(-1) for a in arrays])
    rows = -(-flat.shape[0] // width)
    rows = -(-rows // SUBLANE) * SUBLANE
    flat = jnp.pad(flat, (0, rows * width - flat.shape[0]))
    return flat.reshape(rows, width)


def _unpack(packed, shapes):
    flat = packed.reshape(-1)
    out, off = [], 0
    for s in shapes:
        size = 1
        for d in s:
            size *= d
        out.append(flat[off:off + size].reshape(s))
        off += size
    return out


def kernel(x, c, w_mod, b_mod, g_mix, w_in, gm_ln_g, gm_ln_b, gm_w_s, gm_b_s, w_pa, pool_w, pool_scale, w_pb, conv_w, w_pc, w_o, g_ffn, w_13, w_2, g_final, loss_target, m_w_mod, m_b_mod, m_g_mix, m_w_in, m_gm_ln_g, m_gm_ln_b, m_gm_w_s, m_gm_b_s, m_w_pa, m_pool_w, m_pool_scale, m_w_pb, m_conv_w, m_w_pc, m_w_o, m_g_ffn, m_w_13, m_w_2, m_g_final, v_w_mod, v_b_mod, v_g_mix, v_w_in, v_gm_ln_g, v_gm_ln_b, v_gm_w_s, v_gm_b_s, v_w_pa, v_pool_w, v_pool_scale, v_w_pb, v_conv_w, v_w_pc, v_w_o, v_g_ffn, v_w_13, v_w_2, v_g_final):
    W = dict(w_mod=w_mod, b_mod=b_mod, g_mix=g_mix, w_in=w_in, gm_ln_g=gm_ln_g, gm_ln_b=gm_ln_b, gm_w_s=gm_w_s,
             gm_b_s=gm_b_s, w_pa=w_pa, pool_w=pool_w, pool_scale=pool_scale, w_pb=w_pb, conv_w=conv_w, w_pc=w_pc,
             w_o=w_o, g_ffn=g_ffn, w_13=w_13, w_2=w_2, g_final=g_final)
    Mo = dict(w_mod=m_w_mod, b_mod=m_b_mod, g_mix=m_g_mix, w_in=m_w_in, gm_ln_g=m_gm_ln_g, gm_ln_b=m_gm_ln_b,
              gm_w_s=m_gm_w_s, gm_b_s=m_gm_b_s, w_pa=m_w_pa, pool_w=m_pool_w, pool_scale=m_pool_scale, w_pb=m_w_pb,
              conv_w=m_conv_w, w_pc=m_w_pc, w_o=m_w_o, g_ffn=m_g_ffn, w_13=m_w_13, w_2=m_w_2, g_final=m_g_final)
    Vo = dict(w_mod=v_w_mod, b_mod=v_b_mod, g_mix=v_g_mix, w_in=v_w_in, gm_ln_g=v_gm_ln_g, gm_ln_b=v_gm_ln_b,
              gm_w_s=v_gm_w_s, gm_b_s=v_gm_b_s, w_pa=v_w_pa, pool_w=v_pool_w, pool_scale=v_pool_scale, w_pb=v_w_pb,
              conv_w=v_conv_w, w_pc=v_w_pc, w_o=v_w_o, g_ffn=v_g_ffn, w_13=v_w_13, w_2=v_w_2, g_final=v_g_final)

    B, S, D = x.shape
    T = B * S
    L = w_in.shape[0]
    Bg = B * N_DEV
    N4 = w_mod.shape[2]
    CW = conv_w.shape[2]
    xi, yi, ci = lax.axis_index("x"), lax.axis_index("y"), lax.axis_index("c")
    chip = 2 * xi + yi
    dev = 2 * chip + ci

    head = _pack([c, conv_w], D)
    hrows = head.shape[0]
    got = _allgather8(head, "gather_c_conv").reshape(N_DEV, hrows * D)
    c_all = got[:, :B * D].reshape(Bg, D)
    conv_parts = got[:, B * D:B * D + L * 3 * CW].reshape(N_CHIP, 2, L, 3, CW)[:, 0]
    conv_full = jnp.transpose(conv_parts, (1, 2, 0, 3)).reshape(L, 3, N_CHIP * CW)

    b_cols = lax.dynamic_slice_in_dim(b_mod, chip * N4, N4, axis=1).reshape(L, 1, N4)
    mod_part = _mod_fwd(c_all, w_mod, b_cols, "mod_fwd")
    half = Bg // 2
    mine = lax.dynamic_slice_in_dim(mod_part, ci * half, half, axis=1)
    mod_got = _allgather8(mine.reshape(L * half, N4), "gather_mod").reshape(N_CHIP, 2, L, half, N4)
    mod_full = jnp.transpose(mod_got, (2, 1, 3, 0, 4)).reshape(L, Bg, 6, D)
    mod_mine = lax.dynamic_slice_in_dim(mod_full, dev * B, B, axis=1)
    mod = jnp.pad(mod_mine, ((0, 0), (0, 0), (0, SUBLANE - 6), (0, 0)))

    PARTS = (("w_in",), ("w_pa", "w_pb", "w_pc", "w_o"), ("w_13", "w_2"))
    groups = [(names, l) for l in range(L) for names in PARTS]
    gathers = []
    order = (mod,)
    for gi, (names, l) in enumerate(groups):
        cs = [BIG_COL_SHARDED[BIG.index(n)] for n in names]
        bufs = [_cast_into_full(W[n], l, c_, chip, "cast_" + n, deps=order) for n, c_ in zip(names, cs)]
        ss, rs, bufs, tok = _gather_start(bufs, cs, f"gather_start_{gi}")
        gathers.append((ss, rs, bufs, names, cs))
        order = (tok,)

    def gathered(gi, after):
        ss, rs, bufs, names, cs = gathers[gi]
        return dict(zip(names, _gather_wait(ss, rs, bufs, cs, after, f"gather_wait_{gi}")))

    def mixer_params(l):
        vec = jnp.zeros((SUBLANE, BR_W), F32)
        vec = vec.at[0].set(gm_ln_g[l]).at[1].set(gm_ln_b[l]).at[2].set(pool_scale[l])
        conv = jnp.zeros((SUBLANE, BR_W), F32).at[0:3].set(conv_full[l])
        b_s = jnp.zeros((CHUNK, LANE), F32).at[:, 0:HEADS].set(jnp.transpose(gm_b_s[l]))
        return dict(vec=vec, conv=conv, w_s=gm_w_s[l], b_s=b_s, pool_w=pool_w[l])

    xs = x.reshape(T, D)
    saved = []
    for l in range(L):
        prm = mixer_params(l)
        full = gathered(3 * l, list(order) if l == 0 else [xs])
        z = _normed_matmul(xs, g_mix[l].reshape(1, D), mod[l], 0, 1, full["w_in"], S, "mm_in",
                           deps=order if l == 0 else ())
        cat = _mixer_fwd(z, prm, S, "mixer_fwd")
        full.update(gathered(3 * l + 1, [cat]))
        merged = _proj_fwd(cat, z, full["w_pa"], full["w_pb"], full["w_pc"], D, "proj_fwd")
        x1, mo = _matmul(merged, full["w_o"], mode="nn", name="mm_o", resid=(xs, mod[l], 2, S))
        full.update(gathered(3 * l + 2, [x1]))
        ga, gb, act = _normed_matmul(x1, g_ffn[l].reshape(1, D), mod[l], 3, 4, full["w_13"], S, "mm_13", swiglu=True)
        x2, ffo = _matmul(act, full["w_2"], mode="nn", name="mm_2", resid=(x1, mod[l], 5, S))
        saved.append(dict(prm=prm, full=full, x0=xs, z=z, cat=cat, merged=merged, mo=mo, x1=x1, ga=ga, gb=gb, act=act,
                          ffo=ffo))
        xs = x2

    dx, head_part = _loss_head(xs, loss_target.reshape(T, D), g_final.reshape(1, D), "loss_head")
    loss = lax.psum(head_part[1, 0], ("x", "y", "c"))

    big_grads = {n: [None] * L for n in BIG}
    small_part = {n: [None] * L for n in SMALL if n not in ("b_mod", "g_final")}
    dmod = [None] * L
    scatters = []
    FFN, MIX = ("w_13", "w_2"), ("w_in", "w_pa", "w_pb", "w_pc", "w_o")

    def scatter(names, l, tag):
        cs = [BIG_COL_SHARDED[BIG.index(n)] for n in names]
        ss, rs, gthru, lands, tok = _scatter_start([big_grads[n][l] for n in names], cs, f"scatter_start_{l}{tag}")
        scatters.append((ss, rs, gthru, lands, names, cs, l))
        return (tok,)

    def exchange_small():
        dmod_l = jnp.stack(dmod, axis=0).reshape(L * B, 6 * D)
        rows = -(-(L * B) // SUBLANE) * SUBLANE
        dmod_got = _allgather8(jnp.pad(dmod_l, ((0, rows - L * B), (0, 0))), "gather_dmod")
        local = dict(b_mod=jnp.sum(jnp.stack(dmod, axis=0), axis=1).reshape(L, 6 * D), g_final=head_part[0])
        for n in small_part:
            local[n] = jnp.stack(small_part[n], axis=0)
        packed = _pack([local[n] for n in SMALL], LANE)
        return dmod_got, rows, _allgather8(packed, "gather_small"), [local[n].shape for n in SMALL]

    sent = ()
    for l in reversed(range(L)):
        sv = saved[l]
        full = sv["full"]
        dffo, pg2, da, db = _gated_dgrad(dx, sv["ffo"], mod[l], 5, full["w_2"], S, "mm_2_dgrad",
                                         gab=(sv["ga"], sv["gb"]), deps=sent)
        dx1, h2, pb2, pgf = _dgrad_norm_bwd([da, db], full["w_13"], sv["x1"], dx, g_ffn[l].reshape(1, D), mod[l],
                                            3, 4, S, "mm_13_dgrad")
        big_grads["w_2"][l] = _matmul(sv["act"], dffo, mode="tn", name="mm_2_wgrad")
        big_grads["w_13"][l] = _wgrad_pair(h2, da, db, "mm_13_wgrad")
        sent = scatter(FFN, l, "a")

        dmo, pg1, dmerged = _gated_dgrad(dx1, sv["mo"], mod[l], 2, full["w_o"], S, "mm_o_dgrad", deps=sent)
        big_grads["w_o"][l] = _matmul(sv["merged"], dmo, mode="tn", name="mm_o_wgrad")
        dy, dgl, dcat = _proj_bwd(dmerged, sv["cat"], sv["z"], full["w_pa"], full["w_pb"], full["w_pc"], D,
                                  "proj_bwd")
        for k, n in enumerate(("w_pa", "w_pb", "w_pc")):
            big_grads[n][l] = _matmul(sv["cat"], dy, mode="tn", name="mm_proj_wgrad",
                                      a_cols=(k * BR_W, BR_W), b_cols=(k * D, D))
        dz, pv, dws, dbs, dpw = _mixer_bwd(sv["z"], dcat, dgl, sv["prm"], S, "mixer_bwd")
        dx0, h, pb1, pgm = _dgrad_norm_bwd([dz], full["w_in"], sv["x0"], dx1, g_mix[l].reshape(1, D), mod[l],
                                           0, 1, S, "mm_in_dgrad")
        dx = dx0

        dmod[l] = jnp.stack([pb1[:, 0], pb1[:, 1], pg1[:, 0], pb2[:, 0], pb2[:, 1], pg2[:, 0]], axis=1)
        small_part["g_mix"][l] = pgm[0]
        small_part["g_ffn"][l] = pgf[0]
        small_part["gm_ln_g"][l] = pv[0]
        small_part["gm_ln_b"][l] = pv[1]
        small_part["pool_scale"][l] = pv[2]
        small_part["conv_w"][l] = pv[3:6]
        small_part["gm_w_s"][l] = dws
        small_part["gm_b_s"][l] = jnp.transpose(dbs[:, 0:HEADS])
        small_part["pool_w"][l] = dpw

        if l == 0:
            dmod_got, dmod_rows, small_got, pshapes = exchange_small()
            last = (dmod_got, small_got)
        else:
            last = ()
        big_grads["w_in"][l] = _matmul(h, dz, mode="tn", name="mm_in_wgrad", deps=last)
        sent = scatter(MIX, l, "b")
    grad_x = dx.reshape(B, S, D)

    results = {}

    dmod_all = dmod_got.reshape(N_DEV, dmod_rows, 6 * D)[:, :L * B].reshape(N_DEV, L, B, 6 * D)
    dmod_all = jnp.transpose(dmod_all, (1, 0, 2, 3)).reshape(L, Bg, 6 * D)
    dmod_cols = lax.dynamic_slice_in_dim(dmod_all, chip * N4, N4, axis=2)
    g_wmod = _mod_wgrad(c_all, dmod_cols, "mod_wgrad", deps=sent)
    res = _adamw(w_mod.reshape(L * D, N4), m_w_mod.reshape(L * D, N4), v_w_mod.reshape(L * D, N4),
                 [g_wmod.reshape(L * D, N4)], "adamw_w_mod")
    results["w_mod"] = [r.reshape(L, D, N4) for r in res]

    gathered_small = small_got.reshape(N_DEV, small_got.shape[0] // N_DEV, LANE)
    g_small = dict(zip(SMALL, _unpack(_sum_devices(gathered_small, "sum_small"), pshapes)))
    g_small["conv_w"] = lax.dynamic_slice_in_dim(g_small["conv_w"], chip * CW, CW, axis=2)
    wshapes = [W[n].shape for n in SMALL]
    res = _adamw(_pack([W[n] for n in SMALL], LANE), _pack([Mo[n] for n in SMALL], LANE),
                 _pack([Vo[n] for n in SMALL], LANE), [_pack([g_small[n] for n in SMALL], LANE)], "adamw_small")
    small_res = [_unpack(r, wshapes) for r in res]
    for i, n in enumerate(SMALL):
        results[n] = [small_res[j][i] for j in range(4)]

    stacks = {n: lax.empty((W[n].shape[0] * W[n].shape[1], W[n].shape[2]), F32) for n in BIG}
    after = [res[0], results["w_mod"][0]]
    swaps = []
    for part in (FFN, MIX):
        for ss, rs, gthru, lands, names, cs, l in scatters:
            if names != part:
                continue
            gthru, lands = _scatter_wait(ss, rs, gthru, lands, cs, after, f"scatter_wait_{l}_{names[0]}")
            for n, g, ld, c_ in zip(names, gthru, lands, cs):
                stacks[n] = _sum_into(stacks[n], g, ld, l, c_, chip, "sum_" + n)
            after = [stacks[names[-1]]]
        ss, rs, mine, theirs, tok = _swap_start([stacks[n] for n in part], f"swap_start_{part[0]}")
        swaps.append((ss, rs, mine, theirs, part))
        after = after + [tok]
    for ss, rs, mine, theirs, part in swaps:
        mine, theirs = _swap_wait(ss, rs, mine, theirs, after, f"swap_wait_{part[0]}")
        for n, own, other in zip(part, mine, theirs):
            _, R, C = W[n].shape
            res = _adamw(W[n].reshape(L * R, C), Mo[n].reshape(L * R, C), Vo[n].reshape(L * R, C), [own, other],
                         "adamw_" + n)
            results[n] = [r.reshape(L, R, C) for r in res]
        after = [res[0]]

    return (loss, grad_x, *[results[n][0] for n in WEIGHTS], *[results[n][1] for n in WEIGHTS],
            *[results[n][2] for n in WEIGHTS], *[results[n][3] for n in WEIGHTS])
```

```python
import jax
import jax.numpy as jnp
from jax import lax
from jax.experimental import pallas as pl
from jax.experimental.pallas import tpu as pltpu

F32 = jnp.float32
BF16 = jnp.bfloat16
MESH = pl.DeviceIdType.MESH

EPS = 1e-6
CHUNK = 128
HEADS = 4
HEAD_DIM = 128
BR_W = 512
N_GROUP = 4
GROUP_DIM = 128
HALO = 16
N_SPLIT = 6 * BR_W
N_CHIP = 4
N_DEV = 8

ADAM_LR = 0.001
ADAM_B1 = 0.9
ADAM_B2 = 0.999
ADAM_EPS = 1e-08
ADAM_WD = 0.01
ADAM_STEP = 10

V7X_VMEM_LIMIT = 56 * 1024 * 1024
LANE = 128
SUBLANE = 8

GELU_K = 0.7978845608028654
GELU_C = 0.044715


def _cparams(sem):
    return pltpu.CompilerParams(dimension_semantics=sem, vmem_limit_bytes=V7X_VMEM_LIMIT)


def _pick(n, cap, q=LANE):
    best = None
    d = q
    while d <= min(n, cap):
        if n % d == 0:
            best = d
        d += q
    return n if best is None else best


def _sigmoid(x):
    return 0.5 * jnp.tanh(0.5 * x) + 0.5


def _gelu(x):
    t = jnp.tanh(GELU_K * (x + GELU_C * x * x * x))
    return 0.5 * x * (1.0 + t), t


def _gelu_grad(x, t):
    return 0.5 * (1.0 + t) + 0.5 * x * (1.0 - t * t) * GELU_K * (1.0 + 3.0 * GELU_C * x * x)


def _matmul(a, b, *, mode, name, out_dtype=None, layer=None, a_cols=None, b_cols=None,
            resid=None, deps=(), tm_cap=None, tn_cap=1536, tk_cap=1536):
    out_dtype = BF16 if out_dtype is None else out_dtype
    b2 = b.shape[-2:]
    if tm_cap is None:
        k_len = a.shape[0] if mode == "tn" else a.shape[1]
        if mode == "tn":
            m_len = a.shape[1] if a_cols is None else a_cols[1]
            tm_cap, tk_cap = 1536, 2048
            if m_len <= 1024:
                tn_cap, tk_cap = 1024, 4096
        else:
            tm_cap, tk_cap = (512, 8192) if k_len > 1536 else (1024, 1536)
    if mode == "nn":
        M, K = a.shape
        N = b2[1]
    elif mode == "nt":
        M, K = a.shape
        N = b2[0]
    else:
        K = a.shape[0]
        M = a.shape[1] if a_cols is None else a_cols[1]
        N = b2[1] if b_cols is None else b_cols[1]
    tm = _pick(M if resid is None else resid[3], tm_cap)
    tn = _pick(N, tn_cap)
    tk = _pick(K, tk_cap)
    nk = K // tk
    a_off = 0 if a_cols is None else a_cols[0] // tm
    b_off = 0 if b_cols is None else b_cols[0] // tn
    if a_cols is not None:
        assert a_cols[0] % tm == 0
    if b_cols is not None:
        assert b_cols[0] % tn == 0

    if mode == "nn":
        a_spec = pl.BlockSpec((tm, tk), lambda i, j, k: (i, k))
        b_blk, b_idx = (tk, tn), (lambda i, j, k: (k, j))
        dims = (((1,), (0,)), ((), ()))
    elif mode == "nt":
        a_spec = pl.BlockSpec((tm, tk), lambda i, j, k: (i, k))
        b_blk, b_idx = (tn, tk), (lambda i, j, k: (j, k))
        dims = (((1,), (1,)), ((), ()))
    else:
        a_spec = pl.BlockSpec((tk, tm), lambda i, j, k: (k, i + a_off))
        b_blk, b_idx = (tk, tn), (lambda i, j, k: (k, j + b_off))
        dims = (((0,), (0,)), ((), ()))
    if layer is None:
        b_spec = pl.BlockSpec(b_blk, b_idx)
    else:
        b_spec = pl.BlockSpec((None,) + b_blk, lambda i, j, k: (layer,) + b_idx(i, j, k))

    in_specs = [a_spec, b_spec]
    operands = [a, b]
    o_spec = pl.BlockSpec((tm, tn), lambda i, j, k: (i, j))
    if resid is not None:
        x, mod, row, seq = resid
        D = mod.shape[-1]
        in_specs += [o_spec, pl.BlockSpec((1, SUBLANE, tn), lambda i, j, k: ((i * tm) // seq, 0, j))]
        operands += [x, mod]
        out_shape = (jax.ShapeDtypeStruct((M, N), F32), jax.ShapeDtypeStruct((M, N), BF16))
        out_specs = (o_spec, o_spec)
        assert seq % tm == 0 and D == N
    else:
        out_shape = jax.ShapeDtypeStruct((M, N), out_dtype)
        out_specs = o_spec

    def finish(acc, refs):
        if resid is not None:
            x_ref, mod_ref, o_ref, p_ref = refs
            o_ref[...] = x_ref[...] + mod_ref[0, row:row + 1, :] * acc
            p_ref[...] = acc.astype(BF16)
        else:
            (o_ref,) = refs
            o_ref[...] = acc.astype(out_dtype)

    n_in = len(operands) - 2
    in_specs += [ANY] * len(deps)
    operands += list(deps)

    def body(a_ref, b_ref, *refs):
        refs = refs[:n_in] + refs[n_in + len(deps):]
        part = lax.dot_general(a_ref[...], b_ref[...], dims, preferred_element_type=F32)
        if nk == 1:
            finish(part, refs)
            return
        acc_ref = refs[-1]
        k = pl.program_id(2)

        @pl.when(k == 0)
        def _():
            acc_ref[...] = part

        @pl.when(k > 0)
        def _():
            acc_ref[...] += part

        @pl.when(k == nk - 1)
        def _():
            finish(acc_ref[...], refs[:-1])

    scratch = [] if nk == 1 else [pltpu.VMEM((tm, tn), F32)]
    return pl.pallas_call(
        body, name=name, grid=(M // tm, N // tn, nk), in_specs=in_specs, out_specs=out_specs,
        out_shape=out_shape, scratch_shapes=scratch,
        compiler_params=_cparams(("parallel", "parallel", "arbitrary")),
    )(*operands)


def _row_tile(seq, cap):
    return _pick(seq, cap, HALO)


def _mod_spec(tm, seq, D):
    return pl.BlockSpec((1, SUBLANE, D), lambda i: ((i * tm) // seq, 0, 0))


def _normed_matmul(x, g, mod, shift_row, scale_row, w, seq, name, swiglu=False, deps=()):
    T, D = x.shape
    N = w.shape[1] // 2 if swiglu else w.shape[1]
    tm = _pick(seq, 512 if swiglu else 1024)
    tn = _pick(N, 4096 if swiglu else 1536)
    nj = N // tn
    n_w = 2 if swiglu else 1

    def body(x_ref, g_ref, mod_ref, *rest):
        w_refs = rest[:n_w]
        outs = rest[n_w + len(deps):-1]
        h_ref = rest[-1]

        @pl.when(pl.program_id(1) == 0)
        def _():
            xv = x_ref[...]
            rstd = lax.rsqrt(jnp.mean(xv * xv, axis=-1, keepdims=True) + EPS)
            n = xv * rstd * g_ref[...]
            h = n * (1.0 + mod_ref[0, scale_row:scale_row + 1, :]) + mod_ref[0, shift_row:shift_row + 1, :]
            h_ref[...] = h.astype(BF16)

        h = h_ref[...]
        if not swiglu:
            outs[0][...] = jnp.dot(h, w_refs[0][...], preferred_element_type=F32).astype(BF16)
            return
        a16 = jnp.dot(h, w_refs[0][...], preferred_element_type=F32).astype(BF16)
        b16 = jnp.dot(h, w_refs[1][...], preferred_element_type=F32).astype(BF16)
        outs[0][...] = a16
        outs[1][...] = b16
        outs[2][...] = a16 * _sigmoid(a16) * b16

    once = pl.Buffered(1) if nj == 1 else None
    w_specs = [pl.BlockSpec((D, tn), lambda i, j: (0, j), pipeline_mode=once)]
    if swiglu:
        w_specs.append(pl.BlockSpec((D, tn), lambda i, j: (0, j + nj), pipeline_mode=once))
    o_spec = pl.BlockSpec((tm, tn), lambda i, j: (i, j))
    n_out = 3 if swiglu else 1
    out = pl.pallas_call(
        body, name=name, grid=(T // tm, nj),
        in_specs=[pl.BlockSpec((tm, D), lambda i, j: (i, 0)), pl.BlockSpec((1, D), lambda i, j: (0, 0)),
                  pl.BlockSpec((1, SUBLANE, D), lambda i, j: ((i * tm) // seq, 0, 0))] + w_specs + [ANY] * len(deps),
        out_specs=(o_spec,) * n_out, out_shape=(jax.ShapeDtypeStruct((T, N), BF16),) * n_out,
        scratch_shapes=[pltpu.VMEM((tm, D), BF16)],
        compiler_params=_cparams(("parallel", "arbitrary")),
    )(x, g, mod, *([w] * n_w), *deps)
    return out if swiglu else out[0]


def _dgrad_norm_bwd(parts, w, x, dres, g, mod, shift_row, scale_row, seq, name):
    T, D = x.shape
    B = mod.shape[0]
    n = len(parts)
    Kp = parts[0].shape[1]
    tm = _pick(seq, 512)
    per_seq = seq // tm

    def body(*refs):
        p_refs, w_refs = refs[:len(parts)], refs[len(parts):2 * len(parts)]
        x_ref, dres_ref, g_ref, mod_ref, dx_ref, h_ref, pb_ref, pg_ref = refs[2 * len(parts):]
        i = pl.program_id(0)
        dims = (((1,), (1,)), ((), ()))
        dhv = lax.dot_general(p_refs[0][...], w_refs[0][...], dims, preferred_element_type=F32)
        for p_ref, w_ref in zip(p_refs[1:], w_refs[1:]):
            dhv = dhv + lax.dot_general(p_ref[...], w_ref[...], dims, preferred_element_type=F32)
        xv = x_ref[...]
        gv = g_ref[...]
        scale1 = 1.0 + mod_ref[0, scale_row:scale_row + 1, :]
        rstd = lax.rsqrt(jnp.mean(xv * xv, axis=-1, keepdims=True) + EPS)
        xhat = xv * rstd
        n = xhat * gv
        dn = dhv * scale1
        dxhat = dn * gv
        dx = rstd * (dxhat - xhat * jnp.mean(dxhat * xhat, axis=-1, keepdims=True))
        dx_ref[...] = dres_ref[...] + dx
        h_ref[...] = (n * scale1 + mod_ref[0, shift_row:shift_row + 1, :]).astype(BF16)

        @pl.when(i % per_seq == 0)
        def _():
            pb_ref[...] = jnp.zeros_like(pb_ref)

        @pl.when(i == 0)
        def _():
            pg_ref[...] = jnp.zeros_like(pg_ref)

        pb_ref[0, 0:1, :] += jnp.sum(dhv, axis=0, keepdims=True)
        pb_ref[0, 1:2, :] += jnp.sum(dhv * n, axis=0, keepdims=True)
        pg_ref[0:1, :] += jnp.sum(dn * xhat, axis=0, keepdims=True)

    row = pl.BlockSpec((tm, D), lambda i: (i, 0))
    p_specs = [pl.BlockSpec((tm, Kp), lambda i: (i, 0))] * n
    w_specs = [pl.BlockSpec((D, Kp), lambda i, p=p: (0, p), pipeline_mode=pl.Buffered(1)) for p in range(n)]
    return pl.pallas_call(
        body, name=name, grid=(T // tm,),
        in_specs=p_specs + w_specs + [row, row, pl.BlockSpec((1, D), lambda i: (0, 0)), _mod_spec(tm, seq, D)],
        out_specs=(row, row, _mod_spec(tm, seq, D), pl.BlockSpec((SUBLANE, D), lambda i: (0, 0))),
        out_shape=(jax.ShapeDtypeStruct((T, D), F32), jax.ShapeDtypeStruct((T, D), BF16),
                   jax.ShapeDtypeStruct((B, SUBLANE, D), F32), jax.ShapeDtypeStruct((SUBLANE, D), F32)),
        compiler_params=_cparams(("arbitrary",)),
    )(*parts, *([w] * n), x, dres, g, mod)


def _gated_dgrad(dx, prod, mod, gate_row, w, seq, name, gab=None, deps=()):
    T, D = dx.shape
    B = mod.shape[0]
    N = w.shape[0]
    tm = _pick(seq, 512 if gab is None else 256)
    tn = _pick(N, 4096)
    per_seq = seq // tm
    n_gab = 0 if gab is None else 2

    def body(dx_ref, p_ref, mod_ref, w_ref, *rest):
        gab_refs = rest[:n_gab]
        dp_ref, pb_ref = rest[n_gab + len(deps):n_gab + len(deps) + 2]
        outs = rest[n_gab + len(deps) + 2:-1]
        a_ref = rest[-1]
        i = pl.program_id(0)

        @pl.when(pl.program_id(1) == 0)
        def _():
            dxv = dx_ref[...]
            dp = (dxv * mod_ref[0, gate_row:gate_row + 1, :]).astype(BF16)
            a_ref[...] = dp
            dp_ref[...] = dp

            @pl.when(i % per_seq == 0)
            def _():
                pb_ref[...] = jnp.zeros_like(pb_ref)

            pb_ref[0, 0:1, :] += jnp.sum(dxv * p_ref[...].astype(F32), axis=0, keepdims=True)

        du = lax.dot_general(a_ref[...], w_ref[...], (((1,), (1,)), ((), ())), preferred_element_type=F32)
        if gab is None:
            outs[0][...] = du.astype(BF16)
            return
        du = du.astype(BF16)
        a = gab_refs[0][...]
        b = gab_refs[1][...]
        sg = _sigmoid(a)
        dsg = du * sg
        outs[0][...] = dsg * b * (1.0 + a * (1.0 - sg))
        outs[1][...] = dsg * a

    row = pl.BlockSpec((tm, D), lambda i, j: (i, 0))
    tile = pl.BlockSpec((tm, tn), lambda i, j: (i, j))
    mod_spec = pl.BlockSpec((1, SUBLANE, D), lambda i, j: ((i * tm) // seq, 0, 0))
    n_out = 1 if gab is None else 2
    out = pl.pallas_call(
        body, name=name, grid=(T // tm, N // tn),
        in_specs=[row, row, mod_spec,
                  pl.BlockSpec((tn, D), lambda i, j: (j, 0), pipeline_mode=pl.Buffered(1) if N == tn else None)]
        + [tile] * n_gab
        + [ANY] * len(deps),
        out_specs=(row, mod_spec) + (tile,) * n_out,
        out_shape=(jax.ShapeDtypeStruct((T, D), BF16), jax.ShapeDtypeStruct((B, SUBLANE, D), F32))
        + (jax.ShapeDtypeStruct((T, N), BF16),) * n_out,
        scratch_shapes=[pltpu.VMEM((tm, D), BF16)],
        compiler_params=_cparams(("arbitrary", "arbitrary")),
    )(dx, prod, mod, w, *(gab or ()), *deps)
    return out


def _wgrad_pair(h, da, db, name):
    T, D = h.shape
    Fh = da.shape[1]
    tn = _pick(Fh, 1536)
    tk = _pick(T, 2048)
    half = Fh // tn
    nk = T // tk

    def body(h_ref, da_ref, db_ref, o_ref, acc_ref):
        j, k = pl.program_id(0), pl.program_id(1)
        dims = (((0,), (0,)), ((), ()))

        def accumulate(g_ref):
            part = lax.dot_general(h_ref[...], g_ref[...], dims, preferred_element_type=F32)

            @pl.when(k == 0)
            def _():
                acc_ref[...] = part

            @pl.when(k > 0)
            def _():
                acc_ref[...] += part

        @pl.when(j < half)
        def _():
            accumulate(da_ref)

        @pl.when(j >= half)
        def _():
            accumulate(db_ref)

        @pl.when(k == nk - 1)
        def _():
            o_ref[...] = acc_ref[...].astype(BF16)

    a_idx = lambda j, k: (jnp.where(j < half, k, 0), jnp.minimum(j, half - 1))
    b_idx = lambda j, k: (jnp.where(j >= half, k, 0), jnp.maximum(j - half, 0))
    return pl.pallas_call(
        body, name=name, grid=(2 * half, nk),
        in_specs=[pl.BlockSpec((tk, D), lambda j, k: (k, 0)), pl.BlockSpec((tk, tn), a_idx),
                  pl.BlockSpec((tk, tn), b_idx)],
        out_specs=pl.BlockSpec((D, tn), lambda j, k: (0, j)),
        out_shape=jax.ShapeDtypeStruct((D, 2 * Fh), BF16), scratch_shapes=[pltpu.VMEM((D, tn), F32)],
        compiler_params=_cparams(("parallel", "arbitrary")),
    )(h, da, db)


def _shift_down(v, d):
    return pltpu.roll(v, d, 0)


def _shift_up(v, d):
    return pltpu.roll(v, v.shape[0] - d, 0)


def _tril_mask():
    r = lax.broadcasted_iota(jnp.int32, (CHUNK, CHUNK), 0)
    c = lax.broadcasted_iota(jnp.int32, (CHUNK, CHUNK), 1)
    return c <= r


def _pool_counts(i, tm, seq, rows, first_row):
    r = lax.broadcasted_iota(jnp.int32, (rows, 1), 0) + (i * tm + first_row)
    pos1 = (r % seq + 1).astype(F32)
    return [jnp.minimum(pos1, float(2 << g)) for g in range(N_GROUP)]


def _mixer_forward_values(z_ref, hxb, hch, i, tm, seq, ln_g, ln_b, ws_ref, bs_ref, pw_ref, pscale, cw_ref):
    u = z_ref[:, 0 * BR_W:1 * BR_W]
    v = z_ref[:, 1 * BR_W:2 * BR_W]
    xb = z_ref[:, 2 * BR_W:3 * BR_W].astype(F32)
    bg = z_ref[:, 3 * BR_W:4 * BR_W].astype(F32)
    cg = z_ref[:, 4 * BR_W:5 * BR_W].astype(F32)
    hc = z_ref[:, 5 * BR_W:6 * BR_W].astype(F32)
    out = {}

    ug, tu = _gelu(u)
    vg, tv = _gelu(v)
    vg = vg.astype(F32)
    mu = jnp.mean(vg, axis=-1, keepdims=True)
    vc = vg - mu
    rstd = lax.rsqrt(jnp.mean(vc * vc, axis=-1, keepdims=True) + EPS)
    vhat = vc * rstd
    vn = (vhat * ln_g + ln_b).astype(BF16)
    mask = _tril_mask()
    wt = [jnp.where(mask, ws_ref[h], 0.0).astype(BF16) for h in range(HEADS)]
    rows = []
    for n in range(tm // CHUNK):
        blocks = []
        for h in range(HEADS):
            blk = vn[n * CHUNK:(n + 1) * CHUNK, h * HEAD_DIM:(h + 1) * HEAD_DIM]
            sb = jnp.dot(wt[h], blk, preferred_element_type=F32) + bs_ref[:, h:h + 1]
            blocks.append(sb)
        rows.append(jnp.concatenate(blocks, axis=1))
    s = jnp.concatenate(rows, axis=0) if len(rows) > 1 else rows[0]
    out.update(u=u, v=v, ug=ug, tu=tu, tv=tv, rstd=rstd, vhat=vhat, vn=vn, wt=wt, s=s, a_out=ug * s)

    ext = jnp.concatenate([hxb, xb], axis=0)
    cnt = _pool_counts(i, tm, seq, tm, 0)
    p, qs = [], []
    for g in range(N_GROUP):
        e = ext[:, g * GROUP_DIM:(g + 1) * GROUP_DIM]
        acc = e
        for d in (1, 2, 4, 8)[:g + 1]:
            acc = acc + _shift_down(acc, d)
        pg = acc[HALO:, :] / cnt[g] - xb[:, g * GROUP_DIM:(g + 1) * GROUP_DIM]
        p.append(pg.astype(BF16))
        qs.append(jnp.dot(p[g], pw_ref[g].astype(BF16), preferred_element_type=F32))
    q = jnp.concatenate(qs, axis=1)
    out.update(p=p, q=q, b_out=q * pscale)

    zc = cg * hc
    zce = jnp.concatenate([hch[:, :BR_W] * hch[:, BR_W:], zc], axis=0)
    z1 = _shift_down(zce, 1)[HALO:, :]
    z2 = _shift_down(zce, 2)[HALO:, :]
    y = cw_ref[0:1, :] * z2 + cw_ref[1:2, :] * z1 + cw_ref[2:3, :] * zc
    out.update(bg=bg, cg=cg, hc=hc, zc=zc, z1=z1, z2=z2, y=y, c_out=bg * y)
    return out


def _mixer_specs(tm, T):
    nb = T // HALO
    per = tm // HALO
    prev = lambda i: jnp.maximum(i * per - 1, 0)
    nxt = lambda i: jnp.minimum((i + 1) * per, nb - 1)
    return prev, nxt


def _mixer_fwd(z, prm, seq, name):
    T = z.shape[0]
    tm = _row_tile(seq, 256)
    per_seq = seq // tm
    prev, _ = _mixer_specs(tm, T)

    def body(z_ref, hxb_ref, hch_ref, vec_ref, cw_ref, ws_ref, bs_ref, pw_ref, cat_ref):
        i = pl.program_id(0)
        keep = jnp.where(i % per_seq == 0, 0.0, 1.0)
        hxb = hxb_ref[...].astype(F32) * keep
        hch = hch_ref[...].astype(F32) * keep
        o = _mixer_forward_values(z_ref, hxb, hch, i, tm, seq, vec_ref[0:1, :], vec_ref[1:2, :],
                                  ws_ref, bs_ref, pw_ref, vec_ref[2:3, :], cw_ref)
        cat_ref[:, 0 * BR_W:1 * BR_W] = o["a_out"].astype(BF16)
        cat_ref[:, 1 * BR_W:2 * BR_W] = o["b_out"].astype(BF16)
        cat_ref[:, 2 * BR_W:3 * BR_W] = o["c_out"].astype(BF16)

    full = lambda shape: pl.BlockSpec(shape, lambda i: (0,) * len(shape))
    return pl.pallas_call(
        body, name=name, grid=(T // tm,),
        in_specs=[pl.BlockSpec((tm, N_SPLIT), lambda i: (i, 0)),
                  pl.BlockSpec((HALO, BR_W), lambda i: (prev(i), 2)),
                  pl.BlockSpec((HALO, 2 * BR_W), lambda i: (prev(i), 2)),
                  full((SUBLANE, BR_W)), full((SUBLANE, BR_W)), full((HEADS, CHUNK, CHUNK)),
                  full((CHUNK, LANE)), full((N_GROUP, GROUP_DIM, GROUP_DIM))],
        out_specs=pl.BlockSpec((tm, 3 * BR_W), lambda i: (i, 0)),
        out_shape=jax.ShapeDtypeStruct((T, 3 * BR_W), BF16),
        compiler_params=_cparams(("parallel",)),
    )(z, z, z, prm["vec"], prm["conv"], prm["w_s"], prm["b_s"], prm["pool_w"])


def _mixer_bwd(z, dcat, dgl, prm, seq, name):
    T, IN = z.shape
    GL = IN - N_SPLIT
    tm = _row_tile(seq, 256)
    per_seq = seq // tm
    prev, nxt = _mixer_specs(tm, T)
    nrow = tm + HALO

    def body(z_ref, hxb_ref, hch_ref, nbg_ref, dcat_ref, ndb_ref, ndc_ref, dgl_ref,
             vec_ref, cw_ref, ws_ref, bs_ref, pw_ref,
             dz_ref, pv_ref, dws_ref, dbs_ref, dpw_ref):
        i = pl.program_id(0)
        keep_prev = jnp.where(i % per_seq == 0, 0.0, 1.0)
        keep_next = jnp.where(i % per_seq == per_seq - 1, 0.0, 1.0)
        hxb = hxb_ref[...].astype(F32) * keep_prev
        hch = hch_ref[...].astype(F32) * keep_prev
        ln_g = vec_ref[0:1, :]
        pscale = vec_ref[2:3, :]
        o = _mixer_forward_values(z_ref, hxb, hch, i, tm, seq, ln_g, vec_ref[1:2, :],
                                  ws_ref, bs_ref, pw_ref, pscale, cw_ref)
        dcv = dcat_ref[...].astype(F32)
        da = dcv[:, 0 * BR_W:1 * BR_W]
        db = dcv[:, 1 * BR_W:2 * BR_W]
        dc = dcv[:, 2 * BR_W:3 * BR_W]
        mask = _tril_mask()

        @pl.when(i == 0)
        def _():
            pv_ref[...] = jnp.zeros_like(pv_ref)
            dws_ref[...] = jnp.zeros_like(dws_ref)
            dbs_ref[...] = jnp.zeros_like(dbs_ref)
            dpw_ref[...] = jnp.zeros_like(dpw_ref)

        d_ug = da * o["s"]
        ds = da * o["ug"]
        ds_b = ds.astype(BF16)
        vn = o["vn"]
        dvn_rows = []
        dws = [jnp.zeros((CHUNK, CHUNK), F32) for _ in range(HEADS)]
        dsum = jnp.zeros((CHUNK, BR_W), F32)
        for n in range(tm // CHUNK):
            blocks = []
            rs = slice(n * CHUNK, (n + 1) * CHUNK)
            dsum = dsum + ds[rs, :]
            for h in range(HEADS):
                cs = slice(h * HEAD_DIM, (h + 1) * HEAD_DIM)
                dsb = ds_b[rs, cs]
                blocks.append(lax.dot_general(o["wt"][h], dsb, (((0,), (0,)), ((), ())),
                                              preferred_element_type=F32))
                dws[h] = dws[h] + lax.dot_general(dsb, vn[rs, cs], (((1,), (1,)), ((), ())),
                                                  preferred_element_type=F32)
            dvn_rows.append(jnp.concatenate(blocks, axis=1))
        dvn = jnp.concatenate(dvn_rows, axis=0) if len(dvn_rows) > 1 else dvn_rows[0]
        lane = lax.broadcasted_iota(jnp.int32, (CHUNK, LANE), 1)
        dbs_t = jnp.zeros((CHUNK, LANE), F32)
        for h in range(HEADS):
            dws_ref[h] += jnp.where(mask, dws[h], 0.0)
            rsum = jnp.sum(dsum[:, h * HEAD_DIM:(h + 1) * HEAD_DIM], axis=1, keepdims=True)
            dbs_t = dbs_t + jnp.where(lane == h, rsum, 0.0)
        dbs_ref[...] += dbs_t
        vhat = o["vhat"]
        pv_ref[0:1, :] += jnp.sum(dvn * vhat, axis=0, keepdims=True)
        pv_ref[1:2, :] += jnp.sum(dvn, axis=0, keepdims=True)
        dvhat = dvn * ln_g
        dvg = o["rstd"] * (dvhat - jnp.mean(dvhat, axis=-1, keepdims=True)
                           - vhat * jnp.mean(dvhat * vhat, axis=-1, keepdims=True))
        du = d_ug * _gelu_grad(o["u"], o["tu"])
        dv = dvg * _gelu_grad(o["v"], o["tv"])

        pv_ref[2:3, :] += jnp.sum(db * o["q"], axis=0, keepdims=True)
        dq = (db * pscale).astype(BF16)
        dqn = (ndb_ref[...].astype(F32) * pscale * keep_next).astype(BF16)
        cnt = _pool_counts(i, tm, seq, nrow, 0)
        dxb = []
        for g in range(N_GROUP):
            cs = slice(g * GROUP_DIM, (g + 1) * GROUP_DIM)
            pwg = pw_ref[g].astype(BF16)
            dpw_ref[g] += lax.dot_general(o["p"][g], dq[:, cs], (((0,), (0,)), ((), ())),
                                          preferred_element_type=F32)
            dp = lax.dot_general(dq[:, cs], pwg, (((1,), (1,)), ((), ())), preferred_element_type=F32)
            dpn = lax.dot_general(dqn[:, cs], pwg, (((1,), (1,)), ((), ())), preferred_element_type=F32)
            acc = jnp.concatenate([dp, dpn], axis=0) / cnt[g]
            for d in (1, 2, 4, 8)[:g + 1]:
                acc = acc + _shift_up(acc, d)
            dxb.append(acc[:tm, :] - dp)
        dxb = jnp.concatenate(dxb, axis=1)

        dbg = dc * o["y"]
        dy = dc * o["bg"]
        pv_ref[3:4, :] += jnp.sum(dy * o["z2"], axis=0, keepdims=True)
        pv_ref[4:5, :] += jnp.sum(dy * o["z1"], axis=0, keepdims=True)
        pv_ref[5:6, :] += jnp.sum(dy * o["zc"], axis=0, keepdims=True)
        dyn = ndc_ref[...].astype(F32) * nbg_ref[...].astype(F32) * keep_next
        dye = jnp.concatenate([dy, dyn], axis=0)
        dzc = (cw_ref[2:3, :] * dy + cw_ref[1:2, :] * _shift_up(dye, 1)[:tm, :]
               + cw_ref[0:1, :] * _shift_up(dye, 2)[:tm, :])
        dcg = dzc * o["hc"]
        dhc = dzc * o["cg"]

        dz_ref[:, 0 * BR_W:1 * BR_W] = du.astype(BF16)
        dz_ref[:, 1 * BR_W:2 * BR_W] = dv.astype(BF16)
        dz_ref[:, 2 * BR_W:3 * BR_W] = dxb.astype(BF16)
        dz_ref[:, 3 * BR_W:4 * BR_W] = dbg.astype(BF16)
        dz_ref[:, 4 * BR_W:5 * BR_W] = dcg.astype(BF16)
        dz_ref[:, 5 * BR_W:6 * BR_W] = dhc.astype(BF16)
        dz_ref[:, N_SPLIT:] = dgl_ref[...]

    full = lambda shape: pl.BlockSpec(shape, lambda i: (0,) * len(shape))
    return pl.pallas_call(
        body, name=name, grid=(T // tm,),
        in_specs=[pl.BlockSpec((tm, N_SPLIT), lambda i: (i, 0)),
                  pl.BlockSpec((HALO, BR_W), lambda i: (prev(i), 2)),
                  pl.BlockSpec((HALO, 2 * BR_W), lambda i: (prev(i), 2)),
                  pl.BlockSpec((HALO, BR_W), lambda i: (nxt(i), 3)),
                  pl.BlockSpec((tm, 3 * BR_W), lambda i: (i, 0)),
                  pl.BlockSpec((HALO, BR_W), lambda i: (nxt(i), 1)),
                  pl.BlockSpec((HALO, BR_W), lambda i: (nxt(i), 2)),
                  pl.BlockSpec((tm, GL), lambda i: (i, 0)),
                  full((SUBLANE, BR_W)), full((SUBLANE, BR_W)), full((HEADS, CHUNK, CHUNK)),
                  full((CHUNK, LANE)), full((N_GROUP, GROUP_DIM, GROUP_DIM))],
        out_specs=(pl.BlockSpec((tm, IN), lambda i: (i, 0)),
                   full((SUBLANE, BR_W)), full((HEADS, CHUNK, CHUNK)), full((CHUNK, LANE)),
                   full((N_GROUP, GROUP_DIM, GROUP_DIM))),
        out_shape=(jax.ShapeDtypeStruct((T, IN), BF16),
                   jax.ShapeDtypeStruct((SUBLANE, BR_W), F32),
                   jax.ShapeDtypeStruct((HEADS, CHUNK, CHUNK), F32),
                   jax.ShapeDtypeStruct((CHUNK, LANE), F32),
                   jax.ShapeDtypeStruct((N_GROUP, GROUP_DIM, GROUP_DIM), F32)),
        compiler_params=_cparams(("arbitrary",)),
    )(z, z, z, z, dcat, dcat, dcat, dgl, prm["vec"], prm["conv"], prm["w_s"], prm["b_s"], prm["pool_w"])


def _proj_fwd(cat, z, w_pa, w_pb, w_pc, D, name):
    T, IN = z.shape
    GL = 3 * D
    assert N_SPLIT % GL == 0
    glb = N_SPLIT // GL
    tm = _pick(T, 256, HALO)

    def body(cat_ref, gl_ref, wa_ref, wb_ref, wc_ref, m_ref):
        acc = jnp.zeros((tm, D), F32)
        for k, w_ref in enumerate((wa_ref, wb_ref, wc_ref)):
            y = jnp.dot(cat_ref[:, k * BR_W:(k + 1) * BR_W], w_ref[...], preferred_element_type=F32)
            acc = acc + _sigmoid(gl_ref[:, k * D:(k + 1) * D]).astype(F32) * y
        m_ref[...] = acc.astype(BF16)

    wspec = pl.BlockSpec((BR_W, D), lambda i: (0, 0))
    return pl.pallas_call(
        body, name=name, grid=(T // tm,),
        in_specs=[pl.BlockSpec((tm, 3 * BR_W), lambda i: (i, 0)),
                  pl.BlockSpec((tm, GL), lambda i: (i, glb)), wspec, wspec, wspec],
        out_specs=pl.BlockSpec((tm, D), lambda i: (i, 0)),
        out_shape=jax.ShapeDtypeStruct((T, D), BF16),
        compiler_params=_cparams(("parallel",)),
    )(cat, z, w_pa, w_pb, w_pc)


def _proj_bwd(dmerged, cat, z, w_pa, w_pb, w_pc, D, name):
    T, IN = z.shape
    GL = 3 * D
    glb = N_SPLIT // GL
    tm = _pick(T, 256, HALO)

    def body(dm_ref, cat_ref, gl_ref, wa_ref, wb_ref, wc_ref, dy_ref, dgl_ref, dcat_ref):
        dm = dm_ref[...]
        for k, w_ref in enumerate((wa_ref, wb_ref, wc_ref)):
            w = w_ref[...]
            y = jnp.dot(cat_ref[:, k * BR_W:(k + 1) * BR_W], w, preferred_element_type=F32).astype(BF16)
            sg = _sigmoid(gl_ref[:, k * D:(k + 1) * D])
            dyk = dm * sg
            dy_ref[:, k * D:(k + 1) * D] = dyk
            dgl_ref[:, k * D:(k + 1) * D] = dyk * y * (1.0 - sg)
            dcat_ref[:, k * BR_W:(k + 1) * BR_W] = lax.dot_general(
                dyk, w, (((1,), (1,)), ((), ())), preferred_element_type=F32).astype(BF16)

    wspec = pl.BlockSpec((BR_W, D), lambda i: (0, 0))
    return pl.pallas_call(
        body, name=name, grid=(T // tm,),
        in_specs=[pl.BlockSpec((tm, D), lambda i: (i, 0)),
                  pl.BlockSpec((tm, 3 * BR_W), lambda i: (i, 0)),
                  pl.BlockSpec((tm, GL), lambda i: (i, glb)), wspec, wspec, wspec],
        out_specs=(pl.BlockSpec((tm, GL), lambda i: (i, 0)), pl.BlockSpec((tm, GL), lambda i: (i, 0)),
                   pl.BlockSpec((tm, 3 * BR_W), lambda i: (i, 0))),
        out_shape=(jax.ShapeDtypeStruct((T, GL), BF16), jax.ShapeDtypeStruct((T, GL), BF16),
                   jax.ShapeDtypeStruct((T, 3 * BR_W), BF16)),
        compiler_params=_cparams(("parallel",)),
    )(dmerged, cat, z, w_pa, w_pb, w_pc)


def _loss_head(x, target, g, name):
    T, D = x.shape
    tm = _pick(T, 512, SUBLANE)

    def body(x_ref, t_ref, g_ref, dx_ref, part_ref):
        i = pl.program_id(0)
        xv = x_ref[...]
        gv = g_ref[...]
        rstd = lax.rsqrt(jnp.mean(xv * xv, axis=-1, keepdims=True) + EPS)
        xhat = xv * rstd
        err = xhat * gv - t_ref[...]
        dy = err * (1.0 / D)
        dxhat = dy * gv
        dx_ref[...] = rstd * (dxhat - xhat * jnp.mean(dxhat * xhat, axis=-1, keepdims=True))

        @pl.when(i == 0)
        def _():
            part_ref[...] = jnp.zeros_like(part_ref)

        part_ref[0:1, :] += jnp.sum(dy * xhat, axis=0, keepdims=True)
        part_ref[1:2, :] += jnp.zeros((1, D), F32) + (0.5 / D) * jnp.sum(err * err)

    row = pl.BlockSpec((tm, D), lambda i: (i, 0))
    return pl.pallas_call(
        body, name=name, grid=(T // tm,),
        in_specs=[row, row, pl.BlockSpec((1, D), lambda i: (0, 0))],
        out_specs=(row, pl.BlockSpec((SUBLANE, D), lambda i: (0, 0))),
        out_shape=(jax.ShapeDtypeStruct((T, D), F32), jax.ShapeDtypeStruct((SUBLANE, D), F32)),
        compiler_params=_cparams(("arbitrary",)),
    )(x, target, g)


def _mod_fwd(c_all, w_mod, b_cols, name):
    L, D, N4 = w_mod.shape
    Bg = c_all.shape[0]
    tn = _pick(N4, 768)

    def body(c_ref, w_ref, b_ref, o_ref):
        cv = c_ref[...]
        ca = (cv * _sigmoid(cv)).astype(BF16)
        o_ref[...] = jnp.dot(ca, w_ref[...].astype(BF16), preferred_element_type=F32) + b_ref[...]

    return pl.pallas_call(
        body, name=name, grid=(L, N4 // tn),
        in_specs=[pl.BlockSpec((Bg, D), lambda l, j: (0, 0)),
                  pl.BlockSpec((None, D, tn), lambda l, j: (l, 0, j)),
                  pl.BlockSpec((None, 1, tn), lambda l, j: (l, 0, j))],
        out_specs=pl.BlockSpec((None, Bg, tn), lambda l, j: (l, 0, j)),
        out_shape=jax.ShapeDtypeStruct((L, Bg, N4), F32),
        compiler_params=_cparams(("parallel", "parallel")),
    )(c_all, w_mod, b_cols)


def _mod_wgrad(c_all, dmod_cols, name, deps=()):
    L, Bg, N4 = dmod_cols.shape
    D = c_all.shape[1]
    tn = _pick(N4, 768)

    def body(c_ref, d_ref, *rest):
        cv = c_ref[...]
        ca = (cv * _sigmoid(cv)).astype(BF16)
        rest[-1][...] = lax.dot_general(ca, d_ref[...].astype(BF16), (((0,), (0,)), ((), ())),
                                     preferred_element_type=F32)

    return pl.pallas_call(
        body, name=name, grid=(L, N4 // tn),
        in_specs=[pl.BlockSpec((Bg, D), lambda l, j: (0, 0)),
                  pl.BlockSpec((None, Bg, tn), lambda l, j: (l, 0, j))] + [ANY] * len(deps),
        out_specs=pl.BlockSpec((None, D, tn), lambda l, j: (l, 0, j)),
        out_shape=jax.ShapeDtypeStruct((L, D, N4), F32),
        compiler_params=_cparams(("parallel", "parallel")),
    )(c_all, dmod_cols, *deps)


def _rows_tile(R, C):
    return _pick(R, max(SUBLANE, (512 * 1024) // C), SUBLANE)


def _cast_into_full(w, layer, col_sharded, chip, name, deps=()):
    L, R, C = w.shape
    K, N = (R, C * N_CHIP) if col_sharded else (R * N_CHIP, C)
    tr = _pick(R, max(HALO, (512 * 1024) // C), HALO)
    nb = R // tr

    def body(q_ref, w_ref, *rest):
        rest[-1][...] = w_ref[...].astype(BF16)

    out_idx = (lambda i, q: (i, q[0])) if col_sharded else (lambda i, q: (q[0] * nb + i, 0))
    grid_spec = pltpu.PrefetchScalarGridSpec(
        num_scalar_prefetch=1, grid=(nb,),
        in_specs=[pl.BlockSpec((None, tr, C), lambda i, q: (layer, i, 0))] + [ANY] * len(deps),
        out_specs=pl.BlockSpec((tr, C), out_idx))
    return pl.pallas_call(
        body, name=name, grid_spec=grid_spec, out_shape=jax.ShapeDtypeStruct((K, N), BF16),
        compiler_params=_cparams(("arbitrary",)),
    )(chip.reshape(1).astype(jnp.int32), w, *deps)


def _sum_into(stack, grad, slots, layer, col_sharded, chip, name):
    _, R, C = slots.shape
    tr = _rows_tile(R, C)
    nb = R // tr

    def body(q_ref, stack_ref, g_ref, s_ref, o_ref):
        o_ref[...] = ((g_ref[...].astype(F32) + s_ref[0].astype(F32)) + s_ref[1].astype(F32)) + s_ref[2].astype(F32)

    g_idx = (lambda i, q: (i, q[0])) if col_sharded else (lambda i, q: (q[0] * nb + i, 0))
    grid_spec = pltpu.PrefetchScalarGridSpec(
        num_scalar_prefetch=1, grid=(nb,),
        in_specs=[ANY, pl.BlockSpec((tr, C), g_idx), pl.BlockSpec((3, tr, C), lambda i, q: (0, i, 0))],
        out_specs=pl.BlockSpec((tr, C), lambda i, q: (layer * nb + i, 0)))
    return pl.pallas_call(
        body, name=name, grid_spec=grid_spec, out_shape=jax.ShapeDtypeStruct(stack.shape, F32),
        input_output_aliases={1: 0}, compiler_params=_cparams(("arbitrary",)),
    )(chip.reshape(1).astype(jnp.int32), stack, grad, slots)


def _sum_devices(parts, name):
    n, R, C = parts.shape
    tr = _rows_tile(R, C)

    def body(s_ref, o_ref):
        acc = s_ref[0]
        for d in range(1, n):
            acc = acc + s_ref[d]
        o_ref[...] = acc

    return pl.pallas_call(
        body, name=name, grid=(R // tr,),
        in_specs=[pl.BlockSpec((n, tr, C), lambda i: (0, i, 0))],
        out_specs=pl.BlockSpec((tr, C), lambda i: (i, 0)),
        out_shape=jax.ShapeDtypeStruct((R, C), F32), compiler_params=_cparams(("parallel",)),
    )(parts)


def _adamw(w, m, v, grads, name):
    R, C = w.shape
    tr = _rows_tile(R, C)
    bc1 = 1.0 - ADAM_B1 ** ADAM_STEP
    bc2 = 1.0 - ADAM_B2 ** ADAM_STEP
    ng = len(grads)

    def body(*refs):
        w_ref, m_ref, v_ref = refs[:3]
        g_refs = refs[3:3 + ng]
        g_out, d_out, m_out, v_out = refs[3 + ng:]
        g = g_refs[0][...]
        for r in g_refs[1:]:
            g = g + r[...]
        mn = ADAM_B1 * m_ref[...] + (1.0 - ADAM_B1) * g
        vn = ADAM_B2 * v_ref[...] + (1.0 - ADAM_B2) * (g * g)
        m_hat = mn / bc1
        v_hat = vn / bc2
        g_out[...] = g
        d_out[...] = -ADAM_LR * (m_hat / (jnp.sqrt(v_hat) + ADAM_EPS) + ADAM_WD * w_ref[...])
        m_out[...] = mn
        v_out[...] = vn

    spec = pl.BlockSpec((tr, C), lambda i: (i, 0))
    sds = jax.ShapeDtypeStruct((R, C), F32)
    return pl.pallas_call(
        body, name=name, grid=(R // tr,), in_specs=[spec] * (3 + ng), out_specs=(spec,) * 4,
        out_shape=(sds,) * 4, compiler_params=_cparams(("parallel",)),
    )(w, m, v, *grads)


def _place():
    x, y, c = lax.axis_index("x"), lax.axis_index("y"), lax.axis_index("c")
    chips = [(1 - x, y), (x, 1 - y), (1 - x, 1 - y)]
    return x, y, c, chips


ANY = pl.BlockSpec(memory_space=pl.ANY)


def _allgather8(v, name):
    m_per, n = v.shape

    def body(x_ref, out_ref, send_sems, recv_sems, local_sem):
        x, y, c, chips = _place()
        me, sibling = (x, y, c), (x, y, 1 - c)

        def rows(px, py, pc):
            return out_ref.at[pl.ds((4 * px + 2 * py + pc) * m_per, m_per), :]

        def copy(k, block, to, src=None):
            return pltpu.make_async_remote_copy(
                src_ref=rows(*block) if src is None else src, dst_ref=rows(*block),
                send_sem=send_sems.at[k], recv_sem=recv_sems.at[k], device_id=to, device_id_type=MESH)

        mine = pltpu.make_async_copy(x_ref, rows(*me), local_sem)
        mine.start()
        first = [copy(0, me, sibling, src=x_ref)]
        first += [copy(1 + j, me, (*chip, c), src=x_ref) for j, chip in enumerate(chips)]
        for cp in first:
            cp.start()
        passed = [copy(4 + j, (*chip, c), sibling) for j, chip in enumerate(chips)]
        for j, chip in enumerate(chips):
            copy(1 + j, (*chip, c), me).wait_recv()
            passed[j].start()
        copy(0, sibling, me).wait_recv()
        for j, chip in enumerate(chips):
            copy(4 + j, (*chip, 1 - c), me).wait_recv()
        for cp in first + passed:
            cp.wait_send()
        mine.wait()

    return pl.pallas_call(
        body, name=name, out_shape=jax.ShapeDtypeStruct((N_DEV * m_per, n), v.dtype),
        in_specs=[ANY], out_specs=ANY,
        scratch_shapes=[pltpu.SemaphoreType.DMA((7,)), pltpu.SemaphoreType.DMA((7,)), pltpu.SemaphoreType.DMA],
    )(v)


def _window(ref, col_sharded, q, lead):
    full = (slice(None),) * lead
    if col_sharded:
        width = ref.shape[-1] // N_CHIP
        return ref.at[full + (slice(None), pl.ds(pl.multiple_of(q * width, LANE), width))]
    height = ref.shape[-2] // N_CHIP
    return ref.at[full + (pl.ds(pl.multiple_of(q * height, HALO), height), slice(None))]


HBM = pl.BlockSpec(memory_space=pltpu.HBM)
SEM = pl.BlockSpec(memory_space=pltpu.SEMAPHORE)
EFFECT = pltpu.SideEffectType.DATAFLOW_SIDE_EFFECTING


def _in_hbm(v):
    return pltpu.with_memory_space_constraint(v, pltpu.HBM)


def _gather_start(bufs, col_sharded, name):
    n = len(bufs)

    def body(*refs):
        ins = refs[:n]
        send_sems, recv_sems = refs[n], refs[n + 1]
        token = refs[-1]
        x, y, c, chips = _place()
        q = 2 * x + y
        for w in range(n):
            for k, chip in enumerate(chips):
                pltpu.make_async_remote_copy(
                    src_ref=_window(ins[w], col_sharded[w], q, 0), dst_ref=_window(ins[w], col_sharded[w], q, 0),
                    send_sem=send_sems.at[3 * w + k], recv_sem=recv_sems.at[3 * w + k],
                    device_id=(*chip, c), device_id_type=MESH).start()
        token[...] = jnp.zeros_like(token)

    out = pl.pallas_call(
        body, name=name,
        out_shape=(pltpu.SemaphoreType.DMA((3 * n,)), pltpu.SemaphoreType.DMA((3 * n,)),
                   *[pltpu.HBM(b.shape, b.dtype) for b in bufs], jax.ShapeDtypeStruct((SUBLANE, LANE), F32)),
        in_specs=(HBM,) * n, out_specs=(SEM, SEM) + (HBM,) * n + (pl.BlockSpec(memory_space=pltpu.VMEM),),
        input_output_aliases={w: 2 + w for w in range(n)},
        compiler_params=pltpu.CompilerParams(has_side_effects=EFFECT),
    )(*[_in_hbm(b) for b in bufs])
    return out[0], out[1], list(out[2:2 + n]), out[-1]


def _gather_wait(send_sems, recv_sems, bufs, col_sharded, after, name):
    n = len(bufs)

    def body(*refs):
        ins = refs[:n]
        send_sems, recv_sems = refs[n], refs[n + 1]
        x, y, c, chips = _place()
        q = 2 * x + y
        for w in range(n):
            for k, (cx, cy) in enumerate(chips):
                cp = pltpu.make_async_remote_copy(
                    src_ref=_window(ins[w], col_sharded[w], q, 0),
                    dst_ref=_window(ins[w], col_sharded[w], 2 * cx + cy, 0),
                    send_sem=send_sems.at[3 * w + k], recv_sem=recv_sems.at[3 * w + k],
                    device_id=(cx, cy, c), device_id_type=MESH)
                cp.wait_send()
                cp.wait_recv()

    out = pl.pallas_call(
        body, name=name, out_shape=tuple(pltpu.HBM(b.shape, b.dtype) for b in bufs),
        in_specs=(HBM,) * n + (SEM, SEM) + (ANY,) * len(after), out_specs=(HBM,) * n,
        input_output_aliases={w: w for w in range(n)},
        compiler_params=pltpu.CompilerParams(has_side_effects=EFFECT),
    )(*bufs, send_sems, recv_sems, *after)
    return list(out)


def _scatter_start(grads, col_sharded, name):
    n = len(grads)
    lands = []
    for g, cs in zip(grads, col_sharded):
        K, N = g.shape
        lands.append(lax.empty((3, K, N // N_CHIP) if cs else (3, K // N_CHIP, N), BF16))

    def body(*refs):
        ins, slots = refs[:n], refs[n:2 * n]
        send_sems, recv_sems = refs[2 * n], refs[2 * n + 1]
        token = refs[-1]
        x, y, c, chips = _place()
        for w in range(n):
            for k, (cx, cy) in enumerate(chips):
                pltpu.make_async_remote_copy(
                    src_ref=_window(ins[w], col_sharded[w], 2 * cx + cy, 0), dst_ref=slots[w].at[k],
                    send_sem=send_sems.at[3 * w + k], recv_sem=recv_sems.at[3 * w + k],
                    device_id=(cx, cy, c), device_id_type=MESH).start()
        token[...] = jnp.zeros_like(token)

    out = pl.pallas_call(
        body, name=name,
        out_shape=(pltpu.SemaphoreType.DMA((3 * n,)), pltpu.SemaphoreType.DMA((3 * n,)),
                   *[pltpu.HBM(b.shape, b.dtype) for b in grads], *[pltpu.HBM(b.shape, b.dtype) for b in lands],
                   jax.ShapeDtypeStruct((SUBLANE, LANE), F32)),
        in_specs=(HBM,) * (2 * n),
        out_specs=(SEM, SEM) + (HBM,) * (2 * n) + (pl.BlockSpec(memory_space=pltpu.VMEM),),
        input_output_aliases={w: 2 + w for w in range(2 * n)},
        compiler_params=pltpu.CompilerParams(has_side_effects=EFFECT),
    )(*[_in_hbm(b) for b in grads], *[_in_hbm(b) for b in lands])
    return out[0], out[1], list(out[2:2 + n]), list(out[2 + n:2 + 2 * n]), out[-1]


def _scatter_wait(send_sems, recv_sems, grads, lands, col_sharded, after, name):
    n = len(grads)

    def body(*refs):
        ins, slots = refs[:n], refs[n:2 * n]
        send_sems, recv_sems = refs[2 * n], refs[2 * n + 1]
        x, y, c, chips = _place()
        for w in range(n):
            for k, (cx, cy) in enumerate(chips):
                cp = pltpu.make_async_remote_copy(
                    src_ref=_window(ins[w], col_sharded[w], 2 * cx + cy, 0), dst_ref=slots[w].at[k],
                    send_sem=send_sems.at[3 * w + k], recv_sem=recv_sems.at[3 * w + k],
                    device_id=(cx, cy, c), device_id_type=MESH)
                cp.wait_send()
                cp.wait_recv()

    out = pl.pallas_call(
        body, name=name, out_shape=tuple(pltpu.HBM(b.shape, b.dtype) for b in list(grads) + list(lands)),
        in_specs=(HBM,) * (2 * n) + (SEM, SEM) + (ANY,) * len(after), out_specs=(HBM,) * (2 * n),
        input_output_aliases={w: w for w in range(2 * n)},
        compiler_params=pltpu.CompilerParams(has_side_effects=EFFECT),
    )(*grads, *lands, send_sems, recv_sems, *after)
    return list(out[:n]), list(out[n:])


def _swap_start(arrays, name):
    n = len(arrays)
    lands = [lax.empty(a.shape, a.dtype) for a in arrays]

    def body(*refs):
        ins, lnd = refs[:n], refs[n:2 * n]
        send_sems, recv_sems = refs[2 * n], refs[2 * n + 1]
        token = refs[-1]
        x, y, c, _ = _place()
        for w in range(n):
            pltpu.make_async_remote_copy(
                src_ref=ins[w], dst_ref=lnd[w], send_sem=send_sems.at[w], recv_sem=recv_sems.at[w],
                device_id=(x, y, 1 - c), device_id_type=MESH).start()
        token[...] = jnp.zeros_like(token)

    out = pl.pallas_call(
        body, name=name,
        out_shape=(pltpu.SemaphoreType.DMA((n,)), pltpu.SemaphoreType.DMA((n,)),
                   *[pltpu.HBM(b.shape, b.dtype) for b in arrays], *[pltpu.HBM(b.shape, b.dtype) for b in lands],
                   jax.ShapeDtypeStruct((SUBLANE, LANE), F32)),
        in_specs=(HBM,) * (2 * n),
        out_specs=(SEM, SEM) + (HBM,) * (2 * n) + (pl.BlockSpec(memory_space=pltpu.VMEM),),
        input_output_aliases={w: 2 + w for w in range(2 * n)},
        compiler_params=pltpu.CompilerParams(has_side_effects=EFFECT),
    )(*[_in_hbm(b) for b in arrays], *[_in_hbm(b) for b in lands])
    return out[0], out[1], list(out[2:2 + n]), list(out[2 + n:2 + 2 * n]), out[-1]


def _swap_wait(send_sems, recv_sems, arrays, lands, after, name):
    n = len(arrays)

    def body(*refs):
        ins, lnd = refs[:n], refs[n:2 * n]
        send_sems, recv_sems = refs[2 * n], refs[2 * n + 1]
        x, y, c, _ = _place()
        for w in range(n):
            cp = pltpu.make_async_remote_copy(
                src_ref=ins[w], dst_ref=lnd[w], send_sem=send_sems.at[w], recv_sem=recv_sems.at[w],
                device_id=(x, y, 1 - c), device_id_type=MESH)
            cp.wait_send()
            cp.wait_recv()

    out = pl.pallas_call(
        body, name=name, out_shape=tuple(pltpu.HBM(b.shape, b.dtype) for b in list(arrays) + list(lands)),
        in_specs=(HBM,) * (2 * n) + (SEM, SEM) + (ANY,) * len(after), out_specs=(HBM,) * (2 * n),
        input_output_aliases={w: w for w in range(2 * n)},
        compiler_params=pltpu.CompilerParams(has_side_effects=EFFECT),
    )(*arrays, *lands, send_sems, recv_sems, *after)
    return list(out[:n]), list(out[n:])


BIG = ("w_in", "w_pa", "w_pb", "w_pc", "w_o", "w_13", "w_2")
BIG_COL_SHARDED = (True, True, True, True, False, True, False)
SMALL = ("b_mod", "g_mix", "gm_ln_g", "gm_ln_b", "gm_w_s", "gm_b_s", "pool_w", "pool_scale", "conv_w",
         "g_ffn", "g_final")
WEIGHTS = ("w_mod", "b_mod", "g_mix", "w_in", "gm_ln_g", "gm_ln_b", "gm_w_s", "gm_b_s", "w_pa", "pool_w",
           "pool_scale", "w_pb", "conv_w", "w_pc", "w_o", "g_ffn", "w_13", "w_2", "g_final")


def _pack(arrays, width):
    flat = jnp.concatenate([a.reshape(-1) for a in arrays])
    rows = -(-flat.shape[0] // width)
    rows = -(-rows // SUBLANE) * SUBLANE
    flat = jnp.pad(flat, (0, rows * width - flat.shape[0]))
    return flat.reshape(rows, width)


def _unpack(packed, shapes):
    flat = packed.reshape(-1)
    out, off = [], 0
    for s in shapes:
        size = 1
        for d in s:
            size *= d
        out.append(flat[off:off + size].reshape(s))
        off += size
    return out


def kernel(x, c, w_mod, b_mod, g_mix, w_in, gm_ln_g, gm_ln_b, gm_w_s, gm_b_s, w_pa, pool_w, pool_scale, w_pb, conv_w, w_pc, w_o, g_ffn, w_13, w_2, g_final, loss_target, m_w_mod, m_b_mod, m_g_mix, m_w_in, m_gm_ln_g, m_gm_ln_b, m_gm_w_s, m_gm_b_s, m_w_pa, m_pool_w, m_pool_scale, m_w_pb, m_conv_w, m_w_pc, m_w_o, m_g_ffn, m_w_13, m_w_2, m_g_final, v_w_mod, v_b_mod, v_g_mix, v_w_in, v_gm_ln_g, v_gm_ln_b, v_gm_w_s, v_gm_b_s, v_w_pa, v_pool_w, v_pool_scale, v_w_pb, v_conv_w, v_w_pc, v_w_o, v_g_ffn, v_w_13, v_w_2, v_g_final):
    W = dict(w_mod=w_mod, b_mod=b_mod, g_mix=g_mix, w_in=w_in, gm_ln_g=gm_ln_g, gm_ln_b=gm_ln_b, gm_w_s=gm_w_s,
             gm_b_s=gm_b_s, w_pa=w_pa, pool_w=pool_w, pool_scale=pool_scale, w_pb=w_pb, conv_w=conv_w, w_pc=w_pc,
             w_o=w_o, g_ffn=g_ffn, w_13=w_13, w_2=w_2, g_final=g_final)
    Mo = dict(w_mod=m_w_mod, b_mod=m_b_mod, g_mix=m_g_mix, w_in=m_w_in, gm_ln_g=m_gm_ln_g, gm_ln_b=m_gm_ln_b,
              gm_w_s=m_gm_w_s, gm_b_s=m_gm_b_s, w_pa=m_w_pa, pool_w=m_pool_w, pool_scale=m_pool_scale, w_pb=m_w_pb,
              conv_w=m_conv_w, w_pc=m_w_pc, w_o=m_w_o, g_ffn=m_g_ffn, w_13=m_w_13, w_2=m_w_2, g_final=m_g_final)
    Vo = dict(w_mod=v_w_mod, b_mod=v_b_mod, g_mix=v_g_mix, w_in=v_w_in, gm_ln_g=v_gm_ln_g, gm_ln_b=v_gm_ln_b,
              gm_w_s=v_gm_w_s, gm_b_s=v_gm_b_s, w_pa=v_w_pa, pool_w=v_pool_w, pool_scale=v_pool_scale, w_pb=v_w_pb,
              conv_w=v_conv_w, w_pc=v_w_pc, w_o=v_w_o, g_ffn=v_g_ffn, w_13=v_w_13, w_2=v_w_2, g_final=v_g_final)

    B, S, D = x.shape
    T = B * S
    L = w_in.shape[0]
    Bg = B * N_DEV
    N4 = w_mod.shape[2]
    CW = conv_w.shape[2]
    xi, yi, ci = lax.axis_index("x"), lax.axis_index("y"), lax.axis_index("c")
    chip = 2 * xi + yi
    dev = 2 * chip + ci

    head = _pack([c, conv_w], D)
    hrows = head.shape[0]
    got = _allgather8(head, "gather_c_conv").reshape(N_DEV, hrows * D)
    c_all = got[:, :B * D].reshape(Bg, D)
    conv_parts = got[:, B * D:B * D + L * 3 * CW].reshape(N_CHIP, 2, L, 3, CW)[:, 0]
    conv_full = jnp.transpose(conv_parts, (1, 2, 0, 3)).reshape(L, 3, N_CHIP * CW)

    b_cols = lax.dynamic_slice_in_dim(b_mod, chip * N4, N4, axis=1).reshape(L, 1, N4)
    mod_part = _mod_fwd(c_all, w_mod, b_cols, "mod_fwd")
    half = Bg // 2
    mine = lax.dynamic_slice_in_dim(mod_part, ci * half, half, axis=1)
    mod_got = _allgather8(mine.reshape(L * half, N4), "gather_mod").reshape(N_CHIP, 2, L, half, N4)
    mod_full = jnp.transpose(mod_got, (2, 1, 3, 0, 4)).reshape(L, Bg, 6, D)
    mod_mine = lax.dynamic_slice_in_dim(mod_full, dev * B, B, axis=1)
    mod = jnp.pad(mod_mine, ((0, 0), (0, 0), (0, SUBLANE - 6), (0, 0)))

    PARTS = (("w_in",), ("w_pa", "w_pb", "w_pc", "w_o"), ("w_13", "w_2"))
    groups = [(names, l) for l in range(L) for names in PARTS]
    gathers = []
    order = (mod,)
    for gi, (names, l) in enumerate(groups):
        cs = [BIG_COL_SHARDED[BIG.index(n)] for n in names]
        bufs = [_cast_into_full(W[n], l, c_, chip, "cast_" + n, deps=order) for n, c_ in zip(names, cs)]
        ss, rs, bufs, tok = _gather_start(bufs, cs, f"gather_start_{gi}")
        gathers.append((ss, rs, bufs, names, cs))
        order = (tok,)

    def gathered(gi, after):
        ss, rs, bufs, names, cs = gathers[gi]
        return dict(zip(names, _gather_wait(ss, rs, bufs, cs, after, f"gather_wait_{gi}")))

    def mixer_params(l):
        vec = jnp.zeros((SUBLANE, BR_W), F32)
        vec = vec.at[0].set(gm_ln_g[l]).at[1].set(gm_ln_b[l]).at[2].set(pool_scale[l])
        conv = jnp.zeros((SUBLANE, BR_W), F32).at[0:3].set(conv_full[l])
        b_s = jnp.zeros((CHUNK, LANE), F32).at[:, 0:HEADS].set(jnp.transpose(gm_b_s[l]))
        return dict(vec=vec, conv=conv, w_s=gm_w_s[l], b_s=b_s, pool_w=pool_w[l])

    xs = x.reshape(T, D)
    saved = []
    for l in range(L):
        prm = mixer_params(l)
        full = gathered(3 * l, list(order) if l == 0 else [xs])
        z = _normed_matmul(xs, g_mix[l].reshape(1, D), mod[l], 0, 1, full["w_in"], S, "mm_in",
                           deps=order if l == 0 else ())
        cat = _mixer_fwd(z, prm, S, "mixer_fwd")
        full.update(gathered(3 * l + 1, [cat]))
        merged = _proj_fwd(cat, z, full["w_pa"], full["w_pb"], full["w_pc"], D, "proj_fwd")
        x1, mo = _matmul(merged, full["w_o"], mode="nn", name="mm_o", resid=(xs, mod[l], 2, S))
        full.update(gathered(3 * l + 2, [x1]))
        ga, gb, act = _normed_matmul(x1, g_ffn[l].reshape(1, D), mod[l], 3, 4, full["w_13"], S, "mm_13", swiglu=True)
        x2, ffo = _matmul(act, full["w_2"], mode="nn", name="mm_2", resid=(x1, mod[l], 5, S))
        saved.append(dict(prm=prm, full=full, x0=xs, z=z, cat=cat, merged=merged, mo=mo, x1=x1, ga=ga, gb=gb, act=act,
                          ffo=ffo))
        xs = x2

    dx, head_part = _loss_head(xs, loss_target.reshape(T, D), g_final.reshape(1, D), "loss_head")
    loss = lax.psum(head_part[1, 0], ("x", "y", "c"))

    big_grads = {n: [None] * L for n in BIG}
    small_part = {n: [None] * L for n in SMALL if n not in ("b_mod", "g_final")}
    dmod = [None] * L
    scatters = []
    FFN, MIX = ("w_13", "w_2"), ("w_in", "w_pa", "w_pb", "w_pc", "w_o")

    def scatter(names, l, tag):
        cs = [BIG_COL_SHARDED[BIG.index(n)] for n in names]
        ss, rs, gthru, lands, tok = _scatter_start([big_grads[n][l] for n in names], cs, f"scatter_start_{l}{tag}")
        scatters.append((ss, rs, gthru, lands, names, cs, l))
        return (tok,)

    def exchange_small():
        dmod_l = jnp.stack(dmod, axis=0).reshape(L * B, 6 * D)
        rows = -(-(L * B) // SUBLANE) * SUBLANE
        dmod_got = _allgather8(jnp.pad(dmod_l, ((0, rows - L * B), (0, 0))), "gather_dmod")
        local = dict(b_mod=jnp.sum(jnp.stack(dmod, axis=0), axis=1).reshape(L, 6 * D), g_final=head_part[0])
        for n in small_part:
            local[n] = jnp.stack(small_part[n], axis=0)
        packed = _pack([local[n] for n in SMALL], LANE)
        return dmod_got, rows, _allgather8(packed, "gather_small"), [local[n].shape for n in SMALL]

    sent = ()
    for l in reversed(range(L)):
        sv = saved[l]
        full = sv["full"]
        dffo, pg2, da, db = _gated_dgrad(dx, sv["ffo"], mod[l], 5, full["w_2"], S, "mm_2_dgrad",
                                         gab=(sv["ga"], sv["gb"]), deps=sent)
        dx1, h2, pb2, pgf = _dgrad_norm_bwd([da, db], full["w_13"], sv["x1"], dx, g_ffn[l].reshape(1, D), mod[l],
                                            3, 4, S, "mm_13_dgrad")
        big_grads["w_2"][l] = _matmul(sv["act"], dffo, mode="tn", name="mm_2_wgrad")
        big_grads["w_13"][l] = _wgrad_pair(h2, da, db, "mm_13_wgrad")
        sent = scatter(FFN, l, "a")

        dmo, pg1, dmerged = _gated_dgrad(dx1, sv["mo"], mod[l], 2, full["w_o"], S, "mm_o_dgrad", deps=sent)
        big_grads["w_o"][l] = _matmul(sv["merged"], dmo, mode="tn", name="mm_o_wgrad")
        dy, dgl, dcat = _proj_bwd(dmerged, sv["cat"], sv["z"], full["w_pa"], full["w_pb"], full["w_pc"], D,
                                  "proj_bwd")
        for k, n in enumerate(("w_pa", "w_pb", "w_pc")):
            big_grads[n][l] = _matmul(sv["cat"], dy, mode="tn", name="mm_proj_wgrad",
                                      a_cols=(k * BR_W, BR_W), b_cols=(k * D, D))
        dz, pv, dws, dbs, dpw = _mixer_bwd(sv["z"], dcat, dgl, sv["prm"], S, "mixer_bwd")
        dx0, h, pb1, pgm = _dgrad_norm_bwd([dz], full["w_in"], sv["x0"], dx1, g_mix[l].reshape(1, D), mod[l],
                                           0, 1, S, "mm_in_dgrad")
        dx = dx0

        dmod[l] = jnp.stack([pb1[:, 0], pb1[:, 1], pg1[:, 0], pb2[:, 0], pb2[:, 1], pg2[:, 0]], axis=1)
        small_part["g_mix"][l] = pgm[0]
        small_part["g_ffn"][l] = pgf[0]
        small_part["gm_ln_g"][l] = pv[0]
        small_part["gm_ln_b"][l] = pv[1]
        small_part["pool_scale"][l] = pv[2]
        small_part["conv_w"][l] = pv[3:6]
        small_part["gm_w_s"][l] = dws
        small_part["gm_b_s"][l] = jnp.transpose(dbs[:, 0:HEADS])
        small_part["pool_w"][l] = dpw

        if l == 0:
            dmod_got, dmod_rows, small_got, pshapes = exchange_small()
            last = (dmod_got, small_got)
        else:
            last = ()
        big_grads["w_in"][l] = _matmul(h, dz, mode="tn", name="mm_in_wgrad", deps=last)
        sent = scatter(MIX, l, "b")
    grad_x = dx.reshape(B, S, D)

    results = {}

    dmod_all = dmod_got.reshape(N_DEV, dmod_rows, 6 * D)[:, :L * B].reshape(N_DEV, L, B, 6 * D)
    dmod_all = jnp.transpose(dmod_all, (1, 0, 2, 3)).reshape(L, Bg, 6 * D)
    dmod_cols = lax.dynamic_slice_in_dim(dmod_all, chip * N4, N4, axis=2)
    g_wmod = _mod_wgrad(c_all, dmod_cols, "mod_wgrad", deps=sent)
    res = _adamw(w_mod.reshape(L * D, N4), m_w_mod.reshape(L * D, N4), v_w_mod.reshape(L * D, N4),
                 [g_wmod.reshape(L * D, N4)], "adamw_w_mod")
    results["w_mod"] = [r.reshape(L, D, N4) for r in res]

    gathered_small = small_got.reshape(N_DEV, small_got.shape[0] // N_DEV, LANE)
    g_small = dict(zip(SMALL, _unpack(_sum_devices(gathered_small, "sum_small"), pshapes)))
    g_small["conv_w"] = lax.dynamic_slice_in_dim(g_small["conv_w"], chip * CW, CW, axis=2)
    wshapes = [W[n].shape for n in SMALL]
    res = _adamw(_pack([W[n] for n in SMALL], LANE), _pack([Mo[n] for n in SMALL], LANE),
                 _pack([Vo[n] for n in SMALL], LANE), [_pack([g_small[n] for n in SMALL], LANE)], "adamw_small")
    small_res = [_unpack(r, wshapes) for r in res]
    for i, n in enumerate(SMALL):
        results[n] = [small_res[j][i] for j in range(4)]

    stacks = {n: lax.empty((W[n].shape[0] * W[n].shape[1], W[n].shape[2]), F32) for n in BIG}
    after = [res[0], results["w_mod"][0]]
    swaps = []
    for part in (FFN, MIX):
        for ss, rs, gthru, lands, names, cs, l in scatters:
            if names != part:
                continue
            gthru, lands = _scatter_wait(ss, rs, gthru, lands, cs, after, f"scatter_wait_{l}_{names[0]}")
            for n, g, ld, c_ in zip(names, gthru, lands, cs):
                stacks[n] = _sum_into(stacks[n], g, ld, l, c_, chip, "sum_" + n)
            after = [stacks[names[-1]]]
        ss, rs, mine, theirs, tok = _swap_start([stacks[n] for n in part], f"swap_start_{part[0]}")
        swaps.append((ss, rs, mine, theirs, part))
        after = after + [tok]
    for ss, rs, mine, theirs, part in swaps:
        mine, theirs = _swap_wait(ss, rs, mine, theirs, after, f"swap_wait_{part[0]}")
        for n, own, other in zip(part, mine, theirs):
            _, R, C = W[n].shape
            res = _adamw(W[n].reshape(L * R, C), Mo[n].reshape(L * R, C), Vo[n].reshape(L * R, C), [own, other],
                         "adamw_" + n)
            results[n] = [r.reshape(L, R, C) for r in res]
        after = [res[0]]

    return (loss, grad_x, *[results[n][0] for n in WEIGHTS], *[results[n][1] for n in WEIGHTS],
            *[results[n][2] for n in WEIGHTS], *[results[n][3] for n in WEIGHTS])
```

```python
import jax
import jax.numpy as jnp
from jax import lax
from jax.experimental import pallas as pl
from jax.experimental.pallas import tpu as pltpu

F32 = jnp.float32
BF16 = jnp.bfloat16
MESH = pl.DeviceIdType.MESH

EPS = 1e-6
CHUNK = 128
HEADS = 4
HEAD_DIM = 128
BR_W = 512
N_GROUP = 4
GROUP_DIM = 128
HALO = 16
N_SPLIT = 6 * BR_W
N_CHIP = 4
N_DEV = 8

ADAM_LR = 0.001
ADAM_B1 = 0.9
ADAM_B2 = 0.999
ADAM_EPS = 1e-08
ADAM_WD = 0.01
ADAM_STEP = 10

V7X_VMEM_LIMIT = 56 * 1024 * 1024
LANE = 128
SUBLANE = 8

GELU_K = 0.7978845608028654
GELU_C = 0.044715


def _cparams(sem):
    return pltpu.CompilerParams(dimension_semantics=sem, vmem_limit_bytes=V7X_VMEM_LIMIT)


def _pick(n, cap, q=LANE):
    best = None
    d = q
    while d <= min(n, cap):
        if n % d == 0:
            best = d
        d += q
    return n if best is None else best


def _sigmoid(x):
    return 0.5 * jnp.tanh(0.5 * x) + 0.5


def _gelu(x):
    t = jnp.tanh(GELU_K * (x + GELU_C * x * x * x))
    return 0.5 * x * (1.0 + t), t


def _gelu_grad(x, t):
    return 0.5 * (1.0 + t) + 0.5 * x * (1.0 - t * t) * GELU_K * (1.0 + 3.0 * GELU_C * x * x)


def _matmul(a, b, *, mode, name, out_dtype=None, layer=None, a_cols=None, b_cols=None,
            resid=None, deps=(), tm_cap=None, tn_cap=1536, tk_cap=1536):
    out_dtype = BF16 if out_dtype is None else out_dtype
    b2 = b.shape[-2:]
    if tm_cap is None:
        k_len = a.shape[0] if mode == "tn" else a.shape[1]
        if mode == "tn":
            m_len = a.shape[1] if a_cols is None else a_cols[1]
            tm_cap, tk_cap = 1536, 2048
            if m_len <= 1024:
                tn_cap, tk_cap = 1024, 4096
        else:
            tm_cap, tk_cap = (512, 8192) if k_len > 1536 else (1024, 1536)
    if mode == "nn":
        M, K = a.shape
        N = b2[1]
    elif mode == "nt":
        M, K = a.shape
        N = b2[0]
    else:
        K = a.shape[0]
        M = a.shape[1] if a_cols is None else a_cols[1]
        N = b2[1] if b_cols is None else b_cols[1]
    tm = _pick(M if resid is None else resid[3], tm_cap)
    tn = _pick(N, tn_cap)
    tk = _pick(K, tk_cap)
    nk = K // tk
    a_off = 0 if a_cols is None else a_cols[0] // tm
    b_off = 0 if b_cols is None else b_cols[0] // tn
    if a_cols is not None:
        assert a_cols[0] % tm == 0
    if b_cols is not None:
        assert b_cols[0] % tn == 0

    if mode == "nn":
        a_spec = pl.BlockSpec((tm, tk), lambda i, j, k: (i, k))
        b_blk, b_idx = (tk, tn), (lambda i, j, k: (k, j))
        dims = (((1,), (0,)), ((), ()))
    elif mode == "nt":
        a_spec = pl.BlockSpec((tm, tk), lambda i, j, k: (i, k))
        b_blk, b_idx = (tn, tk), (lambda i, j, k: (j, k))
        dims = (((1,), (1,)), ((), ()))
    else:
        a_spec = pl.BlockSpec((tk, tm), lambda i, j, k: (k, i + a_off))
        b_blk, b_idx = (tk, tn), (lambda i, j, k: (k, j + b_off))
        dims = (((0,), (0,)), ((), ()))
    if layer is None:
        b_spec = pl.BlockSpec(b_blk, b_idx)
    else:
        b_spec = pl.BlockSpec((None,) + b_blk, lambda i, j, k: (layer,) + b_idx(i, j, k))

    in_specs = [a_spec, b_spec]
    operands = [a, b]
    o_spec = pl.BlockSpec((tm, tn), lambda i, j, k: (i, j))
    if resid is not None:
        x, mod, row, seq = resid
        D = mod.shape[-1]
        in_specs += [o_spec, pl.BlockSpec((1, SUBLANE, tn), lambda i, j, k: ((i * tm) // seq, 0, j))]
        operands += [x, mod]
        out_shape = (jax.ShapeDtypeStruct((M, N), F32), jax.ShapeDtypeStruct((M, N), BF16))
        out_specs = (o_spec, o_spec)
        assert seq % tm == 0 and D == N
    else:
        out_shape = jax.ShapeDtypeStruct((M, N), out_dtype)
        out_specs = o_spec

    def finish(acc, refs):
        if resid is not None:
            x_ref, mod_ref, o_ref, p_ref = refs
            o_ref[...] = x_ref[...] + mod_ref[0, row:row + 1, :] * acc
            p_ref[...] = acc.astype(BF16)
        else:
            (o_ref,) = refs
            o_ref[...] = acc.astype(out_dtype)

    n_in = len(operands) - 2
    in_specs += [ANY] * len(deps)
    operands += list(deps)

    def body(a_ref, b_ref, *refs):
        refs = refs[:n_in] + refs[n_in + len(deps):]
        part = lax.dot_general(a_ref[...], b_ref[...], dims, preferred_element_type=F32)
        if nk == 1:
            finish(part, refs)
            return
        acc_ref = refs[-1]
        k = pl.program_id(2)

        @pl.when(k == 0)
        def _():
            acc_ref[...] = part

        @pl.when(k > 0)
        def _():
            acc_ref[...] += part

        @pl.when(k == nk - 1)
        def _():
            finish(acc_ref[...], refs[:-1])

    scratch = [] if nk == 1 else [pltpu.VMEM((tm, tn), F32)]
    return pl.pallas_call(
        body, name=name, grid=(M // tm, N // tn, nk), in_specs=in_specs, out_specs=out_specs,
        out_shape=out_shape, scratch_shapes=scratch,
        compiler_params=_cparams(("parallel", "parallel", "arbitrary")),
    )(*operands)


def _row_tile(seq, cap):
    return _pick(seq, cap, HALO)


def _mod_spec(tm, seq, D):
    return pl.BlockSpec((1, SUBLANE, D), lambda i: ((i * tm) // seq, 0, 0))


def _normed_matmul(x, g, mod, shift_row, scale_row, w, seq, name, swiglu=False, deps=()):
    T, D = x.shape
    N = w.shape[1] // 2 if swiglu else w.shape[1]
    tm = _pick(seq, 512 if swiglu else 1024)
    tn = _pick(N, 4096 if swiglu else 1536)
    nj = N // tn
    n_w = 2 if swiglu else 1

    def body(x_ref, g_ref, mod_ref, *rest):
        w_refs = rest[:n_w]
        outs = rest[n_w + len(deps):-1]
        h_ref = rest[-1]

        @pl.when(pl.program_id(1) == 0)
        def _():
            xv = x_ref[...]
            rstd = lax.rsqrt(jnp.mean(xv * xv, axis=-1, keepdims=True) + EPS)
            n = xv * rstd * g_ref[...]
            h = n * (1.0 + mod_ref[0, scale_row:scale_row + 1, :]) + mod_ref[0, shift_row:shift_row + 1, :]
            h_ref[...] = h.astype(BF16)

        h = h_ref[...]
        if not swiglu:
            outs[0][...] = jnp.dot(h, w_refs[0][...], preferred_element_type=F32).astype(BF16)
            return
        a16 = jnp.dot(h, w_refs[0][...], preferred_element_type=F32).astype(BF16)
        b16 = jnp.dot(h, w_refs[1][...], preferred_element_type=F32).astype(BF16)
        outs[0][...] = a16
        outs[1][...] = b16
        outs[2][...] = a16 * _sigmoid(a16) * b16

    once = pl.Buffered(1) if nj == 1 else None
    w_specs = [pl.BlockSpec((D, tn), lambda i, j: (0, j), pipeline_mode=once)]
    if swiglu:
        w_specs.append(pl.BlockSpec((D, tn), lambda i, j: (0, j + nj), pipeline_mode=once))
    o_spec = pl.BlockSpec((tm, tn), lambda i, j: (i, j))
    n_out = 3 if swiglu else 1
    out = pl.pallas_call(
        body, name=name, grid=(T // tm, nj),
        in_specs=[pl.BlockSpec((tm, D), lambda i, j: (i, 0)), pl.BlockSpec((1, D), lambda i, j: (0, 0)),
                  pl.BlockSpec((1, SUBLANE, D), lambda i, j: ((i * tm) // seq, 0, 0))] + w_specs + [ANY] * len(deps),
        out_specs=(o_spec,) * n_out, out_shape=(jax.ShapeDtypeStruct((T, N), BF16),) * n_out,
        scratch_shapes=[pltpu.VMEM((tm, D), BF16)],
        compiler_params=_cparams(("parallel", "arbitrary")),
    )(x, g, mod, *([w] * n_w), *deps)
    return out if swiglu else out[0]


def _dgrad_norm_bwd(parts, w, x, dres, g, mod, shift_row, scale_row, seq, name):
    T, D = x.shape
    B = mod.shape[0]
    n = len(parts)
    Kp = parts[0].shape[1]
    tm = _pick(seq, 512)
    per_seq = seq // tm

    def body(*refs):
        p_refs, w_refs = refs[:len(parts)], refs[len(parts):2 * len(parts)]
        x_ref, dres_ref, g_ref, mod_ref, dx_ref, h_ref, pb_ref, pg_ref = refs[2 * len(parts):]
        i = pl.program_id(0)
        dims = (((1,), (1,)), ((), ()))
        dhv = lax.dot_general(p_refs[0][...], w_refs[0][...], dims, preferred_element_type=F32)
        for p_ref, w_ref in zip(p_refs[1:], w_refs[1:]):
            dhv = dhv + lax.dot_general(p_ref[...], w_ref[...], dims, preferred_element_type=F32)
        xv = x_ref[...]
        gv = g_ref[...]
        scale1 = 1.0 + mod_ref[0, scale_row:scale_row + 1, :]
        rstd = lax.rsqrt(jnp.mean(xv * xv, axis=-1, keepdims=True) + EPS)
        xhat = xv * rstd
        n = xhat * gv
        dn = dhv * scale1
        dxhat = dn * gv
        dx = rstd * (dxhat - xhat * jnp.mean(dxhat * xhat, axis=-1, keepdims=True))
        dx_ref[...] = dres_ref[...] + dx
        h_ref[...] = (n * scale1 + mod_ref[0, shift_row:shift_row + 1, :]).astype(BF16)

        @pl.when(i % per_seq == 0)
        def _():
            pb_ref[...] = jnp.zeros_like(pb_ref)

        @pl.when(i == 0)
        def _():
            pg_ref[...] = jnp.zeros_like(pg_ref)

        pb_ref[0, 0:1, :] += jnp.sum(dhv, axis=0, keepdims=True)
        pb_ref[0, 1:2, :] += jnp.sum(dhv * n, axis=0, keepdims=True)
        pg_ref[0:1, :] += jnp.sum(dn * xhat, axis=0, keepdims=True)

    row = pl.BlockSpec((tm, D), lambda i: (i, 0))
    p_specs = [pl.BlockSpec((tm, Kp), lambda i: (i, 0))] * n
    w_specs = [pl.BlockSpec((D, Kp), lambda i, p=p: (0, p), pipeline_mode=pl.Buffered(1)) for p in range(n)]
    return pl.pallas_call(
        body, name=name, grid=(T // tm,),
        in_specs=p_specs + w_specs + [row, row, pl.BlockSpec((1, D), lambda i: (0, 0)), _mod_spec(tm, seq, D)],
        out_specs=(row, row, _mod_spec(tm, seq, D), pl.BlockSpec((SUBLANE, D), lambda i: (0, 0))),
        out_shape=(jax.ShapeDtypeStruct((T, D), F32), jax.ShapeDtypeStruct((T, D), BF16),
                   jax.ShapeDtypeStruct((B, SUBLANE, D), F32), jax.ShapeDtypeStruct((SUBLANE, D), F32)),
        compiler_params=_cparams(("arbitrary",)),
    )(*parts, *([w] * n), x, dres, g, mod)


def _gated_dgrad(dx, prod, mod, gate_row, w, seq, name, gab=None, deps=()):
    T, D = dx.shape
    B = mod.shape[0]
    N = w.shape[0]
    tm = _pick(seq, 512 if gab is None else 256)
    tn = _pick(N, 4096)
    per_seq = seq // tm
    n_gab = 0 if gab is None else 2

    def body(dx_ref, p_ref, mod_ref, w_ref, *rest):
        gab_refs = rest[:n_gab]
        dp_ref, pb_ref = rest[n_gab + len(deps):n_gab + len(deps) + 2]
        outs = rest[n_gab + len(deps) + 2:-1]
        a_ref = rest[-1]
        i = pl.program_id(0)

        @pl.when(pl.program_id(1) == 0)
        def _():
            dxv = dx_ref[...]
            dp = (dxv * mod_ref[0, gate_row:gate_row + 1, :]).astype(BF16)
            a_ref[...] = dp
            dp_ref[...] = dp

            @pl.when(i % per_seq == 0)
            def _():
                pb_ref[...] = jnp.zeros_like(pb_ref)

            pb_ref[0, 0:1, :] += jnp.sum(dxv * p_ref[...].astype(F32), axis=0, keepdims=True)

        du = lax.dot_general(a_ref[...], w_ref[...], (((1,), (1,)), ((), ())), preferred_element_type=F32)
        if gab is None:
            outs[0][...] = du.astype(BF16)
            return
        du = du.astype(BF16)
        a = gab_refs[0][...]
        b = gab_refs[1][...]
        sg = _sigmoid(a)
        dsg = du * sg
        outs[0][...] = dsg * b * (1.0 + a * (1.0 - sg))
        outs[1][...] = dsg * a

    row = pl.BlockSpec((tm, D), lambda i, j: (i, 0))
    tile = pl.BlockSpec((tm, tn), lambda i, j: (i, j))
    mod_spec = pl.BlockSpec((1, SUBLANE, D), lambda i, j: ((i * tm) // seq, 0, 0))
    n_out = 1 if gab is None else 2
    out = pl.pallas_call(
        body, name=name, grid=(T // tm, N // tn),
        in_specs=[row, row, mod_spec,
                  pl.BlockSpec((tn, D), lambda i, j: (j, 0), pipeline_mode=pl.Buffered(1) if N == tn else None)]
        + [tile] * n_gab
        + [ANY] * len(deps),
        out_specs=(row, mod_spec) + (tile,) * n_out,
        out_shape=(jax.ShapeDtypeStruct((T, D), BF16), jax.ShapeDtypeStruct((B, SUBLANE, D), F32))
        + (jax.ShapeDtypeStruct((T, N), BF16),) * n_out,
        scratch_shapes=[pltpu.VMEM((tm, D), BF16)],
        compiler_params=_cparams(("arbitrary", "arbitrary")),
    )(dx, prod, mod, w, *(gab or ()), *deps)
    return out


def _wgrad_pair(h, da, db, name):
    T, D = h.shape
    Fh = da.shape[1]
    tn = _pick(Fh, 1536)
    tk = _pick(T, 2048)
    half = Fh // tn
    nk = T // tk

    def body(h_ref, da_ref, db_ref, o_ref, acc_ref):
        j, k = pl.program_id(0), pl.program_id(1)
        dims = (((0,), (0,)), ((), ()))

        def accumulate(g_ref):
            part = lax.dot_general(h_ref[...], g_ref[...], dims, preferred_element_type=F32)

            @pl.when(k == 0)
            def _():
                acc_ref[...] = part

            @pl.when(k > 0)
            def _():
                acc_ref[...] += part

        @pl.when(j < half)
        def _():
            accumulate(da_ref)

        @pl.when(j >= half)
        def _():
            accumulate(db_ref)

        @pl.when(k == nk - 1)
        def _():
            o_ref[...] = acc_ref[...].astype(BF16)

    a_idx = lambda j, k: (jnp.where(j < half, k, 0), jnp.minimum(j, half - 1))
    b_idx = lambda j, k: (jnp.where(j >= half, k, 0), jnp.maximum(j - half, 0))
    return pl.pallas_call(
        body, name=name, grid=(2 * half, nk),
        in_specs=[pl.BlockSpec((tk, D), lambda j, k: (k, 0)), pl.BlockSpec((tk, tn), a_idx),
                  pl.BlockSpec((tk, tn), b_idx)],
        out_specs=pl.BlockSpec((D, tn), lambda j, k: (0, j)),
        out_shape=jax.ShapeDtypeStruct((D, 2 * Fh), BF16), scratch_shapes=[pltpu.VMEM((D, tn), F32)],
        compiler_params=_cparams(("parallel", "arbitrary")),
    )(h, da, db)


def _shift_down(v, d):
    return pltpu.roll(v, d, 0)


def _shift_up(v, d):
    return pltpu.roll(v, v.shape[0] - d, 0)


def _tril_mask():
    r = lax.broadcasted_iota(jnp.int32, (CHUNK, CHUNK), 0)
    c = lax.broadcasted_iota(jnp.int32, (CHUNK, CHUNK), 1)
    return c <= r


def _pool_counts(i, tm, seq, rows, first_row):
    r = lax.broadcasted_iota(jnp.int32, (rows, 1), 0) + (i * tm + first_row)
    pos1 = (r % seq + 1).astype(F32)
    return [jnp.minimum(pos1, float(2 << g)) for g in range(N_GROUP)]


def _mixer_forward_values(z_ref, hxb, hch, i, tm, seq, ln_g, ln_b, ws_ref, bs_ref, pw_ref, pscale, cw_ref):
    u = z_ref[:, 0 * BR_W:1 * BR_W]
    v = z_ref[:, 1 * BR_W:2 * BR_W]
    xb = z_ref[:, 2 * BR_W:3 * BR_W].astype(F32)
    bg = z_ref[:, 3 * BR_W:4 * BR_W].astype(F32)
    cg = z_ref[:, 4 * BR_W:5 * BR_W].astype(F32)
    hc = z_ref[:, 5 * BR_W:6 * BR_W].astype(F32)
    out = {}

    ug, tu = _gelu(u)
    vg, tv = _gelu(v)
    vg = vg.astype(F32)
    mu = jnp.mean(vg, axis=-1, keepdims=True)
    vc = vg - mu
    rstd = lax.rsqrt(jnp.mean(vc * vc, axis=-1, keepdims=True) + EPS)
    vhat = vc * rstd
    vn = (vhat * ln_g + ln_b).astype(BF16)
    mask = _tril_mask()
    wt = [jnp.where(mask, ws_ref[h], 0.0).astype(BF16) for h in range(HEADS)]
    rows = []
    for n in range(tm // CHUNK):
        blocks = []
        for h in range(HEADS):
            blk = vn[n * CHUNK:(n + 1) * CHUNK, h * HEAD_DIM:(h + 1) * HEAD_DIM]
            sb = jnp.dot(wt[h], blk, preferred_element_type=F32) + bs_ref[:, h:h + 1]
            blocks.append(sb)
        rows.append(jnp.concatenate(blocks, axis=1))
    s = jnp.concatenate(rows, axis=0) if len(rows) > 1 else rows[0]
    out.update(u=u, v=v, ug=ug, tu=tu, tv=tv, rstd=rstd, vhat=vhat, vn=vn, wt=wt, s=s, a_out=ug * s)

    ext = jnp.concatenate([hxb, xb], axis=0)
    cnt = _pool_counts(i, tm, seq, tm, 0)
    p, qs = [], []
    for g in range(N_GROUP):
        e = ext[:, g * GROUP_DIM:(g + 1) * GROUP_DIM]
        acc = e
        for d in (1, 2, 4, 8)[:g + 1]:
            acc = acc + _shift_down(acc, d)
        pg = acc[HALO:, :] / cnt[g] - xb[:, g * GROUP_DIM:(g + 1) * GROUP_DIM]
        p.append(pg.astype(BF16))
        qs.append(jnp.dot(p[g], pw_ref[g].astype(BF16), preferred_element_type=F32))
    q = jnp.concatenate(qs, axis=1)
    out.update(p=p, q=q, b_out=q * pscale)

    zc = cg * hc
    zce = jnp.concatenate([hch[:, :BR_W] * hch[:, BR_W:], zc], axis=0)
    z1 = _shift_down(zce, 1)[HALO:, :]
    z2 = _shift_down(zce, 2)[HALO:, :]
    y = cw_ref[0:1, :] * z2 + cw_ref[1:2, :] * z1 + cw_ref[2:3, :] * zc
    out.update(bg=bg, cg=cg, hc=hc, zc=zc, z1=z1, z2=z2, y=y, c_out=bg * y)
    return out


def _mixer_specs(tm, T):
    nb = T // HALO
    per = tm // HALO
    prev = lambda i: jnp.maximum(i * per - 1, 0)
    nxt = lambda i: jnp.minimum((i + 1) * per, nb - 1)
    return prev, nxt


def _mixer_fwd(z, prm, seq, name):
    T = z.shape[0]
    tm = _row_tile(seq, 256)
    per_seq = seq // tm
    prev, _ = _mixer_specs(tm, T)

    def body(z_ref, hxb_ref, hch_ref, vec_ref, cw_ref, ws_ref, bs_ref, pw_ref, cat_ref):
        i = pl.program_id(0)
        keep = jnp.where(i % per_seq == 0, 0.0, 1.0)
        hxb = hxb_ref[...].astype(F32) * keep
        hch = hch_ref[...].astype(F32) * keep
        o = _mixer_forward_values(z_ref, hxb, hch, i, tm, seq, vec_ref[0:1, :], vec_ref[1:2, :],
                                  ws_ref, bs_ref, pw_ref, vec_ref[2:3, :], cw_ref)
        cat_ref[:, 0 * BR_W:1 * BR_W] = o["a_out"].astype(BF16)
        cat_ref[:, 1 * BR_W:2 * BR_W] = o["b_out"].astype(BF16)
        cat_ref[:, 2 * BR_W:3 * BR_W] = o["c_out"].astype(BF16)

    full = lambda shape: pl.BlockSpec(shape, lambda i: (0,) * len(shape))
    return pl.pallas_call(
        body, name=name, grid=(T // tm,),
        in_specs=[pl.BlockSpec((tm, N_SPLIT), lambda i: (i, 0)),
                  pl.BlockSpec((HALO, BR_W), lambda i: (prev(i), 2)),
                  pl.BlockSpec((HALO, 2 * BR_W), lambda i: (prev(i), 2)),
                  full((SUBLANE, BR_W)), full((SUBLANE, BR_W)), full((HEADS, CHUNK, CHUNK)),
                  full((CHUNK, LANE)), full((N_GROUP, GROUP_DIM, GROUP_DIM))],
        out_specs=pl.BlockSpec((tm, 3 * BR_W), lambda i: (i, 0)),
        out_shape=jax.ShapeDtypeStruct((T, 3 * BR_W), BF16),
        compiler_params=_cparams(("parallel",)),
    )(z, z, z, prm["vec"], prm["conv"], prm["w_s"], prm["b_s"], prm["pool_w"])


def _mixer_bwd(z, dcat, dgl, prm, seq, name):
    T, IN = z.shape
    GL = IN - N_SPLIT
    tm = _row_tile(seq, 256)
    per_seq = seq // tm
    prev, nxt = _mixer_specs(tm, T)
    nrow = tm + HALO

    def body(z_ref, hxb_ref, hch_ref, nbg_ref, dcat_ref, ndb_ref, ndc_ref, dgl_ref,
             vec_ref, cw_ref, ws_ref, bs_ref, pw_ref,
             dz_ref, pv_ref, dws_ref, dbs_ref, dpw_ref):
        i = pl.program_id(0)
        keep_prev = jnp.where(i % per_seq == 0, 0.0, 1.0)
        keep_next = jnp.where(i % per_seq == per_seq - 1, 0.0, 1.0)
        hxb = hxb_ref[...].astype(F32) * keep_prev
        hch = hch_ref[...].astype(F32) * keep_prev
        ln_g = vec_ref[0:1, :]
        pscale = vec_ref[2:3, :]
        o = _mixer_forward_values(z_ref, hxb, hch, i, tm, seq, ln_g, vec_ref[1:2, :],
                                  ws_ref, bs_ref, pw_ref, pscale, cw_ref)
        dcv = dcat_ref[...].astype(F32)
        da = dcv[:, 0 * BR_W:1 * BR_W]
        db = dcv[:, 1 * BR_W:2 * BR_W]
        dc = dcv[:, 2 * BR_W:3 * BR_W]
        mask = _tril_mask()

        @pl.when(i == 0)
        def _():
            pv_ref[...] = jnp.zeros_like(pv_ref)
            dws_ref[...] = jnp.zeros_like(dws_ref)
            dbs_ref[...] = jnp.zeros_like(dbs_ref)
            dpw_ref[...] = jnp.zeros_like(dpw_ref)

        d_ug = da * o["s"]
        ds = da * o["ug"]
        ds_b = ds.astype(BF16)
        vn = o["vn"]
        dvn_rows = []
        dws = [jnp.zeros((CHUNK, CHUNK), F32) for _ in range(HEADS)]
        dsum = jnp.zeros((CHUNK, BR_W), F32)
        for n in range(tm // CHUNK):
            blocks = []
            rs = slice(n * CHUNK, (n + 1) * CHUNK)
            dsum = dsum + ds[rs, :]
            for h in range(HEADS):
                cs = slice(h * HEAD_DIM, (h + 1) * HEAD_DIM)
                dsb = ds_b[rs, cs]
                blocks.append(lax.dot_general(o["wt"][h], dsb, (((0,), (0,)), ((), ())),
                                              preferred_element_type=F32))
                dws[h] = dws[h] + lax.dot_general(dsb, vn[rs, cs], (((1,), (1,)), ((), ())),
                                                  preferred_element_type=F32)
            dvn_rows.append(jnp.concatenate(blocks, axis=1))
        dvn = jnp.concatenate(dvn_rows, axis=0) if len(dvn_rows) > 1 else dvn_rows[0]
        lane = lax.broadcasted_iota(jnp.int32, (CHUNK, LANE), 1)
        dbs_t = jnp.zeros((CHUNK, LANE), F32)
        for h in range(HEADS):
            dws_ref[h] += jnp.where(mask, dws[h], 0.0)
            rsum = jnp.sum(dsum[:, h * HEAD_DIM:(h + 1) * HEAD_DIM], axis=1, keepdims=True)
            dbs_t = dbs_t + jnp.where(lane == h, rsum, 0.0)
        dbs_ref[...] += dbs_t
        vhat = o["vhat"]
        pv_ref[0:1, :] += jnp.sum(dvn * vhat, axis=0, keepdims=True)
        pv_ref[1:2, :] += jnp.sum(dvn, axis=0, keepdims=True)
        dvhat = dvn * ln_g
        dvg = o["rstd"] * (dvhat - jnp.mean(dvhat, axis=-1, keepdims=True)
                           - vhat * jnp.mean(dvhat * vhat, axis=-1, keepdims=True))
        du = d_ug * _gelu_grad(o["u"], o["tu"])
        dv = dvg * _gelu_grad(o["v"], o["tv"])

        pv_ref[2:3, :] += jnp.sum(db * o["q"], axis=0, keepdims=True)
        dq = (db * pscale).astype(BF16)
        dqn = (ndb_ref[...].astype(F32) * pscale * keep_next).astype(BF16)
        cnt = _pool_counts(i, tm, seq, nrow, 0)
        dxb = []
        for g in range(N_GROUP):
            cs = slice(g * GROUP_DIM, (g + 1) * GROUP_DIM)
            pwg = pw_ref[g].astype(BF16)
            dpw_ref[g] += lax.dot_general(o["p"][g], dq[:, cs], (((0,), (0,)), ((), ())),
                                          preferred_element_type=F32)
            dp = lax.dot_general(dq[:, cs], pwg, (((1,), (1,)), ((), ())), preferred_element_type=F32)
            dpn = lax.dot_general(dqn[:, cs], pwg, (((1,), (1,)), ((), ())), preferred_element_type=F32)
            acc = jnp.concatenate([dp, dpn], axis=0) / cnt[g]
            for d in (1, 2, 4, 8)[:g + 1]:
                acc = acc + _shift_up(acc, d)
            dxb.append(acc[:tm, :] - dp)
        dxb = jnp.concatenate(dxb, axis=1)

        dbg = dc * o["y"]
        dy = dc * o["bg"]
        pv_ref[3:4, :] += jnp.sum(dy * o["z2"], axis=0, keepdims=True)
        pv_ref[4:5, :] += jnp.sum(dy * o["z1"], axis=0, keepdims=True)
        pv_ref[5:6, :] += jnp.sum(dy * o["zc"], axis=0, keepdims=True)
        dyn = ndc_ref[...].astype(F32) * nbg_ref[...].astype(F32) * keep_next
        dye = jnp.concatenate([dy, dyn], axis=0)
        dzc = (cw_ref[2:3, :] * dy + cw_ref[1:2, :] * _shift_up(dye, 1)[:tm, :]
               + cw_ref[0:1, :] * _shift_up(dye, 2)[:tm, :])
        dcg = dzc * o["hc"]
        dhc = dzc * o["cg"]

        dz_ref[:, 0 * BR_W:1 * BR_W] = du.astype(BF16)
        dz_ref[:, 1 * BR_W:2 * BR_W] = dv.astype(BF16)
        dz_ref[:, 2 * BR_W:3 * BR_W] = dxb.astype(BF16)
        dz_ref[:, 3 * BR_W:4 * BR_W] = dbg.astype(BF16)
        dz_ref[:, 4 * BR_W:5 * BR_W] = dcg.astype(BF16)
        dz_ref[:, 5 * BR_W:6 * BR_W] = dhc.astype(BF16)
        dz_ref[:, N_SPLIT:] = dgl_ref[...]

    full = lambda shape: pl.BlockSpec(shape, lambda i: (0,) * len(shape))
    return pl.pallas_call(
        body, name=name, grid=(T // tm,),
        in_specs=[pl.BlockSpec((tm, N_SPLIT), lambda i: (i, 0)),
                  pl.BlockSpec((HALO, BR_W), lambda i: (prev(i), 2)),
                  pl.BlockSpec((HALO, 2 * BR_W), lambda i: (prev(i), 2)),
                  pl.BlockSpec((HALO, BR_W), lambda i: (nxt(i), 3)),
                  pl.BlockSpec((tm, 3 * BR_W), lambda i: (i, 0)),
                  pl.BlockSpec((HALO, BR_W), lambda i: (nxt(i), 1)),
                  pl.BlockSpec((HALO, BR_W), lambda i: (nxt(i), 2)),
                  pl.BlockSpec((tm, GL), lambda i: (i, 0)),
                  full((SUBLANE, BR_W)), full((SUBLANE, BR_W)), full((HEADS, CHUNK, CHUNK)),
                  full((CHUNK, LANE)), full((N_GROUP, GROUP_DIM, GROUP_DIM))],
        out_specs=(pl.BlockSpec((tm, IN), lambda i: (i, 0)),
                   full((SUBLANE, BR_W)), full((HEADS, CHUNK, CHUNK)), full((CHUNK, LANE)),
                   full((N_GROUP, GROUP_DIM, GROUP_DIM))),
        out_shape=(jax.ShapeDtypeStruct((T, IN), BF16),
                   jax.ShapeDtypeStruct((SUBLANE, BR_W), F32),
                   jax.ShapeDtypeStruct((HEADS, CHUNK, CHUNK), F32),
                   jax.ShapeDtypeStruct((CHUNK, LANE), F32),
                   jax.ShapeDtypeStruct((N_GROUP, GROUP_DIM, GROUP_DIM), F32)),
        compiler_params=_cparams(("arbitrary",)),
    )(z, z, z, z, dcat, dcat, dcat, dgl, prm["vec"], prm["conv"], prm["w_s"], prm["b_s"], prm["pool_w"])


def _proj_fwd(cat, z, w_pa, w_pb, w_pc, D, name):
    T, IN = z.shape
    GL = 3 * D
    assert N_SPLIT % GL == 0
    glb = N_SPLIT // GL
    tm = _pick(T, 512, HALO)

    def body(cat_ref, gl_ref, wa_ref, wb_ref, wc_ref, m_ref):
        acc = jnp.zeros((tm, D), F32)
        for k, w_ref in enumerate((wa_ref, wb_ref, wc_ref)):
            y = jnp.dot(cat_ref[:, k * BR_W:(k + 1) * BR_W], w_ref[...], preferred_element_type=F32)
            acc = acc + _sigmoid(gl_ref[:, k * D:(k + 1) * D]).astype(F32) * y
        m_ref[...] = acc.astype(BF16)

    wspec = pl.BlockSpec((BR_W, D), lambda i: (0, 0))
    return pl.pallas_call(
        body, name=name, grid=(T // tm,),
        in_specs=[pl.BlockSpec((tm, 3 * BR_W), lambda i: (i, 0)),
                  pl.BlockSpec((tm, GL), lambda i: (i, glb)), wspec, wspec, wspec],
        out_specs=pl.BlockSpec((tm, D), lambda i: (i, 0)),
        out_shape=jax.ShapeDtypeStruct((T, D), BF16),
        compiler_params=_cparams(("parallel",)),
    )(cat, z, w_pa, w_pb, w_pc)


def _proj_bwd(dmerged, cat, z, w_pa, w_pb, w_pc, D, name):
    T, IN = z.shape
    GL = 3 * D
    glb = N_SPLIT // GL
    tm = _pick(T, 512, HALO)

    def body(dm_ref, cat_ref, gl_ref, wa_ref, wb_ref, wc_ref, dy_ref, dgl_ref, dcat_ref):
        dm = dm_ref[...]
        for k, w_ref in enumerate((wa_ref, wb_ref, wc_ref)):
            w = w_ref[...]
            y = jnp.dot(cat_ref[:, k * BR_W:(k + 1) * BR_W], w, preferred_element_type=F32).astype(BF16)
            sg = _sigmoid(gl_ref[:, k * D:(k + 1) * D])
            dyk = dm * sg
            dy_ref[:, k * D:(k + 1) * D] = dyk
            dgl_ref[:, k * D:(k + 1) * D] = dyk * y * (1.0 - sg)
            dcat_ref[:, k * BR_W:(k + 1) * BR_W] = lax.dot_general(
                dyk, w, (((1,), (1,)), ((), ())), preferred_element_type=F32).astype(BF16)

    wspec = pl.BlockSpec((BR_W, D), lambda i: (0, 0))
    return pl.pallas_call(
        body, name=name, grid=(T // tm,),
        in_specs=[pl.BlockSpec((tm, D), lambda i: (i, 0)),
                  pl.BlockSpec((tm, 3 * BR_W), lambda i: (i, 0)),
                  pl.BlockSpec((tm, GL), lambda i: (i, glb)), wspec, wspec, wspec],
        out_specs=(pl.BlockSpec((tm, GL), lambda i: (i, 0)), pl.BlockSpec((tm, GL), lambda i: (i, 0)),
                   pl.BlockSpec((tm, 3 * BR_W), lambda i: (i, 0))),
        out_shape=(jax.ShapeDtypeStruct((T, GL), BF16), jax.ShapeDtypeStruct((T, GL), BF16),
                   jax.ShapeDtypeStruct((T, 3 * BR_W), BF16)),
        compiler_params=_cparams(("parallel",)),
    )(dmerged, cat, z, w_pa, w_pb, w_pc)


def _loss_head(x, target, g, name):
    T, D = x.shape
    tm = _pick(T, 512, SUBLANE)

    def body(x_ref, t_ref, g_ref, dx_ref, part_ref):
        i = pl.program_id(0)
        xv = x_ref[...]
        gv = g_ref[...]
        rstd = lax.rsqrt(jnp.mean(xv * xv, axis=-1, keepdims=True) + EPS)
        xhat = xv * rstd
        err = xhat * gv - t_ref[...]
        dy = err * (1.0 / D)
        dxhat = dy * gv
        dx_ref[...] = rstd * (dxhat - xhat * jnp.mean(dxhat * xhat, axis=-1, keepdims=True))

        @pl.when(i == 0)
        def _():
            part_ref[...] = jnp.zeros_like(part_ref)

        part_ref[0:1, :] += jnp.sum(dy * xhat, axis=0, keepdims=True)
        part_ref[1:2, :] += jnp.zeros((1, D), F32) + (0.5 / D) * jnp.sum(err * err)

    row = pl.BlockSpec((tm, D), lambda i: (i, 0))
    return pl.pallas_call(
        body, name=name, grid=(T // tm,),
        in_specs=[row, row, pl.BlockSpec((1, D), lambda i: (0, 0))],
        out_specs=(row, pl.BlockSpec((SUBLANE, D), lambda i: (0, 0))),
        out_shape=(jax.ShapeDtypeStruct((T, D), F32), jax.ShapeDtypeStruct((SUBLANE, D), F32)),
        compiler_params=_cparams(("arbitrary",)),
    )(x, target, g)


def _mod_fwd(c_all, w_mod, b_cols, name):
    L, D, N4 = w_mod.shape
    Bg = c_all.shape[0]
    tn = _pick(N4, 768)

    def body(c_ref, w_ref, b_ref, o_ref):
        cv = c_ref[...]
        ca = (cv * _sigmoid(cv)).astype(BF16)
        o_ref[...] = jnp.dot(ca, w_ref[...].astype(BF16), preferred_element_type=F32) + b_ref[...]

    return pl.pallas_call(
        body, name=name, grid=(L, N4 // tn),
        in_specs=[pl.BlockSpec((Bg, D), lambda l, j: (0, 0)),
                  pl.BlockSpec((None, D, tn), lambda l, j: (l, 0, j)),
                  pl.BlockSpec((None, 1, tn), lambda l, j: (l, 0, j))],
        out_specs=pl.BlockSpec((None, Bg, tn), lambda l, j: (l, 0, j)),
        out_shape=jax.ShapeDtypeStruct((L, Bg, N4), F32),
        compiler_params=_cparams(("parallel", "parallel")),
    )(c_all, w_mod, b_cols)


def _mod_wgrad(c_all, dmod_cols, name, deps=()):
    L, Bg, N4 = dmod_cols.shape
    D = c_all.shape[1]
    tn = _pick(N4, 768)

    def body(c_ref, d_ref, *rest):
        cv = c_ref[...]
        ca = (cv * _sigmoid(cv)).astype(BF16)
        rest[-1][...] = lax.dot_general(ca, d_ref[...].astype(BF16), (((0,), (0,)), ((), ())),
                                     preferred_element_type=F32)

    return pl.pallas_call(
        body, name=name, grid=(L, N4 // tn),
        in_specs=[pl.BlockSpec((Bg, D), lambda l, j: (0, 0)),
                  pl.BlockSpec((None, Bg, tn), lambda l, j: (l, 0, j))] + [ANY] * len(deps),
        out_specs=pl.BlockSpec((None, D, tn), lambda l, j: (l, 0, j)),
        out_shape=jax.ShapeDtypeStruct((L, D, N4), F32),
        compiler_params=_cparams(("parallel", "parallel")),
    )(c_all, dmod_cols, *deps)


def _rows_tile(R, C):
    return _pick(R, max(SUBLANE, (512 * 1024) // C), SUBLANE)


def _cast_into_full(w, layer, col_sharded, chip, name, deps=()):
    L, R, C = w.shape
    K, N = (R, C * N_CHIP) if col_sharded else (R * N_CHIP, C)
    tr = _pick(R, max(HALO, (512 * 1024) // C), HALO)
    nb = R // tr

    def body(q_ref, w_ref, *rest):
        rest[-1][...] = w_ref[...].astype(BF16)

    out_idx = (lambda i, q: (i, q[0])) if col_sharded else (lambda i, q: (q[0] * nb + i, 0))
    grid_spec = pltpu.PrefetchScalarGridSpec(
        num_scalar_prefetch=1, grid=(nb,),
        in_specs=[pl.BlockSpec((None, tr, C), lambda i, q: (layer, i, 0))] + [ANY] * len(deps),
        out_specs=pl.BlockSpec((tr, C), out_idx))
    return pl.pallas_call(
        body, name=name, grid_spec=grid_spec, out_shape=jax.ShapeDtypeStruct((K, N), BF16),
        compiler_params=_cparams(("arbitrary",)),
    )(chip.reshape(1).astype(jnp.int32), w, *deps)


def _sum_into(stack, grad, slots, layer, col_sharded, chip, name):
    _, R, C = slots.shape
    tr = _rows_tile(R, C)
    nb = R // tr

    def body(q_ref, stack_ref, g_ref, s_ref, o_ref):
        o_ref[...] = ((g_ref[...].astype(F32) + s_ref[0].astype(F32)) + s_ref[1].astype(F32)) + s_ref[2].astype(F32)

    g_idx = (lambda i, q: (i, q[0])) if col_sharded else (lambda i, q: (q[0] * nb + i, 0))
    grid_spec = pltpu.PrefetchScalarGridSpec(
        num_scalar_prefetch=1, grid=(nb,),
        in_specs=[ANY, pl.BlockSpec((tr, C), g_idx), pl.BlockSpec((3, tr, C), lambda i, q: (0, i, 0))],
        out_specs=pl.BlockSpec((tr, C), lambda i, q: (layer * nb + i, 0)))
    return pl.pallas_call(
        body, name=name, grid_spec=grid_spec, out_shape=jax.ShapeDtypeStruct(stack.shape, F32),
        input_output_aliases={1: 0}, compiler_params=_cparams(("arbitrary",)),
    )(chip.reshape(1).astype(jnp.int32), stack, grad, slots)


def _sum_devices(parts, name):
    n, R, C = parts.shape
    tr = _rows_tile(R, C)

    def body(s_ref, o_ref):
        acc = s_ref[0]
        for d in range(1, n):
            acc = acc + s_ref[d]
        o_ref[...] = acc

    return pl.pallas_call(
        body, name=name, grid=(R // tr,),
        in_specs=[pl.BlockSpec((n, tr, C), lambda i: (0, i, 0))],
        out_specs=pl.BlockSpec((tr, C), lambda i: (i, 0)),
        out_shape=jax.ShapeDtypeStruct((R, C), F32), compiler_params=_cparams(("parallel",)),
    )(parts)


def _adamw(w, m, v, grads, name):
    R, C = w.shape
    tr = _rows_tile(R, C)
    bc1 = 1.0 - ADAM_B1 ** ADAM_STEP
    bc2 = 1.0 - ADAM_B2 ** ADAM_STEP
    ng = len(grads)

    def body(*refs):
        w_ref, m_ref, v_ref = refs[:3]
        g_refs = refs[3:3 + ng]
        g_out, d_out, m_out, v_out = refs[3 + ng:]
        g = g_refs[0][...]
        for r in g_refs[1:]:
            g = g + r[...]
        mn = ADAM_B1 * m_ref[...] + (1.0 - ADAM_B1) * g
        vn = ADAM_B2 * v_ref[...] + (1.0 - ADAM_B2) * (g * g)
        m_hat = mn / bc1
        v_hat = vn / bc2
        g_out[...] = g
        d_out[...] = -ADAM_LR * (m_hat / (jnp.sqrt(v_hat) + ADAM_EPS) + ADAM_WD * w_ref[...])
        m_out[...] = mn
        v_out[...] = vn

    spec = pl.BlockSpec((tr, C), lambda i: (i, 0))
    sds = jax.ShapeDtypeStruct((R, C), F32)
    return pl.pallas_call(
        body, name=name, grid=(R // tr,), in_specs=[spec] * (3 + ng), out_specs=(spec,) * 4,
        out_shape=(sds,) * 4, compiler_params=_cparams(("parallel",)),
    )(w, m, v, *grads)


def _place():
    x, y, c = lax.axis_index("x"), lax.axis_index("y"), lax.axis_index("c")
    chips = [(1 - x, y), (x, 1 - y), (1 - x, 1 - y)]
    return x, y, c, chips


ANY = pl.BlockSpec(memory_space=pl.ANY)


def _allgather8(v, name):
    m_per, n = v.shape

    def body(x_ref, out_ref, send_sems, recv_sems, local_sem):
        x, y, c, chips = _place()
        me, sibling = (x, y, c), (x, y, 1 - c)

        def rows(px, py, pc):
            return out_ref.at[pl.ds((4 * px + 2 * py + pc) * m_per, m_per), :]

        def copy(k, block, to, src=None):
            return pltpu.make_async_remote_copy(
                src_ref=rows(*block) if src is None else src, dst_ref=rows(*block),
                send_sem=send_sems.at[k], recv_sem=recv_sems.at[k], device_id=to, device_id_type=MESH)

        mine = pltpu.make_async_copy(x_ref, rows(*me), local_sem)
        mine.start()
        first = [copy(0, me, sibling, src=x_ref)]
        first += [copy(1 + j, me, (*chip, c), src=x_ref) for j, chip in enumerate(chips)]
        for cp in first:
            cp.start()
        passed = [copy(4 + j, (*chip, c), sibling) for j, chip in enumerate(chips)]
        for j, chip in enumerate(chips):
            copy(1 + j, (*chip, c), me).wait_recv()
            passed[j].start()
        copy(0, sibling, me).wait_recv()
        for j, chip in enumerate(chips):
            copy(4 + j, (*chip, 1 - c), me).wait_recv()
        for cp in first + passed:
            cp.wait_send()
        mine.wait()

    return pl.pallas_call(
        body, name=name, out_shape=jax.ShapeDtypeStruct((N_DEV * m_per, n), v.dtype),
        in_specs=[ANY], out_specs=ANY,
        scratch_shapes=[pltpu.SemaphoreType.DMA((7,)), pltpu.SemaphoreType.DMA((7,)), pltpu.SemaphoreType.DMA],
    )(v)


def _window(ref, col_sharded, q, lead):
    full = (slice(None),) * lead
    if col_sharded:
        width = ref.shape[-1] // N_CHIP
        return ref.at[full + (slice(None), pl.ds(pl.multiple_of(q * width, LANE), width))]
    height = ref.shape[-2] // N_CHIP
    return ref.at[full + (pl.ds(pl.multiple_of(q * height, HALO), height), slice(None))]


HBM = pl.BlockSpec(memory_space=pltpu.HBM)
SEM = pl.BlockSpec(memory_space=pltpu.SEMAPHORE)
EFFECT = pltpu.SideEffectType.DATAFLOW_SIDE_EFFECTING


def _in_hbm(v):
    return pltpu.with_memory_space_constraint(v, pltpu.HBM)


def _gather_start(bufs, col_sharded, name):
    n = len(bufs)

    def body(*refs):
        ins = refs[:n]
        send_sems, recv_sems = refs[n], refs[n + 1]
        token = refs[-1]
        x, y, c, chips = _place()
        q = 2 * x + y
        for w in range(n):
            for k, chip in enumerate(chips):
                pltpu.make_async_remote_copy(
                    src_ref=_window(ins[w], col_sharded[w], q, 0), dst_ref=_window(ins[w], col_sharded[w], q, 0),
                    send_sem=send_sems.at[3 * w + k], recv_sem=recv_sems.at[3 * w + k],
                    device_id=(*chip, c), device_id_type=MESH).start()
        token[...] = jnp.zeros_like(token)

    out = pl.pallas_call(
        body, name=name,
        out_shape=(pltpu.SemaphoreType.DMA((3 * n,)), pltpu.SemaphoreType.DMA((3 * n,)),
                   *[pltpu.HBM(b.shape, b.dtype) for b in bufs], jax.ShapeDtypeStruct((SUBLANE, LANE), F32)),
        in_specs=(HBM,) * n, out_specs=(SEM, SEM) + (HBM,) * n + (pl.BlockSpec(memory_space=pltpu.VMEM),),
        input_output_aliases={w: 2 + w for w in range(n)},
        compiler_params=pltpu.CompilerParams(has_side_effects=EFFECT),
    )(*[_in_hbm(b) for b in bufs])
    return out[0], out[1], list(out[2:2 + n]), out[-1]


def _gather_wait(send_sems, recv_sems, bufs, col_sharded, after, name):
    n = len(bufs)

    def body(*refs):
        ins = refs[:n]
        send_sems, recv_sems = refs[n], refs[n + 1]
        x, y, c, chips = _place()
        q = 2 * x + y
        for w in range(n):
            for k, (cx, cy) in enumerate(chips):
                cp = pltpu.make_async_remote_copy(
                    src_ref=_window(ins[w], col_sharded[w], q, 0),
                    dst_ref=_window(ins[w], col_sharded[w], 2 * cx + cy, 0),
                    send_sem=send_sems.at[3 * w + k], recv_sem=recv_sems.at[3 * w + k],
                    device_id=(cx, cy, c), device_id_type=MESH)
                cp.wait_send()
                cp.wait_recv()

    out = pl.pallas_call(
        body, name=name, out_shape=tuple(pltpu.HBM(b.shape, b.dtype) for b in bufs),
        in_specs=(HBM,) * n + (SEM, SEM) + (ANY,) * len(after), out_specs=(HBM,) * n,
        input_output_aliases={w: w for w in range(n)},
        compiler_params=pltpu.CompilerParams(has_side_effects=EFFECT),
    )(*bufs, send_sems, recv_sems, *after)
    return list(out)


def _scatter_start(grads, col_sharded, name):
    n = len(grads)
    lands = []
    for g, cs in zip(grads, col_sharded):
        K, N = g.shape
        lands.append(lax.empty((3, K, N // N_CHIP) if cs else (3, K // N_CHIP, N), BF16))

    def body(*refs):
        ins, slots = refs[:n], refs[n:2 * n]
        send_sems, recv_sems = refs[2 * n], refs[2 * n + 1]
        token = refs[-1]
        x, y, c, chips = _place()
        for w in range(n):
            for k, (cx, cy) in enumerate(chips):
                pltpu.make_async_remote_copy(
                    src_ref=_window(ins[w], col_sharded[w], 2 * cx + cy, 0), dst_ref=slots[w].at[k],
                    send_sem=send_sems.at[3 * w + k], recv_sem=recv_sems.at[3 * w + k],
                    device_id=(cx, cy, c), device_id_type=MESH).start()
        token[...] = jnp.zeros_like(token)

    out = pl.pallas_call(
        body, name=name,
        out_shape=(pltpu.SemaphoreType.DMA((3 * n,)), pltpu.SemaphoreType.DMA((3 * n,)),
                   *[pltpu.HBM(b.shape, b.dtype) for b in grads], *[pltpu.HBM(b.shape, b.dtype) for b in lands],
                   jax.ShapeDtypeStruct((SUBLANE, LANE), F32)),
        in_specs=(HBM,) * (2 * n),
        out_specs=(SEM, SEM) + (HBM,) * (2 * n) + (pl.BlockSpec(memory_space=pltpu.VMEM),),
        input_output_aliases={w: 2 + w for w in range(2 * n)},
        compiler_params=pltpu.CompilerParams(has_side_effects=EFFECT),
    )(*[_in_hbm(b) for b in grads], *[_in_hbm(b) for b in lands])
    return out[0], out[1], list(out[2:2 + n]), list(out[2 + n:2 + 2 * n]), out[-1]


def _scatter_wait(send_sems, recv_sems, grads, lands, col_sharded, after, name):
    n = len(grads)

    def body(*refs):
        ins, slots = refs[:n], refs[n:2 * n]
        send_sems, recv_sems = refs[2 * n], refs[2 * n + 1]
        x, y, c, chips = _place()
        for w in range(n):
            for k, (cx, cy) in enumerate(chips):
                cp = pltpu.make_async_remote_copy(
                    src_ref=_window(ins[w], col_sharded[w], 2 * cx + cy, 0), dst_ref=slots[w].at[k],
                    send_sem=send_sems.at[3 * w + k], recv_sem=recv_sems.at[3 * w + k],
                    device_id=(cx, cy, c), device_id_type=MESH)
                cp.wait_send()
                cp.wait_recv()

    out = pl.pallas_call(
        body, name=name, out_shape=tuple(pltpu.HBM(b.shape, b.dtype) for b in list(grads) + list(lands)),
        in_specs=(HBM,) * (2 * n) + (SEM, SEM) + (ANY,) * len(after), out_specs=(HBM,) * (2 * n),
        input_output_aliases={w: w for w in range(2 * n)},
        compiler_params=pltpu.CompilerParams(has_side_effects=EFFECT),
    )(*grads, *lands, send_sems, recv_sems, *after)
    return list(out[:n]), list(out[n:])


def _swap_start(arrays, name):
    n = len(arrays)
    lands = [lax.empty(a.shape, a.dtype) for a in arrays]

    def body(*refs):
        ins, lnd = refs[:n], refs[n:2 * n]
        send_sems, recv_sems = refs[2 * n], refs[2 * n + 1]
        token = refs[-1]
        x, y, c, _ = _place()
        for w in range(n):
            pltpu.make_async_remote_copy(
                src_ref=ins[w], dst_ref=lnd[w], send_sem=send_sems.at[w], recv_sem=recv_sems.at[w],
                device_id=(x, y, 1 - c), device_id_type=MESH).start()
        token[...] = jnp.zeros_like(token)

    out = pl.pallas_call(
        body, name=name,
        out_shape=(pltpu.SemaphoreType.DMA((n,)), pltpu.SemaphoreType.DMA((n,)),
                   *[pltpu.HBM(b.shape, b.dtype) for b in arrays], *[pltpu.HBM(b.shape, b.dtype) for b in lands],
                   jax.ShapeDtypeStruct((SUBLANE, LANE), F32)),
        in_specs=(HBM,) * (2 * n),
        out_specs=(SEM, SEM) + (HBM,) * (2 * n) + (pl.BlockSpec(memory_space=pltpu.VMEM),),
        input_output_aliases={w: 2 + w for w in range(2 * n)},
        compiler_params=pltpu.CompilerParams(has_side_effects=EFFECT),
    )(*[_in_hbm(b) for b in arrays], *[_in_hbm(b) for b in lands])
    return out[0], out[1], list(out[2:2 + n]), list(out[2 + n:2 + 2 * n]), out[-1]


def _swap_wait(send_sems, recv_sems, arrays, lands, after, name):
    n = len(arrays)

    def body(*refs):
        ins, lnd = refs[:n], refs[n:2 * n]
        send_sems, recv_sems = refs[2 * n], refs[2 * n + 1]
        x, y, c, _ = _place()
        for w in range(n):
            cp = pltpu.make_async_remote_copy(
                src_ref=ins[w], dst_ref=lnd[w], send_sem=send_sems.at[w], recv_sem=recv_sems.at[w],
                device_id=(x, y, 1 - c), device_id_type=MESH)
            cp.wait_send()
            cp.wait_recv()

    out = pl.pallas_call(
        body, name=name, out_shape=tuple(pltpu.HBM(b.shape, b.dtype) for b in list(arrays) + list(lands)),
        in_specs=(HBM,) * (2 * n) + (SEM, SEM) + (ANY,) * len(after), out_specs=(HBM,) * (2 * n),
        input_output_aliases={w: w for w in range(2 * n)},
        compiler_params=pltpu.CompilerParams(has_side_effects=EFFECT),
    )(*arrays, *lands, send_sems, recv_sems, *after)
    return list(out[:n]), list(out[n:])


BIG = ("w_in", "w_pa", "w_pb", "w_pc", "w_o", "w_13", "w_2")
BIG_COL_SHARDED = (True, True, True, True, False, True, False)
SMALL = ("b_mod", "g_mix", "gm_ln_g", "gm_ln_b", "gm_w_s", "gm_b_s", "pool_w", "pool_scale", "conv_w",
         "g_ffn", "g_final")
WEIGHTS = ("w_mod", "b_mod", "g_mix", "w_in", "gm_ln_g", "gm_ln_b", "gm_w_s", "gm_b_s", "w_pa", "pool_w",
           "pool_scale", "w_pb", "conv_w", "w_pc", "w_o", "g_ffn", "w_13", "w_2", "g_final")


def _pack(arrays, width):
    flat = jnp.concatenate([a.reshape(-1) for a in arrays])
    rows = -(-flat.shape[0] // width)
    rows = -(-rows // SUBLANE) * SUBLANE
    flat = jnp.pad(flat, (0, rows * width - flat.shape[0]))
    return flat.reshape(rows, width)


def _unpack(packed, shapes):
    flat = packed.reshape(-1)
    out, off = [], 0
    for s in shapes:
        size = 1
        for d in s:
            size *= d
        out.append(flat[off:off + size].reshape(s))
        off += size
    return out


def kernel(x, c, w_mod, b_mod, g_mix, w_in, gm_ln_g, gm_ln_b, gm_w_s, gm_b_s, w_pa, pool_w, pool_scale, w_pb, conv_w, w_pc, w_o, g_ffn, w_13, w_2, g_final, loss_target, m_w_mod, m_b_mod, m_g_mix, m_w_in, m_gm_ln_g, m_gm_ln_b, m_gm_w_s, m_gm_b_s, m_w_pa, m_pool_w, m_pool_scale, m_w_pb, m_conv_w, m_w_pc, m_w_o, m_g_ffn, m_w_13, m_w_2, m_g_final, v_w_mod, v_b_mod, v_g_mix, v_w_in, v_gm_ln_g, v_gm_ln_b, v_gm_w_s, v_gm_b_s, v_w_pa, v_pool_w, v_pool_scale, v_w_pb, v_conv_w, v_w_pc, v_w_o, v_g_ffn, v_w_13, v_w_2, v_g_final):
    W = dict(w_mod=w_mod, b_mod=b_mod, g_mix=g_mix, w_in=w_in, gm_ln_g=gm_ln_g, gm_ln_b=gm_ln_b, gm_w_s=gm_w_s,
             gm_b_s=gm_b_s, w_pa=w_pa, pool_w=pool_w, pool_scale=pool_scale, w_pb=w_pb, conv_w=conv_w, w_pc=w_pc,
             w_o=w_o, g_ffn=g_ffn, w_13=w_13, w_2=w_2, g_final=g_final)
    Mo = dict(w_mod=m_w_mod, b_mod=m_b_mod, g_mix=m_g_mix, w_in=m_w_in, gm_ln_g=m_gm_ln_g, gm_ln_b=m_gm_ln_b,
              gm_w_s=m_gm_w_s, gm_b_s=m_gm_b_s, w_pa=m_w_pa, pool_w=m_pool_w, pool_scale=m_pool_scale, w_pb=m_w_pb,
              conv_w=m_conv_w, w_pc=m_w_pc, w_o=m_w_o, g_ffn=m_g_ffn, w_13=m_w_13, w_2=m_w_2, g_final=m_g_final)
    Vo = dict(w_mod=v_w_mod, b_mod=v_b_mod, g_mix=v_g_mix, w_in=v_w_in, gm_ln_g=v_gm_ln_g, gm_ln_b=v_gm_ln_b,
              gm_w_s=v_gm_w_s, gm_b_s=v_gm_b_s, w_pa=v_w_pa, pool_w=v_pool_w, pool_scale=v_pool_scale, w_pb=v_w_pb,
              conv_w=v_conv_w, w_pc=v_w_pc, w_o=v_w_o, g_ffn=v_g_ffn, w_13=v_w_13, w_2=v_w_2, g_final=v_g_final)

    B, S, D = x.shape
    T = B * S
    L = w_in.shape[0]
    Bg = B * N_DEV
    N4 = w_mod.shape[2]
    CW = conv_w.shape[2]
    xi, yi, ci = lax.axis_index("x"), lax.axis_index("y"), lax.axis_index("c")
    chip = 2 * xi + yi
    dev = 2 * chip + ci

    head = _pack([c, conv_w], D)
    hrows = head.shape[0]
    got = _allgather8(head, "gather_c_conv").reshape(N_DEV, hrows * D)
    c_all = got[:, :B * D].reshape(Bg, D)
    conv_parts = got[:, B * D:B * D + L * 3 * CW].reshape(N_CHIP, 2, L, 3, CW)[:, 0]
    conv_full = jnp.transpose(conv_parts, (1, 2, 0, 3)).reshape(L, 3, N_CHIP * CW)

    b_cols = lax.dynamic_slice_in_dim(b_mod, chip * N4, N4, axis=1).reshape(L, 1, N4)
    mod_part = _mod_fwd(c_all, w_mod, b_cols, "mod_fwd")
    half = Bg // 2
    mine = lax.dynamic_slice_in_dim(mod_part, ci * half, half, axis=1)
    mod_got = _allgather8(mine.reshape(L * half, N4), "gather_mod").reshape(N_CHIP, 2, L, half, N4)
    mod_full = jnp.transpose(mod_got, (2, 1, 3, 0, 4)).reshape(L, Bg, 6, D)
    mod_mine = lax.dynamic_slice_in_dim(mod_full, dev * B, B, axis=1)
    mod = jnp.pad(mod_mine, ((0, 0), (0, 0), (0, SUBLANE - 6), (0, 0)))

    PARTS = (("w_in",), ("w_pa", "w_pb", "w_pc", "w_o"), ("w_13", "w_2"))
    groups = [(names, l) for l in range(L) for names in PARTS]
    gathers = []
    order = (mod,)
    for gi, (names, l) in enumerate(groups):
        cs = [BIG_COL_SHARDED[BIG.index(n)] for n in names]
        bufs = [_cast_into_full(W[n], l, c_, chip, "cast_" + n, deps=order) for n, c_ in zip(names, cs)]
        ss, rs, bufs, tok = _gather_start(bufs, cs, f"gather_start_{gi}")
        gathers.append((ss, rs, bufs, names, cs))
        order = (tok,)

    def gathered(gi, after):
        ss, rs, bufs, names, cs = gathers[gi]
        return dict(zip(names, _gather_wait(ss, rs, bufs, cs, after, f"gather_wait_{gi}")))

    def mixer_params(l):
        vec = jnp.zeros((SUBLANE, BR_W), F32)
        vec = vec.at[0].set(gm_ln_g[l]).at[1].set(gm_ln_b[l]).at[2].set(pool_scale[l])
        conv = jnp.zeros((SUBLANE, BR_W), F32).at[0:3].set(conv_full[l])
        b_s = jnp.zeros((CHUNK, LANE), F32).at[:, 0:HEADS].set(jnp.transpose(gm_b_s[l]))
        return dict(vec=vec, conv=conv, w_s=gm_w_s[l], b_s=b_s, pool_w=pool_w[l])

    xs = x.reshape(T, D)
    saved = []
    for l in range(L):
        prm = mixer_params(l)
        full = gathered(3 * l, list(order) if l == 0 else [xs])
        z = _normed_matmul(xs, g_mix[l].reshape(1, D), mod[l], 0, 1, full["w_in"], S, "mm_in",
                           deps=order if l == 0 else ())
        cat = _mixer_fwd(z, prm, S, "mixer_fwd")
        full.update(gathered(3 * l + 1, [cat]))
        merged = _proj_fwd(cat, z, full["w_pa"], full["w_pb"], full["w_pc"], D, "proj_fwd")
        x1, mo = _matmul(merged, full["w_o"], mode="nn", name="mm_o", resid=(xs, mod[l], 2, S))
        full.update(gathered(3 * l + 2, [x1]))
        ga, gb, act = _normed_matmul(x1, g_ffn[l].reshape(1, D), mod[l], 3, 4, full["w_13"], S, "mm_13", swiglu=True)
        x2, ffo = _matmul(act, full["w_2"], mode="nn", name="mm_2", resid=(x1, mod[l], 5, S))
        saved.append(dict(prm=prm, full=full, x0=xs, z=z, cat=cat, merged=merged, mo=mo, x1=x1, ga=ga, gb=gb, act=act,
                          ffo=ffo))
        xs = x2

    dx, head_part = _loss_head(xs, loss_target.reshape(T, D), g_final.reshape(1, D), "loss_head")
    loss = lax.psum(head_part[1, 0], ("x", "y", "c"))

    big_grads = {n: [None] * L for n in BIG}
    small_part = {n: [None] * L for n in SMALL if n not in ("b_mod", "g_final")}
    dmod = [None] * L
    scatters = []
    FFN, MIX = ("w_13", "w_2"), ("w_in", "w_pa", "w_pb", "w_pc", "w_o")

    def scatter(names, l, tag):
        cs = [BIG_COL_SHARDED[BIG.index(n)] for n in names]
        ss, rs, gthru, lands, tok = _scatter_start([big_grads[n][l] for n in names], cs, f"scatter_start_{l}{tag}")
        scatters.append((ss, rs, gthru, lands, names, cs, l))
        return (tok,)

    def exchange_small():
        dmod_l = jnp.stack(dmod, axis=0).reshape(L * B, 6 * D)
        rows = -(-(L * B) // SUBLANE) * SUBLANE
        dmod_got = _allgather8(jnp.pad(dmod_l, ((0, rows - L * B), (0, 0))), "gather_dmod")
        local = dict(b_mod=jnp.sum(jnp.stack(dmod, axis=0), axis=1).reshape(L, 6 * D), g_final=head_part[0])
        for n in small_part:
            local[n] = jnp.stack(small_part[n], axis=0)
        packed = _pack([local[n] for n in SMALL], LANE)
        return dmod_got, rows, _allgather8(packed, "gather_small"), [local[n].shape for n in SMALL]

    sent = ()
    for l in reversed(range(L)):
        sv = saved[l]
        full = sv["full"]
        dffo, pg2, da, db = _gated_dgrad(dx, sv["ffo"], mod[l], 5, full["w_2"], S, "mm_2_dgrad",
                                         gab=(sv["ga"], sv["gb"]), deps=sent)
        dx1, h2, pb2, pgf = _dgrad_norm_bwd([da, db], full["w_13"], sv["x1"], dx, g_ffn[l].reshape(1, D), mod[l],
                                            3, 4, S, "mm_13_dgrad")
        big_grads["w_2"][l] = _matmul(sv["act"], dffo, mode="tn", name="mm_2_wgrad")
        big_grads["w_13"][l] = _wgrad_pair(h2, da, db, "mm_13_wgrad")
        sent = scatter(FFN, l, "a")

        dmo, pg1, dmerged = _gated_dgrad(dx1, sv["mo"], mod[l], 2, full["w_o"], S, "mm_o_dgrad", deps=sent)
        big_grads["w_o"][l] = _matmul(sv["merged"], dmo, mode="tn", name="mm_o_wgrad")
        dy, dgl, dcat = _proj_bwd(dmerged, sv["cat"], sv["z"], full["w_pa"], full["w_pb"], full["w_pc"], D,
                                  "proj_bwd")
        for k, n in enumerate(("w_pa", "w_pb", "w_pc")):
            big_grads[n][l] = _matmul(sv["cat"], dy, mode="tn", name="mm_proj_wgrad",
                                      a_cols=(k * BR_W, BR_W), b_cols=(k * D, D))
        dz, pv, dws, dbs, dpw = _mixer_bwd(sv["z"], dcat, dgl, sv["prm"], S, "mixer_bwd")
        dx0, h, pb1, pgm = _dgrad_norm_bwd([dz], full["w_in"], sv["x0"], dx1, g_mix[l].reshape(1, D), mod[l],
                                           0, 1, S, "mm_in_dgrad")
        dx = dx0

        dmod[l] = jnp.stack([pb1[:, 0], pb1[:, 1], pg1[:, 0], pb2[:, 0], pb2[:, 1], pg2[:, 0]], axis=1)
        small_part["g_mix"][l] = pgm[0]
        small_part["g_ffn"][l] = pgf[0]
        small_part["gm_ln_g"][l] = pv[0]
        small_part["gm_ln_b"][l] = pv[1]
        small_part["pool_scale"][l] = pv[2]
        small_part["conv_w"][l] = pv[3:6]
        small_part["gm_w_s"][l] = dws
        small_part["gm_b_s"][l] = jnp.transpose(dbs[:, 0:HEADS])
        small_part["pool_w"][l] = dpw

        if l == 0:
            dmod_got, dmod_rows, small_got, pshapes = exchange_small()
            last = (dmod_got, small_got)
        else:
            last = ()
        big_grads["w_in"][l] = _matmul(h, dz, mode="tn", name="mm_in_wgrad", deps=last)
        sent = scatter(MIX, l, "b")
    grad_x = dx.reshape(B, S, D)

    results = {}

    dmod_all = dmod_got.reshape(N_DEV, dmod_rows, 6 * D)[:, :L * B].reshape(N_DEV, L, B, 6 * D)
    dmod_all = jnp.transpose(dmod_all, (1, 0, 2, 3)).reshape(L, Bg, 6 * D)
    dmod_cols = lax.dynamic_slice_in_dim(dmod_all, chip * N4, N4, axis=2)
    g_wmod = _mod_wgrad(c_all, dmod_cols, "mod_wgrad", deps=sent)
    res = _adamw(w_mod.reshape(L * D, N4), m_w_mod.reshape(L * D, N4), v_w_mod.reshape(L * D, N4),
                 [g_wmod.reshape(L * D, N4)], "adamw_w_mod")
    results["w_mod"] = [r.reshape(L, D, N4) for r in res]

    gathered_small = small_got.reshape(N_DEV, small_got.shape[0] // N_DEV, LANE)
    g_small = dict(zip(SMALL, _unpack(_sum_devices(gathered_small, "sum_small"), pshapes)))
    g_small["conv_w"] = lax.dynamic_slice_in_dim(g_small["conv_w"], chip * CW, CW, axis=2)
    wshapes = [W[n].shape for n in SMALL]
    res = _adamw(_pack([W[n] for n in SMALL], LANE), _pack([Mo[n] for n in SMALL], LANE),
                 _pack([Vo[n] for n in SMALL], LANE), [_pack([g_small[n] for n in SMALL], LANE)], "adamw_small")
    small_res = [_unpack(r, wshapes) for r in res]
    for i, n in enumerate(SMALL):
        results[n] = [small_res[j][i] for j in range(4)]

    stacks = {n: lax.empty((W[n].shape[0] * W[n].shape[1], W[n].shape[2]), F32) for n in BIG}
    after = [res[0], results["w_mod"][0]]
    swaps = []
    for part in (FFN, MIX):
        for ss, rs, gthru, lands, names, cs, l in scatters:
            if names != part:
                continue
            gthru, lands = _scatter_wait(ss, rs, gthru, lands, cs, after, f"scatter_wait_{l}_{names[0]}")
            for n, g, ld, c_ in zip(names, gthru, lands, cs):
                stacks[n] = _sum_into(stacks[n], g, ld, l, c_, chip, "sum_" + n)
            after = [stacks[names[-1]]]
        ss, rs, mine, theirs, tok = _swap_start([stacks[n] for n in part], f"swap_start_{part[0]}")
        swaps.append((ss, rs, mine, theirs, part))
        after = after + [tok]
    for ss, rs, mine, theirs, part in swaps:
        mine, theirs = _swap_wait(ss, rs, mine, theirs, after, f"swap_wait_{part[0]}")
        for n, own, other in zip(part, mine, theirs):
            _, R, C = W[n].shape
            res = _adamw(W[n].reshape(L * R, C), Mo[n].reshape(L * R, C), Vo[n].reshape(L * R, C), [own, other],
                         "adamw_" + n)
            results[n] = [r.reshape(L, R, C) for r in res]
        after = [res[0]]

    return (loss, grad_x, *[results[n][0] for n in WEIGHTS], *[results[n][1] for n in WEIGHTS],
            *[results[n][2] for n in WEIGHTS], *[results[n][3] for n in WEIGHTS])
```
